```python
import jax
import jax.numpy as jnp
from jax import lax
import numpy as np

D_MODEL = 2048
BATCH = 4
SEQ = 2048
DEPTH = 4

CHUNK = 64
N_A_LAYERS = DEPTH // 2
N_B_LAYERS = DEPTH - N_A_LAYERS
S5_GROUP = 16
S5_GROUPS = D_MODEL // S5_GROUP
S5_STATE = 64
DT_MIN = 1e-3
DT_MAX = 1e-1
B_HEADS = 16
B_HEAD_DIM = D_MODEL // B_HEADS
LEFT_CHUNKS = 8
BAND = LEFT_CHUNKS + 1
MAX_REL = 256
N_REL = MAX_REL + CHUNK
N_MEM = 256
MEM_HEADS = 4
MEM_HEAD_DIM = 128
MEM_DIM = MEM_HEADS * MEM_HEAD_DIM
D_FF = 5632
CONV_W = 3
EPS = 1e-6
NEG_INF = -1e30

kernel_name = "s5_chunkattn_yoco_hybrid_trunk"


def rmsnorm(x, g):
    xf = x.astype(jnp.float32)
    y = xf * lax.rsqrt(jnp.mean(xf * xf, axis=-1, keepdims=True) + EPS)
    return (y * g.astype(jnp.float32)).astype(x.dtype)


def _ssm_combine(e1, e2):
    a1r, a1i, b1r, b1i = e1
    a2r, a2i, b2r, b2i = e2
    ar = a2r * a1r - a2i * a1i
    ai = a2r * a1i + a2i * a1r
    br = a2r * b1r - a2i * b1i + b2r
    bi = a2r * b1i + a2i * b1r + b2i
    return ar, ai, br, bi


def s5_ssm(u, lam_re, lam_im, log_dt, b_re, b_im, c_re, c_im, d_skip):
    f32 = jnp.float32
    bsz, seq, _ = u.shape
    uf = u.astype(f32).reshape(bsz, seq, S5_GROUPS, S5_GROUP)
    lr = lam_re.astype(f32)
    li = lam_im.astype(f32)
    dt = jnp.exp(log_dt.astype(f32))[:, None]
    mag = jnp.exp(lr * dt)
    ang = li * dt
    ab_re = mag * jnp.cos(ang)
    ab_im = mag * jnp.sin(ang)
    den = lr * lr + li * li
    nr = ab_re - 1.0
    f_re = (nr * lr + ab_im * li) / den
    f_im = (ab_im * lr - nr * li) / den
    br = b_re.astype(f32)
    bi = b_im.astype(f32)
    bb_re = f_re[..., None] * br - f_im[..., None] * bi
    bb_im = f_re[..., None] * bi + f_im[..., None] * br
    bu_re = jnp.einsum("bsgh,gph->sbgp", uf, bb_re)
    bu_im = jnp.einsum("bsgh,gph->sbgp", uf, bb_im)
    a_re = jnp.broadcast_to(ab_re, (seq, 1) + ab_re.shape)
    a_im = jnp.broadcast_to(ab_im, (seq, 1) + ab_im.shape)
    _, _, x_re, x_im = lax.associative_scan(_ssm_combine, (a_re, a_im, bu_re, bu_im), axis=0)
    y = (jnp.einsum("sbgp,ghp->bsgh", x_re, c_re.astype(f32))
         - jnp.einsum("sbgp,ghp->bsgh", x_im, c_im.astype(f32)))
    y = y + d_skip.astype(f32).reshape(S5_GROUPS, S5_GROUP) * uf
    return y.reshape(bsz, seq, D_MODEL).astype(u.dtype)


def mixer_a(h, w_in, lam_re, lam_im, log_dt, b_re, b_im, c_re, c_im, d_skip, w_glu):
    u = h @ w_in
    y = jax.nn.gelu(s5_ssm(u, lam_re, lam_im, log_dt, b_re, b_im, c_re, c_im, d_skip))
    val, gate = jnp.split(y @ w_glu, 2, axis=-1)
    return val * jax.nn.sigmoid(gate)


def rel_bias_index():
    qi = np.arange(CHUNK)[:, None, None]
    slot = np.arange(BAND)[None, :, None]
    kj = np.arange(CHUNK)[None, None, :]
    dist = (LEFT_CHUNKS - slot) * CHUNK + qi - kj
    idx = np.clip(dist, -(CHUNK - 1), MAX_REL) + (CHUNK - 1)
    return idx.reshape(CHUNK, BAND * CHUNK)


def band_gather(t, n_chunks):
    bsz = t.shape[0]
    tc = t.reshape(bsz, n_chunks, CHUNK, B_HEADS, B_HEAD_DIM)
    tp = jnp.pad(tc, ((0, 0), (LEFT_CHUNKS, 0), (0, 0), (0, 0), (0, 0)))
    return jnp.concatenate([tp[:, i:i + n_chunks] for i in range(BAND)], axis=2)


def mixer_b(h, w_q, rel_bias, w_o, k_band, v_band, band_valid):
    bsz, seq, _ = h.shape
    n_chunks = seq // CHUNK
    q = (h @ w_q).reshape(bsz, n_chunks, CHUNK, B_HEADS, B_HEAD_DIM)
    scale = B_HEAD_DIM ** -0.5
    s = jnp.einsum("bnqhd,bnkhd->bnhqk", q.astype(jnp.float32), k_band.astype(jnp.float32)) * scale
    bias = rel_bias.astype(jnp.float32)[:, rel_bias_index()]
    s = s + bias[None, None]
    s = jnp.where(band_valid[None, :, None, None, :], s, NEG_INF)
    p = jax.nn.softmax(s, axis=-1).astype(v_band.dtype)
    o = jnp.einsum("bnhqk,bnkhd->bnqhd", p, v_band).reshape(bsz, seq, D_MODEL)
    return o @ w_o


def mem_attention(h, mem_n, w_q, w_kv, w_o):
    bsz, seq, _ = h.shape
    q = (h @ w_q).reshape(bsz, seq, MEM_HEADS, MEM_HEAD_DIM)
    k, v = jnp.split(mem_n @ w_kv, 2, axis=-1)
    k = k.reshape(bsz, -1, MEM_HEADS, MEM_HEAD_DIM)
    v = v.reshape(bsz, -1, MEM_HEADS, MEM_HEAD_DIM)
    s = jnp.einsum("bshd,bmhd->bhsm", q.astype(jnp.float32), k.astype(jnp.float32)) * (MEM_HEAD_DIM ** -0.5)
    p = jax.nn.softmax(s, axis=-1).astype(v.dtype)
    o = jnp.einsum("bhsm,bmhd->bshd", p, v).reshape(bsz, seq, MEM_DIM)
    return o @ w_o


def conv_ffn(h, w_up, conv_w, conv_b, w_down):
    seq = h.shape[1]
    up = h @ w_up
    upp = jnp.pad(up, ((0, 0), (CONV_W - 1, 0), (0, 0)))
    acc = conv_b
    for k in range(CONV_W):
        acc = acc + upp[:, k:k + seq] * conv_w[k]
    val, gate = jnp.split(acc, 2, axis=-1)
    return (val * jax.nn.silu(gate)) @ w_down


def setup_inputs(seed: int = 0) -> dict:
    key = jax.random.key(seed)
    ks = jax.random.split(key, 40)
    f32 = jnp.float32
    D, G, P, H = D_MODEL, S5_GROUPS, S5_STATE, S5_GROUP

    def nrm(k, shape, scale):
        return jax.random.normal(k, shape, f32) * scale

    def gain(k, shape):
        return 1.0 + 0.05 * jax.random.normal(k, shape, f32)

    lam_im0 = jnp.pi * jnp.arange(P, dtype=f32)
    inp = {
        "x": nrm(ks[0], (BATCH, SEQ, D), 1.0),
        "mem": nrm(ks[1], (BATCH, N_MEM, D), 1.0),
        "norm_mix": gain(ks[2], (DEPTH, 2, D)),
        "norm_mem": gain(ks[3], (DEPTH, 2, D)),
        "norm_ffn": gain(ks[4], (DEPTH, 2, D)),
        "mem_in_norm": gain(ks[5], (D,)),
        "a_w_in": nrm(ks[6], (N_A_LAYERS, D, D), D ** -0.5),
        "a_lam_re": -0.5 + 0.01 * jax.random.normal(ks[7], (N_A_LAYERS, G, P), f32),
        "a_lam_im": lam_im0 + 0.01 * jax.random.normal(ks[8], (N_A_LAYERS, G, P), f32),
        "a_log_dt": jax.random.uniform(ks[9], (N_A_LAYERS, G), f32, float(np.log(DT_MIN)), float(np.log(DT_MAX))),
        "a_b_re": nrm(ks[10], (N_A_LAYERS, G, P, H), (2 * H) ** -0.5),
        "a_b_im": nrm(ks[11], (N_A_LAYERS, G, P, H), (2 * H) ** -0.5),
        "a_c_re": nrm(ks[12], (N_A_LAYERS, G, H, P), P ** -0.5),
        "a_c_im": nrm(ks[13], (N_A_LAYERS, G, H, P), P ** -0.5),
        "a_d": nrm(ks[14], (N_A_LAYERS, D), 1.0),
        "a_w_glu": nrm(ks[15], (N_A_LAYERS, D, 2 * D), D ** -0.5),
        "kv_norm": gain(ks[16], (D,)),
        "w_k": nrm(ks[17], (D, D), D ** -0.5),
        "w_v": nrm(ks[18], (D, D), D ** -0.5),
        "b_w_q": nrm(ks[19], (N_B_LAYERS, D, D), D ** -0.5),
        "b_rel_bias": nrm(ks[20], (N_B_LAYERS, B_HEADS, N_REL), 0.1),
        "b_w_o": nrm(ks[21], (N_B_LAYERS, D, D), D ** -0.5),
        "m_w_q": nrm(ks[22], (DEPTH, D, MEM_DIM), D ** -0.5),
        "m_w_kv": nrm(ks[23], (DEPTH, D, 2 * MEM_DIM), D ** -0.5),
        "m_w_o": nrm(ks[24], (DEPTH, MEM_DIM, D), MEM_DIM ** -0.5),
        "f_w_up": nrm(ks[25], (DEPTH, D, 2 * D_FF), D ** -0.5),
        "f_conv_w": nrm(ks[26], (DEPTH, CONV_W, 2 * D_FF), CONV_W ** -0.5),
        "f_conv_b": nrm(ks[27], (DEPTH, 2 * D_FF), 0.01),
        "f_w_down": nrm(ks[28], (DEPTH, D_FF, D), D_FF ** -0.5),
    }
    return inp


def reference(x, mem, norm_mix, norm_mem, norm_ffn, mem_in_norm,
              a_w_in, a_lam_re, a_lam_im, a_log_dt, a_b_re, a_b_im, a_c_re, a_c_im, a_d, a_w_glu,
              kv_norm, w_k, w_v, b_w_q, b_rel_bias, b_w_o,
              m_w_q, m_w_kv, m_w_o, f_w_up, f_conv_w, f_conv_b, f_w_down):
    bsz, seq, _ = x.shape
    n_chunks = seq // CHUNK
    slot_chunk = np.arange(n_chunks)[:, None] - LEFT_CHUNKS + np.arange(BAND)[None, :]
    band_valid = jnp.asarray(np.repeat(slot_chunk >= 0, CHUNK, axis=1))
    mem_n = rmsnorm(mem, mem_in_norm)
    k_band = None
    v_band = None
    for l in range(DEPTH):
        if l == N_A_LAYERS:
            hk = rmsnorm(x, kv_norm)
            k = (hk @ w_k).reshape(bsz, seq, B_HEADS, B_HEAD_DIM)
            v = (hk @ w_v).reshape(bsz, seq, B_HEADS, B_HEAD_DIM)
            k_band = band_gather(k, n_chunks)
            v_band = band_gather(v, n_chunks)
        h = rmsnorm(x, norm_mix[l, 0])
        if l < N_A_LAYERS:
            m = mixer_a(h, a_w_in[l], a_lam_re[l], a_lam_im[l], a_log_dt[l], a_b_re[l], a_b_im[l],
                        a_c_re[l], a_c_im[l], a_d[l], a_w_glu[l])
        else:
            j = l - N_A_LAYERS
            m = mixer_b(h, b_w_q[j], b_rel_bias[j], b_w_o[j], k_band, v_band, band_valid)
        x = x + rmsnorm(m, norm_mix[l, 1])
        c = mem_attention(rmsnorm(x, norm_mem[l, 0]), mem_n, m_w_q[l], m_w_kv[l], m_w_o[l])
        x = x + rmsnorm(c, norm_mem[l, 1])
        f = conv_ffn(rmsnorm(x, norm_ffn[l, 0]), f_w_up[l], f_conv_w[l], f_conv_b[l], f_w_down[l])
        x = x + rmsnorm(f, norm_ffn[l, 1])
    return x
```

```python
import functools

import numpy as np
import jax
import jax.numpy as jnp
from jax import lax
from jax.experimental import pallas as pl
from jax.experimental.pallas import tpu as pltpu

EPS = 1e-6
NEG_INF = -1e30
BF16 = jnp.bfloat16
F32 = jnp.float32

S5_GROUP = 16
S5_CHUNK = 16
ATT_CHUNK = 64
ATT_LEFT = 8
ATT_HEAD_DIM = 128
ATT_MAX_REL = 256
ATT_QGROUP = 4
MEM_HEAD_DIM = 128
CONV_W = 3

VMEM_LIMIT_BYTES = 56 * 1024 * 1024
ROW_TILE = 512
SUBLANES = 8


def _params(*sem):
    return pltpu.CompilerParams(dimension_semantics=sem, vmem_limit_bytes=VMEM_LIMIT_BYTES)


def _rms(x, g):
    ms = jnp.mean(x * x, axis=-1, keepdims=True)
    return x * lax.rsqrt(ms + EPS) * g


def _sigmoid(x):
    return 1.0 / (1.0 + jnp.exp(-x))


def _dot(a, b):
    return jnp.dot(a, b, preferred_element_type=F32)


def _norm_matmul_kernel(x_ref, g_ref, w_ref, o_ref, h_ref):
    @pl.when(pl.program_id(1) == 0)
    def _():
        h_ref[...] = _rms(x_ref[...], g_ref[...]).astype(h_ref.dtype)

    o_ref[...] = _dot(h_ref[...], w_ref[...]).astype(o_ref.dtype)


def norm_matmul(x, g, w, *, tm=ROW_TILE, tn=512, out_dtype=BF16):
    m, k = x.shape
    n = w.shape[1]
    tn = min(tn, n)
    return pl.pallas_call(
        _norm_matmul_kernel,
        grid=(m // tm, n // tn),
        in_specs=[
            pl.BlockSpec((tm, k), lambda i, j: (i, 0)),
            pl.BlockSpec((1, k), lambda i, j: (0, 0)),
            pl.BlockSpec((k, tn), lambda i, j: (0, j)),
        ],
        out_specs=pl.BlockSpec((tm, tn), lambda i, j: (i, j)),
        out_shape=jax.ShapeDtypeStruct((m, n), out_dtype),
        scratch_shapes=[pltpu.VMEM((tm, k), BF16)],
        compiler_params=_params("parallel", "arbitrary"),
        name="norm_matmul",
    )(x, g.reshape(1, k), w)


def _proj_res_kernel(a_ref, w_ref, x_ref, g_ref, o_ref):
    f = _dot(a_ref[...], w_ref[...])
    o_ref[...] = x_ref[...] + _rms(f, g_ref[...])


def proj_res(a, w, x, g, *, tm=ROW_TILE):
    m, k = a.shape
    d = w.shape[1]
    return pl.pallas_call(
        _proj_res_kernel,
        grid=(m // tm,),
        in_specs=[
            pl.BlockSpec((tm, k), lambda i: (i, 0)),
            pl.BlockSpec((k, d), lambda i: (0, 0)),
            pl.BlockSpec((tm, d), lambda i: (i, 0)),
            pl.BlockSpec((1, d), lambda i: (0, 0)),
        ],
        out_specs=pl.BlockSpec((tm, d), lambda i: (i, 0)),
        out_shape=jax.ShapeDtypeStruct((m, d), F32),
        compiler_params=_params("parallel"),
        name="proj_res",
    )(a, w, x, g.reshape(1, d))


def _glu_res_kernel(y_ref, wv_ref, wg_ref, x_ref, g_ref, o_ref, m_ref, *, nj, tn):
    j = pl.program_id(1)
    y = y_ref[...]
    m_ref[j] = _dot(y, wv_ref[...]) * _sigmoid(_dot(y, wg_ref[...]))

    @pl.when(j == nj - 1)
    def _():
        ss = None
        for jj in range(nj):
            mj = m_ref[jj]
            s = jnp.sum(mj * mj, axis=-1, keepdims=True)
            ss = s if ss is None else ss + s
        inv = lax.rsqrt(ss / (nj * tn) + EPS)
        for jj in range(nj):
            sl = slice(jj * tn, (jj + 1) * tn)
            o_ref[:, sl] = x_ref[:, sl] + m_ref[jj] * inv * g_ref[:, sl]


def glu_res(y, w_glu, x, g, *, tm=ROW_TILE, tn=512):
    m, k = y.shape
    d = w_glu.shape[1] // 2
    tn = min(tn, d)
    nj = d // tn
    return pl.pallas_call(
        functools.partial(_glu_res_kernel, nj=nj, tn=tn),
        grid=(m // tm, nj),
        in_specs=[
            pl.BlockSpec((tm, k), lambda i, j: (i, 0)),
            pl.BlockSpec((k, tn), lambda i, j: (0, j)),
            pl.BlockSpec((k, tn), lambda i, j: (0, j + nj)),
            pl.BlockSpec((tm, d), lambda i, j: (i, 0)),
            pl.BlockSpec((1, d), lambda i, j: (0, 0)),
        ],
        out_specs=pl.BlockSpec((tm, d), lambda i, j: (i, 0)),
        out_shape=jax.ShapeDtypeStruct((m, d), F32),
        scratch_shapes=[pltpu.VMEM((nj, tm, tn), F32)],
        compiler_params=_params("parallel", "arbitrary"),
        name="glu_res",
    )(y, w_glu, w_glu, x, g.reshape(1, d))


def _s5_operators(lam_re, lam_im, log_dt, b_re, b_im, c_re, c_im, d_skip):
    hp = lax.Precision.HIGHEST
    n_g, n_p = lam_re.shape
    n_h = S5_GROUP
    n_l = S5_CHUNK
    lr = lam_re.astype(F32)
    li = lam_im.astype(F32)
    dt = jnp.exp(log_dt.astype(F32))[:, None]
    mag = jnp.exp(lr * dt)
    ang = li * dt
    ab_re = mag * jnp.cos(ang)
    ab_im = mag * jnp.sin(ang)
    den = lr * lr + li * li
    nr = ab_re - 1.0
    f_re = (nr * lr + ab_im * li) / den
    f_im = (ab_im * lr - nr * li) / den
    br = b_re.astype(F32)
    bi = b_im.astype(F32)
    bb_re = f_re[..., None] * br - f_im[..., None] * bi
    bb_im = f_re[..., None] * bi + f_im[..., None] * br
    tau = jnp.arange(n_l + 1, dtype=F32)[:, None, None]
    pw_mag = jnp.exp(lr * dt * tau)
    pw_re = pw_mag * jnp.cos(ang * tau)
    pw_im = pw_mag * jnp.sin(ang * tau)
    cr = c_re.astype(F32)
    ci = c_im.astype(F32)
    ca_re = cr[None] * pw_re[:, :, None, :] - ci[None] * pw_im[:, :, None, :]
    ca_im = cr[None] * pw_im[:, :, None, :] + ci[None] * pw_re[:, :, None, :]
    kern = (jnp.einsum("tgop,gpi->tgoi", ca_re[:n_l], bb_re, precision=hp)
            - jnp.einsum("tgop,gpi->tgoi", ca_im[:n_l], bb_im, precision=hp))
    lag = np.arange(n_l)[None, :] - np.arange(n_l)[:, None]
    t_full = kern[np.clip(lag, 0, n_l - 1)]
    t_full = jnp.where(jnp.asarray(lag >= 0)[:, :, None, None, None], t_full, 0.0)
    t_mat = jnp.transpose(t_full, (2, 0, 4, 1, 3))
    skip = d_skip.astype(F32).reshape(n_g, n_h)
    eye = (jnp.eye(n_l, dtype=F32)[:, None, :, None] * jnp.eye(n_h, dtype=F32)[None, :, None, :])
    t_mat = t_mat + eye[None] * skip[:, None, :, None, None]
    t_mat = t_mat.reshape(n_g, n_l * n_h, n_l * n_h)
    rev = pw_re[n_l - 1 - np.arange(n_l)], pw_im[n_l - 1 - np.arange(n_l)]
    w_re = rev[0][..., None] * bb_re[None] - rev[1][..., None] * bb_im[None]
    w_im = rev[0][..., None] * bb_im[None] + rev[1][..., None] * bb_re[None]
    w_re = jnp.transpose(w_re, (1, 0, 3, 2)).reshape(n_g, n_l * n_h, n_p)
    w_im = jnp.transpose(w_im, (1, 0, 3, 2)).reshape(n_g, n_l * n_h, n_p)
    def pair_cols(w):
        w = w.reshape(n_g // 2, 2, n_l * n_h, n_p)
        z = jnp.zeros_like(w[:, 0])
        return jnp.stack([jnp.concatenate([w[:, 0], z], -1), jnp.concatenate([z, w[:, 1]], -1)], 1)
    w_re2 = pair_cols(w_re)
    w_im2 = pair_cols(w_im)
    v_re = jnp.transpose(ca_re[1:], (1, 3, 0, 2)).reshape(n_g, n_p, n_l * n_h)
    v_im = -jnp.transpose(ca_im[1:], (1, 3, 0, 2)).reshape(n_g, n_p, n_l * n_h)
    def pair_rows(v):
        v = v.reshape(n_g // 2, 2, n_p, n_l * n_h)
        z = jnp.zeros_like(v[:, 0])
        top = jnp.concatenate([v[:, 0], z], -1)
        bot = jnp.concatenate([z, v[:, 1]], -1)
        return jnp.concatenate([top, bot], 1)
    v_re2 = pair_rows(v_re)
    v_im2 = pair_rows(v_im)
    a_re = pw_re[n_l].reshape(1, n_g * n_p)
    a_im = pw_im[n_l].reshape(1, n_g * n_p)
    return (t_mat.astype(BF16), w_re2.astype(BF16), w_im2.astype(BF16),
            v_re2.astype(BF16), v_im2.astype(BF16), a_re, a_im)


def _s5_inject_kernel(u_ref, wre_ref, wim_ref, ore_ref, oim_ref):
    u0 = u_ref[0]
    u1 = u_ref[1]
    ore_ref[...] = _dot(u0, wre_ref[0, 0]) + _dot(u1, wre_ref[0, 1])
    oim_ref[...] = _dot(u0, wim_ref[0, 0]) + _dot(u1, wim_ref[0, 1])


def s5_inject(ug, w_re2, w_im2):
    n_g, r, lh = ug.shape
    p2 = w_re2.shape[-1]
    out = jax.ShapeDtypeStruct((r, (n_g // 2) * p2), F32)
    return pl.pallas_call(
        _s5_inject_kernel,
        grid=(n_g // 2,),
        in_specs=[
            pl.BlockSpec((2, r, lh), lambda j: (j, 0, 0)),
            pl.BlockSpec((1, 2, lh, p2), lambda j: (j, 0, 0, 0)),
            pl.BlockSpec((1, 2, lh, p2), lambda j: (j, 0, 0, 0)),
        ],
        out_specs=[pl.BlockSpec((r, p2), lambda j: (0, j)), pl.BlockSpec((r, p2), lambda j: (0, j))],
        out_shape=[out, out],
        compiler_params=_params("parallel"),
        name="s5_inject",
    )(ug, w_re2, w_im2)


def _s5_scan_kernel(wre_ref, wim_ref, are_ref, aim_ref, xre_ref, xim_ref, *, n_c):
    n_b, tl = wre_ref.shape[1:]
    a_re = jnp.broadcast_to(are_ref[...], (n_b, tl))
    a_im = jnp.broadcast_to(aim_ref[...], (n_b, tl))

    def body(c, carry):
        s_re, s_im = carry
        xre_ref[c] = s_re
        xim_ref[c] = s_im
        n_re = a_re * s_re - a_im * s_im + wre_ref[c]
        n_im = a_re * s_im + a_im * s_re + wim_ref[c]
        return n_re, n_im

    zero = jnp.zeros((n_b, tl), F32)
    lax.fori_loop(0, n_c, body, (zero, zero))


def s5_scan(w_re, w_im, a_re, a_im, *, tl=512):
    n_c, n_b, gp = w_re.shape
    out = jax.ShapeDtypeStruct((n_c, n_b, gp), F32)
    blk = pl.BlockSpec((n_c, n_b, tl), lambda j: (0, 0, j))
    coef = pl.BlockSpec((1, tl), lambda j: (0, j))
    return pl.pallas_call(
        functools.partial(_s5_scan_kernel, n_c=n_c),
        grid=(gp // tl,),
        in_specs=[blk, blk, coef, coef],
        out_specs=[blk, blk],
        out_shape=[out, out],
        compiler_params=_params("parallel"),
        name="s5_scan",
    )(w_re, w_im, a_re, a_im)


def _gelu_tanh(y):
    c = np.float32(np.sqrt(2.0 / np.pi))
    return 0.5 * y * (1.0 + jnp.tanh(c * (y + 0.044715 * (y * y * y))))


def _s5_out_kernel(u_ref, t_ref, xre_ref, xim_ref, vre_ref, vim_ref, o_ref):
    lh = u_ref.shape[-1]
    carry = (_dot(xre_ref[...].astype(BF16), vre_ref[0])
             + _dot(xim_ref[...].astype(BF16), vim_ref[0]))
    for k in range(2):
        y = _dot(u_ref[k], t_ref[k]) + carry[:, k * lh:(k + 1) * lh]
        o_ref[k] = _gelu_tanh(y).astype(o_ref.dtype)


def s5_out(ug, t_mat, x_re, x_im, v_re2, v_im2):
    n_g, r, lh = ug.shape
    p2 = v_re2.shape[1]
    return pl.pallas_call(
        _s5_out_kernel,
        grid=(n_g // 2,),
        in_specs=[
            pl.BlockSpec((2, r, lh), lambda j: (j, 0, 0)),
            pl.BlockSpec((2, lh, lh), lambda j: (j, 0, 0)),
            pl.BlockSpec((r, p2), lambda j: (0, j)),
            pl.BlockSpec((r, p2), lambda j: (0, j)),
            pl.BlockSpec((1, p2, 2 * lh), lambda j: (j, 0, 0)),
            pl.BlockSpec((1, p2, 2 * lh), lambda j: (j, 0, 0)),
        ],
        out_specs=pl.BlockSpec((2, r, lh), lambda j: (j, 0, 0)),
        out_shape=jax.ShapeDtypeStruct((n_g, r, lh), BF16),
        compiler_params=_params("parallel"),
        name="s5_out",
    )(ug, t_mat, x_re, x_im, v_re2, v_im2)


def s5_mixer(u, bsz, seq, ops):
    t_mat, w_re2, w_im2, v_re2, v_im2, a_re, a_im = ops
    d = u.shape[1]
    n_g = d // S5_GROUP
    n_c = seq // S5_CHUNK
    ug = u.reshape(bsz, n_c, S5_CHUNK, n_g, S5_GROUP)
    ug = jnp.transpose(ug, (3, 1, 0, 2, 4)).reshape(n_g, n_c * bsz, S5_CHUNK * S5_GROUP)
    w_re, w_im = s5_inject(ug, w_re2, w_im2)
    gp = w_re.shape[1]
    x_re, x_im = s5_scan(w_re.reshape(n_c, bsz, gp), w_im.reshape(n_c, bsz, gp), a_re, a_im)
    yg = s5_out(ug, t_mat, x_re.reshape(n_c * bsz, gp), x_im.reshape(n_c * bsz, gp), v_re2, v_im2)
    yg = yg.reshape(n_g, n_c, bsz, S5_CHUNK, S5_GROUP)
    return jnp.transpose(yg, (2, 1, 3, 0, 4)).reshape(bsz * seq, d)


def _band_bias(rel_bias):
    c = ATT_CHUNK
    nq = ATT_QGROUP * c
    nk = (ATT_QGROUP + ATT_LEFT) * c
    qi = np.arange(nq)[:, None]
    kj = np.arange(nk)[None, :]
    dist = ATT_LEFT * c + qi - kj
    idx = np.clip(dist, -(c - 1), ATT_MAX_REL) + (c - 1)
    qa = qi // c
    kc = kj // c
    valid = (kc >= qa) & (kc <= qa + ATT_LEFT)
    bias = rel_bias.astype(F32)[:, idx]
    return jnp.where(jnp.asarray(valid)[None], bias, NEG_INF)


def _band_attn_kernel(q_ref, k_ref, v_ref, b_ref, o_ref, *, seq):
    c = ATT_CHUNK
    nq = ATT_QGROUP * c
    nk = (ATT_QGROUP + ATT_LEFT) * c
    for gi in range(seq // nq):
        q0 = gi * nq
        k0 = q0 - ATT_LEFT * c
        off = max(0, -k0)
        k0 = k0 + off
        q = q_ref[q0:q0 + nq, :]
        k = k_ref[k0:q0 + nq, :]
        v = v_ref[k0:q0 + nq, :]
        s = lax.dot_general(q, k, (((1,), (1,)), ((), ())), preferred_element_type=F32)
        s = s + b_ref[0, :, off:nk]
        m = jnp.max(s, axis=-1, keepdims=True)
        p = jnp.exp(s - m)
        l = jnp.sum(p, axis=-1, keepdims=True)
        o = _dot(p.astype(BF16), v) / l
        o_ref[q0:q0 + nq, :] = o.astype(o_ref.dtype)


def band_attention(q, kv, bias, *, bsz, seq):
    m, d = q.shape
    n_h = d // ATT_HEAD_DIM
    dh = ATT_HEAD_DIM
    return pl.pallas_call(
        functools.partial(_band_attn_kernel, seq=seq),
        grid=(bsz, n_h),
        in_specs=[
            pl.BlockSpec((seq, dh), lambda b, h: (b, h)),
            pl.BlockSpec((seq, dh), lambda b, h: (b, h)),
            pl.BlockSpec((seq, dh), lambda b, h: (b, h + n_h)),
            pl.BlockSpec((1,) + bias.shape[1:], lambda b, h: (h, 0, 0)),
        ],
        out_specs=pl.BlockSpec((seq, dh), lambda b, h: (b, h)),
        out_shape=jax.ShapeDtypeStruct((m, d), BF16),
        compiler_params=_params("parallel", "parallel"),
        name="band_attention",
    )(q, kv, kv, bias)


def _mem_attn_kernel(x_ref, g1_ref, wq_ref, k_ref, v_ref, wo_ref, g2_ref, o_ref):
    x = x_ref[...]
    h = _rms(x, g1_ref[...]).astype(BF16)
    q = _dot(h, wq_ref[...]).astype(BF16)
    k = k_ref[...]
    v = v_ref[...]
    dh = MEM_HEAD_DIM
    heads = []
    for hd in range(q.shape[1] // dh):
        sl = slice(hd * dh, (hd + 1) * dh)
        s = lax.dot_general(q[:, sl], k[:, sl], (((1,), (1,)), ((), ())), preferred_element_type=F32)
        m = jnp.max(s, axis=-1, keepdims=True)
        p = jnp.exp(s - m)
        l = jnp.sum(p, axis=-1, keepdims=True)
        heads.append((_dot(p.astype(BF16), v[:, sl]) / l).astype(BF16))
    o = jnp.concatenate(heads, axis=-1)
    c = _dot(o, wo_ref[...])
    o_ref[...] = x + _rms(c, g2_ref[...])


def mem_attention(x, g1, wq, kvmem, layer, wo, g2, *, seq, n_mem, tm=ROW_TILE):
    m, d = x.shape
    md = wq.shape[1]
    tiles_per_batch = seq // tm
    return pl.pallas_call(
        _mem_attn_kernel,
        grid=(m // tm,),
        in_specs=[
            pl.BlockSpec((tm, d), lambda i: (i, 0)),
            pl.BlockSpec((1, d), lambda i: (0, 0)),
            pl.BlockSpec((d, md), lambda i: (0, 0)),
            pl.BlockSpec((n_mem, md), lambda i: (i // tiles_per_batch, 2 * layer)),
            pl.BlockSpec((n_mem, md), lambda i: (i // tiles_per_batch, 2 * layer + 1)),
            pl.BlockSpec((md, d), lambda i: (0, 0)),
            pl.BlockSpec((1, d), lambda i: (0, 0)),
        ],
        out_specs=pl.BlockSpec((tm, d), lambda i: (i, 0)),
        out_shape=jax.ShapeDtypeStruct((m, d), F32),
        compiler_params=_params("parallel"),
        name="mem_attention",
    )(x, g1.reshape(1, d), wq, kvmem, kvmem, wo, g2.reshape(1, d))


def _conv_ffn_kernel(x_ref, g1_ref, wuv_ref, wug_ref, cwv_ref, cwg_ref, cbv_ref, cbg_ref, wd_ref,
                     g2_ref, o_ref, h_ref, acc_ref, tail_ref, *, nj, tiles_per_batch):
    i = pl.program_id(0)
    j = pl.program_id(1)
    tm = x_ref.shape[0]

    @pl.when(j == 0)
    def _():
        h_ref[...] = _rms(x_ref[...], g1_ref[...]).astype(h_ref.dtype)

    @pl.when(i % tiles_per_batch == 0)
    def _():
        tail_ref[j] = jnp.zeros(tail_ref.shape[1:], F32)

    h = h_ref[...]

    def conv(w_ref, cw_ref, cb_ref, slot):
        up = _dot(h, w_ref[...])
        ext = jnp.concatenate([tail_ref[j, slot], up], axis=0)
        tail_ref[j, slot] = up[tm - SUBLANES:, :]
        back1 = pltpu.roll(ext, 1, axis=0)[SUBLANES:, :]
        back2 = pltpu.roll(ext, 2, axis=0)[SUBLANES:, :]
        cw = cw_ref[...]
        return cb_ref[...] + back2 * cw[0:1, :] + back1 * cw[1:2, :] + up * cw[2:3, :]

    val = conv(wuv_ref, cwv_ref, cbv_ref, 0)
    gate = conv(wug_ref, cwg_ref, cbg_ref, 1)
    act = (val * (gate * _sigmoid(gate))).astype(BF16)
    part = _dot(act, wd_ref[...])

    @pl.when(j == 0)
    def _():
        acc_ref[...] = part

    @pl.when(j > 0)
    def _():
        acc_ref[...] += part

    @pl.when(j == nj - 1)
    def _():
        o_ref[...] = x_ref[...] + _rms(acc_ref[...], g2_ref[...])


def conv_ffn(x, g1, w_up, conv_w, conv_b, w_down, g2, *, seq, tm=ROW_TILE, tf=512):
    m, d = x.shape
    f = w_down.shape[0]
    nj = f // tf
    tiles_per_batch = seq // tm
    conv_b = conv_b.reshape(1, 2 * f)
    return pl.pallas_call(
        functools.partial(_conv_ffn_kernel, nj=nj, tiles_per_batch=tiles_per_batch),
        grid=(m // tm, nj),
        in_specs=[
            pl.BlockSpec((tm, d), lambda i, j: (i, 0)),
            pl.BlockSpec((1, d), lambda i, j: (0, 0)),
            pl.BlockSpec((d, tf), lambda i, j: (0, j)),
            pl.BlockSpec((d, tf), lambda i, j: (0, j + nj)),
            pl.BlockSpec((CONV_W, tf), lambda i, j: (0, j)),
            pl.BlockSpec((CONV_W, tf), lambda i, j: (0, j + nj)),
            pl.BlockSpec((1, tf), lambda i, j: (0, j)),
            pl.BlockSpec((1, tf), lambda i, j: (0, j + nj)),
            pl.BlockSpec((tf, d), lambda i, j: (j, 0)),
            pl.BlockSpec((1, d), lambda i, j: (0, 0)),
        ],
        out_specs=pl.BlockSpec((tm, d), lambda i, j: (i, 0)),
        out_shape=jax.ShapeDtypeStruct((m, d), F32),
        scratch_shapes=[
            pltpu.VMEM((tm, d), BF16),
            pltpu.VMEM((tm, d), F32),
            pltpu.VMEM((nj, 2, SUBLANES, tf), F32),
        ],
        compiler_params=_params("arbitrary", "arbitrary"),
        name="conv_ffn",
    )(x, g1.reshape(1, d), w_up, w_up, conv_w, conv_w, conv_b, conv_b, w_down, g2.reshape(1, d))


def kernel(x, mem, norm_mix, norm_mem, norm_ffn, mem_in_norm, a_w_in, a_lam_re, a_lam_im, a_log_dt,
           a_b_re, a_b_im, a_c_re, a_c_im, a_d, a_w_glu, kv_norm, w_k, w_v, b_w_q, b_rel_bias, b_w_o,
           m_w_q, m_w_kv, m_w_o, f_w_up, f_conv_w, f_conv_b, f_w_down):
    bsz, seq, d = x.shape
    depth = norm_mix.shape[0]
    n_a = a_w_in.shape[0]
    n_mem = mem.shape[1]
    md = m_w_q.shape[2]
    assert seq % ROW_TILE == 0 and seq % (ATT_QGROUP * ATT_CHUNK) == 0 and seq % S5_CHUNK == 0

    xf = x.reshape(bsz * seq, d)

    w_kv_all = jnp.transpose(m_w_kv, (1, 0, 2)).reshape(d, depth * 2 * md).astype(BF16)
    kvmem = norm_matmul(mem.reshape(bsz * n_mem, d), mem_in_norm, w_kv_all)

    kv = None
    for l in range(depth):
        if l == n_a:
            w_kv = jnp.concatenate([w_k, w_v], axis=1).astype(BF16)
            kv = norm_matmul(xf, kv_norm, w_kv)
        if l < n_a:
            u = norm_matmul(xf, norm_mix[l, 0], a_w_in[l].astype(BF16))
            ops = _s5_operators(a_lam_re[l], a_lam_im[l], a_log_dt[l], a_b_re[l], a_b_im[l],
                                a_c_re[l], a_c_im[l], a_d[l])
            y = s5_mixer(u, bsz, seq, ops)
            xf = glu_res(y, a_w_glu[l].astype(BF16), xf, norm_mix[l, 1])
        else:
            jb = l - n_a
            w_q = (b_w_q[jb] * (ATT_HEAD_DIM ** -0.5)).astype(BF16)
            q = norm_matmul(xf, norm_mix[l, 0], w_q)
            o = band_attention(q, kv, _band_bias(b_rel_bias[jb]), bsz=bsz, seq=seq)
            xf = proj_res(o, b_w_o[jb].astype(BF16), xf, norm_mix[l, 1])
        xf = mem_attention(xf, norm_mem[l, 0], (m_w_q[l] * (MEM_HEAD_DIM ** -0.5)).astype(BF16),
                           kvmem, l, m_w_o[l].astype(BF16), norm_mem[l, 1], seq=seq, n_mem=n_mem)
        xf = conv_ffn(xf, norm_ffn[l, 0], f_w_up[l].astype(BF16), f_conv_w[l], f_conv_b[l],
                      f_w_down[l].astype(BF16), norm_ffn[l, 1], seq=seq)
    return xf.reshape(bsz, seq, d)
```

```python
import functools

import numpy as np
import jax
import jax.numpy as jnp
from jax import lax
from jax.experimental import pallas as pl
from jax.experimental.pallas import tpu as pltpu

EPS = 1e-6
NEG_INF = -1e30
BF16 = jnp.bfloat16
F32 = jnp.float32

S5_GROUP = 16
S5_CHUNK = 16
S5_POS_PER_STEP = 4
ATT_CHUNK = 64
ATT_LEFT = 8
ATT_HEAD_DIM = 128
ATT_MAX_REL = 256
ATT_QGROUP = 4
MEM_HEAD_DIM = 128
CONV_W = 3

VMEM_LIMIT_BYTES = 56 * 1024 * 1024
ROW_TILE = 512
COL_TILE = 2048
FFN_ROW_SPLIT = 2
SUBLANES = 8
LANES = 128

_NT = (((1,), (1,)), ((), ()))


def _params(*sem):
    return pltpu.CompilerParams(dimension_semantics=sem, vmem_limit_bytes=VMEM_LIMIT_BYTES)


def _rms(x, g):
    ms = jnp.mean(x * x, axis=-1, keepdims=True)
    return x * lax.rsqrt(ms + EPS) * g


def _sigmoid(x):
    return 1.0 / (1.0 + jnp.exp(-x))


def _dot(a, b):
    return jnp.dot(a, b, preferred_element_type=F32)


def _dot_nt(a, b):
    return lax.dot_general(a, b, _NT, preferred_element_type=F32)


def _norm_matmul_kernel(x_ref, g_ref, w_ref, o_ref, h_ref):
    @pl.when(pl.program_id(1) == 0)
    def _():
        h_ref[...] = _rms(x_ref[...], g_ref[...]).astype(h_ref.dtype)

    o_ref[...] = _dot(h_ref[...], w_ref[...]).astype(o_ref.dtype)


def norm_matmul(x, g, w, *, tm=ROW_TILE, tn=COL_TILE, out_dtype=BF16):
    m, k = x.shape
    n = w.shape[1]
    tn = min(tn, n)
    return pl.pallas_call(
        _norm_matmul_kernel,
        grid=(m // tm, n // tn),
        in_specs=[
            pl.BlockSpec((tm, k), lambda i, j: (i, 0)),
            pl.BlockSpec((1, k), lambda i, j: (0, 0)),
            pl.BlockSpec((k, tn), lambda i, j: (0, j)),
        ],
        out_specs=pl.BlockSpec((tm, tn), lambda i, j: (i, j)),
        out_shape=jax.ShapeDtypeStruct((m, n), out_dtype),
        scratch_shapes=[pltpu.VMEM((tm, k), BF16)],
        compiler_params=_params("parallel", "arbitrary"),
        name="norm_matmul",
    )(x, g.reshape(1, k), w)


def _proj_res_kernel(a_ref, w_ref, x_ref, g_ref, o_ref):
    f = _dot(a_ref[...], w_ref[...])
    o_ref[...] = x_ref[...] + _rms(f, g_ref[...])


def proj_res(a, w, x, g, *, tm=ROW_TILE):
    m, k = a.shape
    d = w.shape[1]
    return pl.pallas_call(
        _proj_res_kernel,
        grid=(m // tm,),
        in_specs=[
            pl.BlockSpec((tm, k), lambda i: (i, 0)),
            pl.BlockSpec((k, d), lambda i: (0, 0)),
            pl.BlockSpec((tm, d), lambda i: (i, 0)),
            pl.BlockSpec((1, d), lambda i: (0, 0)),
        ],
        out_specs=pl.BlockSpec((tm, d), lambda i: (i, 0)),
        out_shape=jax.ShapeDtypeStruct((m, d), F32),
        compiler_params=_params("parallel"),
        name="proj_res",
    )(a, w, x, g.reshape(1, d))


def _s5_operators(lam_re, lam_im, log_dt, b_re, b_im, c_re, c_im, d_skip):
    hp = lax.Precision.HIGHEST
    n_g, n_p = lam_re.shape
    n_h = S5_GROUP
    n_l = S5_CHUNK
    lh = n_l * n_h
    lr = lam_re.astype(F32)
    li = lam_im.astype(F32)
    dt = jnp.exp(log_dt.astype(F32))[:, None]
    mag = jnp.exp(lr * dt)
    ang = li * dt
    ab_re = mag * jnp.cos(ang)
    ab_im = mag * jnp.sin(ang)
    den = lr * lr + li * li
    nr = ab_re - 1.0
    f_re = (nr * lr + ab_im * li) / den
    f_im = (ab_im * lr - nr * li) / den
    br = b_re.astype(F32)
    bi = b_im.astype(F32)
    bb_re = f_re[..., None] * br - f_im[..., None] * bi
    bb_im = f_re[..., None] * bi + f_im[..., None] * br
    tau = jnp.arange(n_l + 1, dtype=F32)
    pw_re = jnp.exp((lr * dt)[..., None] * tau) * jnp.cos(ang[..., None] * tau)
    pw_im = jnp.exp((lr * dt)[..., None] * tau) * jnp.sin(ang[..., None] * tau)
    tau_m = tau[None, :, None]
    pt_re = jnp.exp((lr * dt)[:, None, :] * tau_m) * jnp.cos(ang[:, None, :] * tau_m)
    pt_im = jnp.exp((lr * dt)[:, None, :] * tau_m) * jnp.sin(ang[:, None, :] * tau_m)
    cr = c_re.astype(F32)[:, None]
    ci = c_im.astype(F32)[:, None]
    ca_re = cr * pt_re[:, :, None, :] - ci * pt_im[:, :, None, :]
    ca_im = cr * pt_im[:, :, None, :] + ci * pt_re[:, :, None, :]
    kern = (jnp.einsum("gtop,gpi->goti", ca_re[:, :n_l], bb_re, precision=hp)
            - jnp.einsum("gtop,gpi->goti", ca_im[:, :n_l], bb_im, precision=hp))
    krev = kern[:, :, ::-1, :].reshape(n_g, n_h, lh)
    rows = []
    for t in range(n_l):
        sh = (n_l - 1 - t) * n_h
        rows.append(jnp.pad(krev[:, :, sh:], ((0, 0), (0, 0), (0, sh))))
    t_mat = jnp.stack(rows, axis=1).reshape(n_g, lh, lh)
    skip = jnp.tile(d_skip.astype(F32).reshape(n_g, n_h), (1, n_l))
    t_mat = t_mat + jnp.eye(lh, dtype=F32)[None] * skip[:, :, None]
    rev_re = pw_re[:, :, n_l - 1::-1]
    rev_im = pw_im[:, :, n_l - 1::-1]
    w_re = (rev_re[..., None] * bb_re[:, :, None, :] - rev_im[..., None] * bb_im[:, :, None, :])
    w_im = (rev_re[..., None] * bb_im[:, :, None, :] + rev_im[..., None] * bb_re[:, :, None, :])
    def slot_rows(w):
        w = w.reshape(n_g // 2, 2, n_p, lh)
        z = jnp.zeros_like(w[:, 0])
        return jnp.stack([jnp.concatenate([w[:, 0], z], 1), jnp.concatenate([z, w[:, 1]], 1)], 1)
    w_pack = jnp.stack([slot_rows(w_re), slot_rows(w_im)], axis=2)
    def slot_cols(v):
        v = v.reshape(n_g // 2, 2, lh, n_p)
        z = jnp.zeros_like(v[:, 0])
        return jnp.stack([jnp.concatenate([v[:, 0], z], -1), jnp.concatenate([z, v[:, 1]], -1)], 1)
    v_pack = jnp.stack([slot_cols(ca_re[:, 1:]), slot_cols(-ca_im[:, 1:])], axis=2)
    a_re = pw_re[:, :, n_l].reshape(1, n_g * n_p)
    a_im = pw_im[:, :, n_l].reshape(1, n_g * n_p)
    return t_mat.astype(BF16), w_pack.astype(BF16), v_pack.astype(BF16), a_re, a_im


def _chunk_view(x, bsz, seq):
    d = x.shape[1]
    return x.reshape(bsz, seq // S5_CHUNK, S5_CHUNK * d)


def _norm_matmul_t_kernel(x_ref, g_ref, wt_ref, o_ref, h_ref):
    nc = x_ref.shape[0]
    d = g_ref.shape[1]
    for k in range(S5_POS_PER_STEP):
        h_ref[k * nc:(k + 1) * nc, :] = _rms(x_ref[:, k * d:(k + 1) * d], g_ref[...]).astype(BF16)
    o_ref[...] = _dot_nt(wt_ref[...], h_ref[...]).astype(o_ref.dtype)


def norm_matmul_t(x, g, wt, *, bsz, seq):
    d = x.shape[1]
    n = wt.shape[0]
    nc = seq // S5_CHUNK
    sq = S5_POS_PER_STEP
    steps = S5_CHUNK // sq
    return pl.pallas_call(
        _norm_matmul_t_kernel,
        grid=(bsz, steps),
        in_specs=[
            pl.BlockSpec((None, nc, sq * d), lambda b, s: (b, 0, s)),
            pl.BlockSpec((1, d), lambda b, s: (0, 0)),
            pl.BlockSpec((n, d), lambda b, s: (0, 0)),
        ],
        out_specs=pl.BlockSpec((n, sq * nc), lambda b, s: (0, b * steps + s)),
        out_shape=jax.ShapeDtypeStruct((n, bsz * seq), BF16),
        scratch_shapes=[pltpu.VMEM((sq * nc, d), BF16)],
        compiler_params=_params("parallel", "parallel"),
        name="norm_matmul_t",
    )(_chunk_view(x, bsz, seq), g.reshape(1, d), wt)


def _chunk_operand(u_ref, k, nc):
    h = S5_GROUP
    return jnp.concatenate(
        [u_ref[k * h:(k + 1) * h, s * nc:(s + 1) * nc] for s in range(S5_CHUNK)], axis=0)


def _s5_inject_kernel(u_ref, w_ref, ore_ref, oim_ref):
    nc = ore_ref.shape[0]
    u0 = _chunk_operand(u_ref, 0, nc)
    u1 = _chunk_operand(u_ref, 1, nc)
    ore_ref[...] = (_dot(w_ref[0, 0, 0], u0) + _dot(w_ref[0, 1, 0], u1)).T
    oim_ref[...] = (_dot(w_ref[0, 0, 1], u0) + _dot(w_ref[0, 1, 1], u1)).T


def s5_inject(ut, w_pack, *, bsz, seq):
    n_pair = w_pack.shape[0]
    p2 = w_pack.shape[3]
    nc = seq // S5_CHUNK
    out = jax.ShapeDtypeStruct((nc, bsz * n_pair * p2), F32)
    oblk = pl.BlockSpec((nc, p2), lambda j, b: (0, b * n_pair + j))
    return pl.pallas_call(
        _s5_inject_kernel,
        grid=(n_pair, bsz),
        in_specs=[
            pl.BlockSpec((2 * S5_GROUP, seq), lambda j, b: (j, b)),
            pl.BlockSpec((1,) + w_pack.shape[1:], lambda j, b: (j, 0, 0, 0, 0)),
        ],
        out_specs=[oblk, oblk],
        out_shape=[out, out],
        compiler_params=_params("parallel", "parallel"),
        name="s5_inject",
    )(ut, w_pack)


def _s5_scan_kernel(wre_ref, wim_ref, are_ref, aim_ref, xre_ref, xim_ref, *, n_c):
    n_b, tl = wre_ref.shape[1:]
    a_re = jnp.broadcast_to(are_ref[...], (n_b, tl))
    a_im = jnp.broadcast_to(aim_ref[...], (n_b, tl))

    def body(c, carry):
        s_re, s_im = carry
        xre_ref[c] = s_re
        xim_ref[c] = s_im
        n_re = a_re * s_re - a_im * s_im + wre_ref[c]
        n_im = a_re * s_im + a_im * s_re + wim_ref[c]
        return n_re, n_im

    zero = jnp.zeros((n_b, tl), F32)
    lax.fori_loop(0, n_c, body, (zero, zero))


def s5_scan(w_re, w_im, a_re, a_im, *, tl=512):
    n_c, n_b, gp = w_re.shape
    out = jax.ShapeDtypeStruct((n_c, n_b, gp), F32)
    blk = pl.BlockSpec((n_c, n_b, tl), lambda j: (0, 0, j))
    coef = pl.BlockSpec((1, tl), lambda j: (0, j))
    return pl.pallas_call(
        functools.partial(_s5_scan_kernel, n_c=n_c),
        grid=(gp // tl,),
        in_specs=[blk, blk, coef, coef],
        out_specs=[blk, blk],
        out_shape=[out, out],
        compiler_params=_params("parallel"),
        name="s5_scan",
    )(w_re, w_im, a_re, a_im)


def _gelu_tanh(y):
    c = np.float32(np.sqrt(2.0 / np.pi))
    return 0.5 * y * (1.0 + jnp.tanh(c * (y + 0.044715 * (y * y * y))))


def _s5_out_kernel(u_ref, t_ref, xre_ref, xim_ref, v_ref, o_ref):
    nc = xre_ref.shape[0]
    h = S5_GROUP
    x_re = xre_ref[...].astype(BF16)
    x_im = xim_ref[...].astype(BF16)
    for k in range(2):
        y = (_dot(t_ref[k], _chunk_operand(u_ref, k, nc))
             + _dot_nt(v_ref[0, k, 0], x_re) + _dot_nt(v_ref[0, k, 1], x_im))
        y = _gelu_tanh(y).astype(o_ref.dtype)
        for t in range(S5_CHUNK):
            o_ref[k * h:(k + 1) * h, t * nc:(t + 1) * nc] = y[t * h:(t + 1) * h, :]


def s5_out(ut, t_mat, x_re, x_im, v_pack, *, bsz, seq):
    n_pair = v_pack.shape[0]
    p2 = v_pack.shape[-1]
    nc = seq // S5_CHUNK
    lh = t_mat.shape[-1]
    xblk = pl.BlockSpec((nc, p2), lambda j, b: (0, b * n_pair + j))
    ublk = pl.BlockSpec((2 * S5_GROUP, seq), lambda j, b: (j, b))
    return pl.pallas_call(
        _s5_out_kernel,
        grid=(n_pair, bsz),
        in_specs=[
            ublk,
            pl.BlockSpec((2, lh, lh), lambda j, b: (j, 0, 0)),
            xblk,
            xblk,
            pl.BlockSpec((1,) + v_pack.shape[1:], lambda j, b: (j, 0, 0, 0, 0)),
        ],
        out_specs=ublk,
        out_shape=jax.ShapeDtypeStruct(ut.shape, BF16),
        compiler_params=_params("parallel", "parallel"),
        name="s5_out",
    )(ut, t_mat, x_re, x_im, v_pack)


def s5_mixer(ut, bsz, seq, ops):
    t_mat, w_pack, v_pack, a_re, a_im = ops
    nc = seq // S5_CHUNK
    w_re, w_im = s5_inject(ut, w_pack, bsz=bsz, seq=seq)
    gp = w_re.shape[1] // bsz
    x_re, x_im = s5_scan(w_re.reshape(nc, bsz, gp), w_im.reshape(nc, bsz, gp), a_re, a_im)
    return s5_out(ut, t_mat, x_re.reshape(nc, bsz * gp), x_im.reshape(nc, bsz * gp), v_pack,
                  bsz=bsz, seq=seq)


def _glu_res_kernel(yt_ref, wv_ref, wg_ref, x_ref, g_ref, o_ref, y_ref, m_ref, *, nj, tn):
    j = pl.program_id(2)

    @pl.when(j == 0)
    def _():
        y_ref[...] = yt_ref[...].astype(F32).T.astype(BF16)

    y = y_ref[...]
    m_ref[j] = _dot(y, wv_ref[...]) * _sigmoid(_dot(y, wg_ref[...]))

    @pl.when(j == nj - 1)
    def _():
        nc = x_ref.shape[0]
        d = g_ref.shape[1]
        ss = None
        for jj in range(nj):
            mj = m_ref[jj]
            s = jnp.sum(mj * mj, axis=-1, keepdims=True)
            ss = s if ss is None else ss + s
        inv = lax.rsqrt(ss / d + EPS)
        for k in range(S5_POS_PER_STEP):
            rows = slice(k * nc, (k + 1) * nc)
            for jj in range(nj):
                cols = slice(k * d + jj * tn, k * d + (jj + 1) * tn)
                o_ref[:, cols] = (x_ref[:, cols]
                                  + m_ref[jj, rows, :] * inv[rows] * g_ref[:, jj * tn:(jj + 1) * tn])


def glu_res(yt, w_glu, x, g, *, bsz, seq, tn=512):
    k, _ = yt.shape
    d = w_glu.shape[1] // 2
    tn = min(tn, d)
    nj = d // tn
    nc = seq // S5_CHUNK
    sq = S5_POS_PER_STEP
    steps = S5_CHUNK // sq
    xblk = pl.BlockSpec((None, nc, sq * d), lambda b, s, j: (b, 0, s))
    out = pl.pallas_call(
        functools.partial(_glu_res_kernel, nj=nj, tn=tn),
        grid=(bsz, steps, nj),
        in_specs=[
            pl.BlockSpec((k, sq * nc), lambda b, s, j: (0, b * steps + s)),
            pl.BlockSpec((k, tn), lambda b, s, j: (0, j)),
            pl.BlockSpec((k, tn), lambda b, s, j: (0, j + nj)),
            xblk,
            pl.BlockSpec((1, d), lambda b, s, j: (0, 0)),
        ],
        out_specs=xblk,
        out_shape=jax.ShapeDtypeStruct((bsz, nc, S5_CHUNK * d), F32),
        scratch_shapes=[pltpu.VMEM((sq * nc, k), BF16), pltpu.VMEM((nj, sq * nc, tn), F32)],
        compiler_params=_params("parallel", "parallel", "arbitrary"),
        name="glu_res",
    )(yt, w_glu, w_glu, _chunk_view(x, bsz, seq), g.reshape(1, d))
    return out.reshape(bsz * seq, d)


def _band_bias(rel_bias):
    c = ATT_CHUNK
    nq = ATT_QGROUP * c
    nk = (ATT_QGROUP + ATT_LEFT) * c
    r = rel_bias.astype(F32)
    n_h = r.shape[0]
    d_lo = ATT_LEFT * c - (nk - 1)
    d_hi = ATT_LEFT * c + nq - 1
    n_lo = -(c - 1) - d_lo
    n_hi = d_hi - ATT_MAX_REL
    e = jnp.concatenate([jnp.broadcast_to(r[:, :1], (n_h, n_lo)), r,
                         jnp.broadcast_to(r[:, -1:], (n_h, n_hi))], axis=1)
    width = nq + nk - 1
    assert e.shape[1] == width
    frev = jnp.concatenate([e[:, ::-1], jnp.zeros((n_h, 1), F32)], axis=1)
    skew = jnp.tile(frev, (1, nq))[:, :nq * width].reshape(n_h, nq, width)
    bias = skew[:, :, nq - 1:nq - 1 + nk]
    qa = np.arange(nq)[:, None] // c
    kc = np.arange(nk)[None, :] // c
    valid = (kc >= qa) & (kc <= qa + ATT_LEFT)
    return jnp.where(jnp.asarray(valid)[None], bias, NEG_INF)


def _band_attn_kernel(q_ref, k_ref, v_ref, b_ref, o_ref, *, seq):
    c = ATT_CHUNK
    nq = ATT_QGROUP * c
    nk = (ATT_QGROUP + ATT_LEFT) * c
    for gi in range(seq // nq):
        q0 = gi * nq
        k0 = q0 - ATT_LEFT * c
        off = max(0, -k0)
        k0 = k0 + off
        q = q_ref[q0:q0 + nq, :]
        k = k_ref[k0:q0 + nq, :]
        v = v_ref[k0:q0 + nq, :]
        s = _dot_nt(q, k) + b_ref[0, :, off:nk]
        m = jnp.max(s, axis=-1, keepdims=True)
        p = jnp.exp(s - m)
        l = jnp.sum(p, axis=-1, keepdims=True)
        o = _dot(p.astype(BF16), v) / l
        o_ref[q0:q0 + nq, :] = o.astype(o_ref.dtype)


def band_attention(q, kv, bias, *, bsz, seq):
    m, d = q.shape
    n_h = d // ATT_HEAD_DIM
    dh = ATT_HEAD_DIM
    return pl.pallas_call(
        functools.partial(_band_attn_kernel, seq=seq),
        grid=(bsz, n_h),
        in_specs=[
            pl.BlockSpec((seq, dh), lambda b, h: (b, h)),
            pl.BlockSpec((seq, dh), lambda b, h: (b, h)),
            pl.BlockSpec((seq, dh), lambda b, h: (b, h + n_h)),
            pl.BlockSpec((1,) + bias.shape[1:], lambda b, h: (h, 0, 0)),
        ],
        out_specs=pl.BlockSpec((seq, dh), lambda b, h: (b, h)),
        out_shape=jax.ShapeDtypeStruct((m, d), BF16),
        compiler_params=_params("parallel", "parallel"),
        name="band_attention",
    )(q, kv, kv, bias)


def _mem_attn_kernel(x_ref, g1_ref, wq_ref, k_ref, v_ref, wo_ref, g2_ref, o_ref):
    x = x_ref[...]
    h = _rms(x, g1_ref[...]).astype(BF16)
    q = _dot(h, wq_ref[...]).astype(BF16)
    k = k_ref[...]
    v = v_ref[...]
    dh = MEM_HEAD_DIM
    heads = []
    for hd in range(q.shape[1] // dh):
        sl = slice(hd * dh, (hd + 1) * dh)
        s = _dot_nt(q[:, sl], k[:, sl])
        m = jnp.max(s, axis=-1, keepdims=True)
        p = jnp.exp(s - m)
        l = jnp.sum(p, axis=-1, keepdims=True)
        heads.append((_dot(p.astype(BF16), v[:, sl]) / l).astype(BF16))
    o = jnp.concatenate(heads, axis=-1)
    c = _dot(o, wo_ref[...])
    o_ref[...] = x + _rms(c, g2_ref[...])


def mem_attention(x, g1, wq, kvmem, layer, wo, g2, *, seq, n_mem, tm=ROW_TILE):
    m, d = x.shape
    md = wq.shape[1]
    tiles_per_batch = seq // tm
    return pl.pallas_call(
        _mem_attn_kernel,
        grid=(m // tm,),
        in_specs=[
            pl.BlockSpec((tm, d), lambda i: (i, 0)),
            pl.BlockSpec((1, d), lambda i: (0, 0)),
            pl.BlockSpec((d, md), lambda i: (0, 0)),
            pl.BlockSpec((n_mem, md), lambda i: (i // tiles_per_batch, 2 * layer)),
            pl.BlockSpec((n_mem, md), lambda i: (i // tiles_per_batch, 2 * layer + 1)),
            pl.BlockSpec((md, d), lambda i: (0, 0)),
            pl.BlockSpec((1, d), lambda i: (0, 0)),
        ],
        out_specs=pl.BlockSpec((tm, d), lambda i: (i, 0)),
        out_shape=jax.ShapeDtypeStruct((m, d), F32),
        compiler_params=_params("parallel"),
        name="mem_attention",
    )(x, g1.reshape(1, d), wq, kvmem, kvmem, wo, g2.reshape(1, d))


def _conv_ffn_kernel(x_ref, g1_ref, wuv_ref, wug_ref, cwv_ref, cwg_ref, cbv_ref, cbg_ref, wd_ref,
                     g2_ref, o_ref, h_ref, acc_ref, tail_ref, *, nj, tiles_per_batch):
    i = pl.program_id(0)
    j = pl.program_id(1)
    tm = x_ref.shape[0]
    hm = tm // FFN_ROW_SPLIT

    @pl.when(j == 0)
    def _():
        h_ref[...] = _rms(x_ref[...], g1_ref[...]).astype(h_ref.dtype)
        acc_ref[...] = jnp.zeros(acc_ref.shape, F32)

    @pl.when(i % tiles_per_batch == 0)
    def _():
        tail_ref[j] = jnp.zeros(tail_ref.shape[1:], F32)

    prev = [tail_ref[j, 0], tail_ref[j, 1]]
    cw = [cwv_ref[...], cwg_ref[...]]
    cb = [cbv_ref[...], cbg_ref[...]]
    w_up = [wuv_ref, wug_ref]

    for r in range(FFN_ROW_SPLIT):
        rows = slice(r * hm, (r + 1) * hm)
        h = h_ref[rows, :]
        branch = []
        for slot in range(2):
            up = _dot(h, w_up[slot][...])
            ext = jnp.concatenate([prev[slot], up], axis=0)
            prev[slot] = up[hm - SUBLANES:, :]
            back1 = pltpu.roll(ext, 1, axis=0)[SUBLANES:, :]
            back2 = pltpu.roll(ext, 2, axis=0)[SUBLANES:, :]
            branch.append(cb[slot] + back2 * cw[slot][0:1, :] + back1 * cw[slot][1:2, :]
                          + up * cw[slot][2:3, :])
        val, gate = branch
        act = (val * (gate * _sigmoid(gate))).astype(BF16)
        acc_ref[rows, :] += _dot(act, wd_ref[...])

    tail_ref[j, 0] = prev[0]
    tail_ref[j, 1] = prev[1]

    @pl.when(j == nj - 1)
    def _():
        o_ref[...] = x_ref[...] + _rms(acc_ref[...], g2_ref[...])


def conv_ffn(x, g1, w_up, conv_w, conv_b, w_down, g2, *, seq, tm=ROW_TILE, tf=512):
    m, d = x.shape
    f = w_down.shape[0]
    nj = f // tf
    tiles_per_batch = seq // tm
    conv_b = conv_b.reshape(1, 2 * f)
    return pl.pallas_call(
        functools.partial(_conv_ffn_kernel, nj=nj, tiles_per_batch=tiles_per_batch),
        grid=(m // tm, nj),
        in_specs=[
            pl.BlockSpec((tm, d), lambda i, j: (i, 0)),
            pl.BlockSpec((1, d), lambda i, j: (0, 0)),
            pl.BlockSpec((d, tf), lambda i, j: (0, j)),
            pl.BlockSpec((d, tf), lambda i, j: (0, j + nj)),
            pl.BlockSpec((CONV_W, tf), lambda i, j: (0, j)),
            pl.BlockSpec((CONV_W, tf), lambda i, j: (0, j + nj)),
            pl.BlockSpec((1, tf), lambda i, j: (0, j)),
            pl.BlockSpec((1, tf), lambda i, j: (0, j + nj)),
            pl.BlockSpec((tf, d), lambda i, j: (j, 0)),
            pl.BlockSpec((1, d), lambda i, j: (0, 0)),
        ],
        out_specs=pl.BlockSpec((tm, d), lambda i, j: (i, 0)),
        out_shape=jax.ShapeDtypeStruct((m, d), F32),
        scratch_shapes=[
            pltpu.VMEM((tm, d), BF16),
            pltpu.VMEM((tm, d), F32),
            pltpu.VMEM((nj, 2, SUBLANES, tf), F32),
        ],
        compiler_params=_params("arbitrary", "arbitrary"),
        name="conv_ffn",
    )(x, g1.reshape(1, d), w_up, w_up, conv_w, conv_w, conv_b, conv_b, w_down, g2.reshape(1, d))


def kernel(x, mem, norm_mix, norm_mem, norm_ffn, mem_in_norm, a_w_in, a_lam_re, a_lam_im, a_log_dt,
           a_b_re, a_b_im, a_c_re, a_c_im, a_d, a_w_glu, kv_norm, w_k, w_v, b_w_q, b_rel_bias, b_w_o,
           m_w_q, m_w_kv, m_w_o, f_w_up, f_conv_w, f_conv_b, f_w_down):
    bsz, seq, d = x.shape
    depth = norm_mix.shape[0]
    n_a = a_w_in.shape[0]
    n_mem = mem.shape[1]
    md = m_w_q.shape[2]
    assert seq % ROW_TILE == 0 and seq % (ATT_QGROUP * ATT_CHUNK) == 0
    assert (seq // S5_CHUNK) % LANES == 0 and (d // S5_GROUP) % 2 == 0

    xf = x.reshape(bsz * seq, d)

    w_kv_all = jnp.transpose(m_w_kv, (1, 0, 2)).reshape(d, depth * 2 * md).astype(BF16)
    kvmem = norm_matmul(mem.reshape(bsz * n_mem, d), mem_in_norm, w_kv_all)

    kv = None
    for l in range(depth):
        if l == n_a:
            w_kv = jnp.concatenate([w_k, w_v], axis=1).astype(BF16)
            kv = norm_matmul(xf, kv_norm, w_kv)
        if l < n_a:
            ut = norm_matmul_t(xf, norm_mix[l, 0], a_w_in[l].T.astype(BF16), bsz=bsz, seq=seq)
            ops = _s5_operators(a_lam_re[l], a_lam_im[l], a_log_dt[l], a_b_re[l], a_b_im[l],
                                a_c_re[l], a_c_im[l], a_d[l])
            yt = s5_mixer(ut, bsz, seq, ops)
            xf = glu_res(yt, a_w_glu[l].astype(BF16), xf, norm_mix[l, 1], bsz=bsz, seq=seq)
        else:
            jb = l - n_a
            w_q = (b_w_q[jb] * (ATT_HEAD_DIM ** -0.5)).astype(BF16)
            q = norm_matmul(xf, norm_mix[l, 0], w_q)
            o = band_attention(q, kv, _band_bias(b_rel_bias[jb]), bsz=bsz, seq=seq)
            xf = proj_res(o, b_w_o[jb].astype(BF16), xf, norm_mix[l, 1])
        xf = mem_attention(xf, norm_mem[l, 0], (m_w_q[l] * (MEM_HEAD_DIM ** -0.5)).astype(BF16),
                           kvmem, l, m_w_o[l].astype(BF16), norm_mem[l, 1], seq=seq, n_mem=n_mem)
        xf = conv_ffn(xf, norm_ffn[l, 0], f_w_up[l].astype(BF16), f_conv_w[l], f_conv_b[l],
                      f_w_down[l].astype(BF16), norm_ffn[l, 1], seq=seq)
    return xf.reshape(bsz, seq, d)
```

```python
import functools

import numpy as np
import jax
import jax.numpy as jnp
from jax import lax
from jax.experimental import pallas as pl
from jax.experimental.pallas import tpu as pltpu

EPS = 1e-6
NEG_INF = -1e30
BF16 = jnp.bfloat16
F32 = jnp.float32

S5_GROUP = 16
S5_CHUNK = 16
S5_POS_PER_STEP = 4
ATT_CHUNK = 64
ATT_LEFT = 8
ATT_HEAD_DIM = 128
ATT_MAX_REL = 256
ATT_QGROUP = 4
MEM_HEAD_DIM = 128
CONV_W = 3

VMEM_LIMIT_BYTES = 56 * 1024 * 1024
ROW_TILE = 512
COL_TILE = 2048
FFN_ROW_SPLIT = 2
FFN_COL_TILE = 512
GLU_COL_TILE = 512
SUBLANES = 8
LANES = 128

_NT = (((1,), (1,)), ((), ()))


def _params(*sem):
    return pltpu.CompilerParams(dimension_semantics=sem, vmem_limit_bytes=VMEM_LIMIT_BYTES)


def _rms(x, g):
    ms = jnp.mean(x * x, axis=-1, keepdims=True)
    return x * lax.rsqrt(ms + EPS) * g


def _sigmoid(x):
    return 1.0 / (1.0 + jnp.exp(-x))


def _dot(a, b):
    return jnp.dot(a, b, preferred_element_type=F32)


def _dot_nt(a, b):
    return lax.dot_general(a, b, _NT, preferred_element_type=F32)


def _norm_matmul_kernel(x_ref, g_ref, w_ref, o_ref, h_ref):
    @pl.when(pl.program_id(1) == 0)
    def _():
        h_ref[...] = _rms(x_ref[...], g_ref[...]).astype(h_ref.dtype)

    o_ref[...] = _dot(h_ref[...], w_ref[...]).astype(o_ref.dtype)


def norm_matmul(x, g, w, *, tm=ROW_TILE, tn=COL_TILE, out_dtype=BF16):
    m, k = x.shape
    n = w.shape[1]
    tn = min(tn, n)
    return pl.pallas_call(
        _norm_matmul_kernel,
        grid=(m // tm, n // tn),
        in_specs=[
            pl.BlockSpec((tm, k), lambda i, j: (i, 0)),
            pl.BlockSpec((1, k), lambda i, j: (0, 0)),
            pl.BlockSpec((k, tn), lambda i, j: (0, j)),
        ],
        out_specs=pl.BlockSpec((tm, tn), lambda i, j: (i, j)),
        out_shape=jax.ShapeDtypeStruct((m, n), out_dtype),
        scratch_shapes=[pltpu.VMEM((tm, k), BF16)],
        compiler_params=_params("parallel", "arbitrary"),
        name="norm_matmul",
    )(x, g.reshape(1, k), w)


def _proj_res_kernel(a_ref, w_ref, x_ref, g_ref, o_ref):
    f = _dot(a_ref[...], w_ref[...])
    o_ref[...] = x_ref[...] + _rms(f, g_ref[...])


def proj_res(a, w, x, g, *, tm=ROW_TILE):
    m, k = a.shape
    d = w.shape[1]
    return pl.pallas_call(
        _proj_res_kernel,
        grid=(m // tm,),
        in_specs=[
            pl.BlockSpec((tm, k), lambda i: (i, 0)),
            pl.BlockSpec((k, d), lambda i: (0, 0)),
            pl.BlockSpec((tm, d), lambda i: (i, 0)),
            pl.BlockSpec((1, d), lambda i: (0, 0)),
        ],
        out_specs=pl.BlockSpec((tm, d), lambda i: (i, 0)),
        out_shape=jax.ShapeDtypeStruct((m, d), F32),
        compiler_params=_params("parallel"),
        name="proj_res",
    )(a, w, x, g.reshape(1, d))


def _lag_kernel_body(a_ref, b_ref, o_ref):
    for g in range(a_ref.shape[0]):
        o_ref[g] = jnp.dot(a_ref[g], b_ref[g], precision=lax.Precision.HIGHEST,
                           preferred_element_type=F32)


def _lag_kernels(ca_cat, bb_cat, *, gb=8):
    n_g, lh, p2 = ca_cat.shape
    n_h = bb_cat.shape[-1]
    return pl.pallas_call(
        _lag_kernel_body,
        grid=(n_g // gb,),
        in_specs=[
            pl.BlockSpec((gb, lh, p2), lambda i: (i, 0, 0)),
            pl.BlockSpec((gb, p2, n_h), lambda i: (i, 0, 0)),
        ],
        out_specs=pl.BlockSpec((gb, lh, n_h), lambda i: (i, 0, 0)),
        out_shape=jax.ShapeDtypeStruct((n_g, lh, n_h), F32),
        compiler_params=_params("parallel"),
        name="s5_lag_kernels",
    )(ca_cat, bb_cat)


def _s5_operators(lam_re, lam_im, log_dt, b_re, b_im, c_re, c_im, d_skip, bsz):
    n_g, n_p = lam_re.shape
    n_h = S5_GROUP
    n_l = S5_CHUNK
    lh = n_l * n_h
    lr = lam_re.astype(F32)
    li = lam_im.astype(F32)
    dt = jnp.exp(log_dt.astype(F32))[:, None]
    mag = jnp.exp(lr * dt)
    ang = li * dt
    ab_re = mag * jnp.cos(ang)
    ab_im = mag * jnp.sin(ang)
    den = lr * lr + li * li
    nr = ab_re - 1.0
    f_re = (nr * lr + ab_im * li) / den
    f_im = (ab_im * lr - nr * li) / den
    br = b_re.astype(F32)
    bi = b_im.astype(F32)
    bb_re = f_re[..., None] * br - f_im[..., None] * bi
    bb_im = f_re[..., None] * bi + f_im[..., None] * br
    tau = jnp.arange(n_l + 1, dtype=F32)
    pw_re = jnp.exp((lr * dt)[..., None] * tau) * jnp.cos(ang[..., None] * tau)
    pw_im = jnp.exp((lr * dt)[..., None] * tau) * jnp.sin(ang[..., None] * tau)
    tau_m = tau[None, :, None]
    pt_re = jnp.exp((lr * dt)[:, None, :] * tau_m) * jnp.cos(ang[:, None, :] * tau_m)
    pt_im = jnp.exp((lr * dt)[:, None, :] * tau_m) * jnp.sin(ang[:, None, :] * tau_m)
    cr = c_re.astype(F32)[:, None]
    ci = c_im.astype(F32)[:, None]
    ca_re = cr * pt_re[:, :, None, :] - ci * pt_im[:, :, None, :]
    ca_im = cr * pt_im[:, :, None, :] + ci * pt_re[:, :, None, :]
    cr_o = c_re.astype(F32)[:, :, None, :]
    ci_o = c_im.astype(F32)[:, :, None, :]
    pl_re = pt_re[:, None, :n_l, :]
    pl_im = pt_im[:, None, :n_l, :]
    ca_cat = jnp.concatenate([cr_o * pl_re - ci_o * pl_im, -(cr_o * pl_im + ci_o * pl_re)], axis=-1)
    bb_cat = jnp.concatenate([bb_re, bb_im], axis=1)
    kern = _lag_kernels(ca_cat.reshape(n_g, lh, 2 * n_p), bb_cat).reshape(n_g, n_h, n_l, n_h)
    krev = kern[:, :, ::-1, :].reshape(n_g, n_h, lh)
    rows = []
    for t in range(n_l):
        sh = (n_l - 1 - t) * n_h
        rows.append(jnp.pad(krev[:, :, sh:], ((0, 0), (0, 0), (0, sh))))
    t_mat = jnp.stack(rows, axis=1).reshape(n_g, lh, lh)
    skip = jnp.tile(d_skip.astype(F32).reshape(n_g, n_h), (1, n_l))
    t_mat = t_mat + jnp.eye(lh, dtype=F32)[None] * skip[:, :, None]
    rev_re = pw_re[:, :, n_l - 1::-1]
    rev_im = pw_im[:, :, n_l - 1::-1]
    w_re = (rev_re[..., None] * bb_re[:, :, None, :] - rev_im[..., None] * bb_im[:, :, None, :])
    w_im = (rev_re[..., None] * bb_im[:, :, None, :] + rev_im[..., None] * bb_re[:, :, None, :])
    def slot_rows(w):
        w = w.reshape(n_g // 2, 2, n_p, lh)
        z = jnp.zeros_like(w[:, 0])
        return jnp.stack([jnp.concatenate([w[:, 0], z], 1), jnp.concatenate([z, w[:, 1]], 1)], 1)
    w_pack = jnp.stack([slot_rows(w_re), slot_rows(w_im)], axis=2)
    def slot_cols(v):
        v = v.reshape(n_g // 2, 2, lh, n_p)
        z = jnp.zeros_like(v[:, 0])
        return jnp.stack([jnp.concatenate([v[:, 0], z], -1), jnp.concatenate([z, v[:, 1]], -1)], 1)
    v_pack = jnp.stack([slot_cols(ca_re[:, 1:]), slot_cols(-ca_im[:, 1:])], axis=2)
    def state_cols(a):
        a = jnp.broadcast_to(a.reshape(n_g // 2, 1, 2 * n_p), (n_g // 2, bsz, 2 * n_p))
        return a.reshape(1, n_g * bsz * n_p)
    a_re = state_cols(pw_re[:, :, n_l])
    a_im = state_cols(pw_im[:, :, n_l])
    return t_mat.astype(BF16), w_pack.astype(BF16), v_pack.astype(BF16), a_re, a_im


def _chunk_view(x, bsz, seq):
    d = x.shape[1]
    return x.reshape(bsz, seq // S5_CHUNK, S5_CHUNK * d)


def _norm_matmul_t_kernel(x_ref, g_ref, wt_ref, o_ref, h_ref):
    nc = x_ref.shape[0]
    d = g_ref.shape[1]
    for k in range(S5_POS_PER_STEP):
        h_ref[k * nc:(k + 1) * nc, :] = _rms(x_ref[:, k * d:(k + 1) * d], g_ref[...]).astype(BF16)
    o_ref[...] = _dot_nt(wt_ref[...], h_ref[...]).astype(o_ref.dtype)


def norm_matmul_t(x, g, wt, *, bsz, seq):
    d = x.shape[1]
    n = wt.shape[0]
    nc = seq // S5_CHUNK
    sq = S5_POS_PER_STEP
    steps = S5_CHUNK // sq
    return pl.pallas_call(
        _norm_matmul_t_kernel,
        grid=(bsz, steps),
        in_specs=[
            pl.BlockSpec((None, nc, sq * d), lambda b, s: (b, 0, s)),
            pl.BlockSpec((1, d), lambda b, s: (0, 0)),
            pl.BlockSpec((n, d), lambda b, s: (0, 0)),
        ],
        out_specs=pl.BlockSpec((n, sq * nc), lambda b, s: (0, b * steps + s)),
        out_shape=jax.ShapeDtypeStruct((n, bsz * seq), BF16),
        scratch_shapes=[pltpu.VMEM((sq * nc, d), BF16)],
        compiler_params=_params("parallel", "parallel"),
        name="norm_matmul_t",
    )(_chunk_view(x, bsz, seq), g.reshape(1, d), wt)


def _chunk_operand(u_ref, k, nc):
    h = S5_GROUP
    return jnp.concatenate(
        [u_ref[k * h:(k + 1) * h, s * nc:(s + 1) * nc] for s in range(S5_CHUNK)], axis=0)


def _s5_inject_kernel(u_ref, w_ref, ore_ref, oim_ref, *, bsz, seq):
    nc = ore_ref.shape[0]
    p2 = w_ref.shape[3]
    for b in range(bsz):
        ub = u_ref.at[:, b * seq:(b + 1) * seq]
        u0 = _chunk_operand(ub, 0, nc)
        u1 = _chunk_operand(ub, 1, nc)
        cols = slice(b * p2, (b + 1) * p2)
        ore_ref[:, cols] = (_dot(w_ref[0, 0, 0], u0) + _dot(w_ref[0, 1, 0], u1)).T
        oim_ref[:, cols] = (_dot(w_ref[0, 0, 1], u0) + _dot(w_ref[0, 1, 1], u1)).T


def s5_inject(ut, w_pack, *, bsz, seq):
    n_pair = w_pack.shape[0]
    p2 = w_pack.shape[3]
    nc = seq // S5_CHUNK
    out = jax.ShapeDtypeStruct((nc, n_pair * bsz * p2), F32)
    oblk = pl.BlockSpec((nc, bsz * p2), lambda j: (0, j))
    return pl.pallas_call(
        functools.partial(_s5_inject_kernel, bsz=bsz, seq=seq),
        grid=(n_pair,),
        in_specs=[
            pl.BlockSpec((2 * S5_GROUP, bsz * seq), lambda j: (j, 0)),
            pl.BlockSpec((1,) + w_pack.shape[1:], lambda j: (j, 0, 0, 0, 0)),
        ],
        out_specs=[oblk, oblk],
        out_shape=[out, out],
        compiler_params=_params("parallel"),
        name="s5_inject",
    )(ut, w_pack)


def _s5_scan_kernel(wre_ref, wim_ref, are_ref, aim_ref, xre_ref, xim_ref, *, n_c):
    a_re = are_ref[...]
    a_im = aim_ref[...]

    def body(c, carry):
        s_re, s_im = carry
        row = pl.ds(c, 1)
        xre_ref[row, :] = s_re
        xim_ref[row, :] = s_im
        n_re = a_re * s_re - a_im * s_im + wre_ref[row, :]
        n_im = a_re * s_im + a_im * s_re + wim_ref[row, :]
        return n_re, n_im

    zero = jnp.zeros(a_re.shape, F32)
    lax.fori_loop(0, n_c, body, (zero, zero))


def s5_scan(w_re, w_im, a_re, a_im, *, tl=2048):
    n_c, cols = w_re.shape
    tl = min(tl, cols)
    out = jax.ShapeDtypeStruct((n_c, cols), F32)
    blk = pl.BlockSpec((n_c, tl), lambda j: (0, j))
    coef = pl.BlockSpec((1, tl), lambda j: (0, j))
    return pl.pallas_call(
        functools.partial(_s5_scan_kernel, n_c=n_c),
        grid=(cols // tl,),
        in_specs=[blk, blk, coef, coef],
        out_specs=[blk, blk],
        out_shape=[out, out],
        compiler_params=_params("parallel"),
        name="s5_scan",
    )(w_re, w_im, a_re, a_im)


def _gelu_tanh(y):
    c = np.float32(np.sqrt(2.0 / np.pi))
    return 0.5 * y * (1.0 + jnp.tanh(c * (y + 0.044715 * (y * y * y))))


def _s5_out_kernel(u_ref, t_ref, xre_ref, xim_ref, v_ref, o_ref, *, bsz, seq):
    nc = xre_ref.shape[0]
    p2 = v_ref.shape[-1]
    h = S5_GROUP
    for b in range(bsz):
        ub = u_ref.at[:, b * seq:(b + 1) * seq]
        x_re = xre_ref[:, b * p2:(b + 1) * p2].astype(BF16)
        x_im = xim_ref[:, b * p2:(b + 1) * p2].astype(BF16)
        for k in range(2):
            y = (_dot(t_ref[k], _chunk_operand(ub, k, nc))
                 + _dot_nt(v_ref[0, k, 0], x_re) + _dot_nt(v_ref[0, k, 1], x_im))
            y = _gelu_tanh(y).astype(o_ref.dtype)
            for t in range(S5_CHUNK):
                lanes = slice(b * seq + t * nc, b * seq + (t + 1) * nc)
                o_ref[k * h:(k + 1) * h, lanes] = y[t * h:(t + 1) * h, :]


def s5_out(ut, t_mat, x_re, x_im, v_pack, *, bsz, seq):
    n_pair = v_pack.shape[0]
    p2 = v_pack.shape[-1]
    nc = seq // S5_CHUNK
    lh = t_mat.shape[-1]
    xblk = pl.BlockSpec((nc, bsz * p2), lambda j: (0, j))
    ublk = pl.BlockSpec((2 * S5_GROUP, bsz * seq), lambda j: (j, 0))
    return pl.pallas_call(
        functools.partial(_s5_out_kernel, bsz=bsz, seq=seq),
        grid=(n_pair,),
        in_specs=[
            ublk,
            pl.BlockSpec((2, lh, lh), lambda j: (j, 0, 0)),
            xblk,
            xblk,
            pl.BlockSpec((1,) + v_pack.shape[1:], lambda j: (j, 0, 0, 0, 0)),
        ],
        out_specs=ublk,
        out_shape=jax.ShapeDtypeStruct(ut.shape, BF16),
        compiler_params=_params("parallel"),
        name="s5_out",
    )(ut, t_mat, x_re, x_im, v_pack)


def s5_mixer(ut, bsz, seq, ops):
    t_mat, w_pack, v_pack, a_re, a_im = ops
    w_re, w_im = s5_inject(ut, w_pack, bsz=bsz, seq=seq)
    x_re, x_im = s5_scan(w_re, w_im, a_re, a_im)
    return s5_out(ut, t_mat, x_re, x_im, v_pack, bsz=bsz, seq=seq)


def _block_pairs(w, tn):
    k, n2 = w.shape
    nj = n2 // (2 * tn)
    return jnp.transpose(w.reshape(k, 2, nj, tn), (2, 0, 1, 3)).reshape(nj, k, 2 * tn)


def _glu_res_kernel(yt_ref, w_ref, x_ref, g_ref, o_ref, y_ref, m_ref, *, nj, tn):
    j = pl.program_id(2)

    @pl.when(j == 0)
    def _():
        y_ref[...] = yt_ref[...].astype(F32).T.astype(BF16)

    z = _dot(y_ref[...], w_ref[...])
    m_ref[j] = z[:, :tn] * _sigmoid(z[:, tn:])

    @pl.when(j == nj - 1)
    def _():
        nc = x_ref.shape[0]
        d = g_ref.shape[1]
        ss = None
        for jj in range(nj):
            mj = m_ref[jj]
            s = jnp.sum(mj * mj, axis=-1, keepdims=True)
            ss = s if ss is None else ss + s
        inv = lax.rsqrt(ss / d + EPS)
        for k in range(S5_POS_PER_STEP):
            rows = slice(k * nc, (k + 1) * nc)
            for jj in range(nj):
                cols = slice(k * d + jj * tn, k * d + (jj + 1) * tn)
                o_ref[:, cols] = (x_ref[:, cols]
                                  + m_ref[jj, rows, :] * inv[rows] * g_ref[:, jj * tn:(jj + 1) * tn])


def glu_res(yt, w_glu, x, g, *, bsz, seq):
    k, _ = yt.shape
    nj = w_glu.shape[0]
    tn = w_glu.shape[2] // 2
    d = nj * tn
    nc = seq // S5_CHUNK
    sq = S5_POS_PER_STEP
    steps = S5_CHUNK // sq
    xblk = pl.BlockSpec((None, nc, sq * d), lambda b, s, j: (b, 0, s))
    out = pl.pallas_call(
        functools.partial(_glu_res_kernel, nj=nj, tn=tn),
        grid=(bsz, steps, nj),
        in_specs=[
            pl.BlockSpec((k, sq * nc), lambda b, s, j: (0, b * steps + s)),
            pl.BlockSpec((None, k, 2 * tn), lambda b, s, j: (j, 0, 0)),
            xblk,
            pl.BlockSpec((1, d), lambda b, s, j: (0, 0)),
        ],
        out_specs=xblk,
        out_shape=jax.ShapeDtypeStruct((bsz, nc, S5_CHUNK * d), F32),
        scratch_shapes=[pltpu.VMEM((sq * nc, k), BF16), pltpu.VMEM((nj, sq * nc, tn), F32)],
        compiler_params=_params("parallel", "parallel", "arbitrary"),
        name="glu_res",
    )(yt, w_glu, _chunk_view(x, bsz, seq), g.reshape(1, d))
    return out.reshape(bsz * seq, d)


def _band_bias(rel_bias):
    c = ATT_CHUNK
    nq = ATT_QGROUP * c
    nk = (ATT_QGROUP + ATT_LEFT) * c
    r = rel_bias.astype(F32)
    n_h = r.shape[0]
    d_lo = ATT_LEFT * c - (nk - 1)
    d_hi = ATT_LEFT * c + nq - 1
    n_lo = -(c - 1) - d_lo
    n_hi = d_hi - ATT_MAX_REL
    e = jnp.concatenate([jnp.broadcast_to(r[:, :1], (n_h, n_lo)), r,
                         jnp.broadcast_to(r[:, -1:], (n_h, n_hi))], axis=1)
    width = nq + nk - 1
    assert e.shape[1] == width
    frev = jnp.concatenate([e[:, ::-1], jnp.zeros((n_h, 1), F32)], axis=1)
    skew = jnp.tile(frev, (1, nq))[:, :nq * width].reshape(n_h, nq, width)
    bias = skew[:, :, nq - 1:nq - 1 + nk]
    qa = np.arange(nq)[:, None] // c
    kc = np.arange(nk)[None, :] // c
    valid = (kc >= qa) & (kc <= qa + ATT_LEFT)
    return jnp.where(jnp.asarray(valid)[None], bias, NEG_INF)


def _band_attn_kernel(q_ref, k_ref, v_ref, b_ref, o_ref, *, seq):
    c = ATT_CHUNK
    nq = ATT_QGROUP * c
    nk = (ATT_QGROUP + ATT_LEFT) * c
    for gi in range(seq // nq):
        q0 = gi * nq
        k0 = q0 - ATT_LEFT * c
        off = max(0, -k0)
        k0 = k0 + off
        q = q_ref[q0:q0 + nq, :]
        k = k_ref[k0:q0 + nq, :]
        v = v_ref[k0:q0 + nq, :]
        s = _dot_nt(q, k) + b_ref[0, :, off:nk]
        m = jnp.max(s, axis=-1, keepdims=True)
        p = jnp.exp(s - m)
        l = jnp.sum(p, axis=-1, keepdims=True)
        o = _dot(p.astype(BF16), v) / l
        o_ref[q0:q0 + nq, :] = o.astype(o_ref.dtype)


def band_attention(q, kv, bias, *, bsz, seq):
    m, d = q.shape
    n_h = d // ATT_HEAD_DIM
    dh = ATT_HEAD_DIM
    return pl.pallas_call(
        functools.partial(_band_attn_kernel, seq=seq),
        grid=(bsz, n_h),
        in_specs=[
            pl.BlockSpec((seq, dh), lambda b, h: (b, h)),
            pl.BlockSpec((seq, dh), lambda b, h: (b, h)),
            pl.BlockSpec((seq, dh), lambda b, h: (b, h + n_h)),
            pl.BlockSpec((1,) + bias.shape[1:], lambda b, h: (h, 0, 0)),
        ],
        out_specs=pl.BlockSpec((seq, dh), lambda b, h: (b, h)),
        out_shape=jax.ShapeDtypeStruct((m, d), BF16),
        compiler_params=_params("parallel", "parallel"),
        name="band_attention",
    )(q, kv, kv, bias)


def _mem_attn_kernel(x_ref, g1_ref, wq_ref, k_ref, v_ref, wo_ref, g2_ref, o_ref):
    x = x_ref[...]
    h = _rms(x, g1_ref[...]).astype(BF16)
    q = _dot(h, wq_ref[...]).astype(BF16)
    k = k_ref[...]
    v = v_ref[...]
    dh = MEM_HEAD_DIM
    heads = []
    for hd in range(q.shape[1] // dh):
        sl = slice(hd * dh, (hd + 1) * dh)
        s = _dot_nt(q[:, sl], k[:, sl])
        m = jnp.max(s, axis=-1, keepdims=True)
        p = jnp.exp(s - m)
        l = jnp.sum(p, axis=-1, keepdims=True)
        heads.append((_dot(p.astype(BF16), v[:, sl]) / l).astype(BF16))
    o = jnp.concatenate(heads, axis=-1)
    c = _dot(o, wo_ref[...])
    o_ref[...] = x + _rms(c, g2_ref[...])


def mem_attention(x, g1, wq, kvmem, layer, wo, g2, *, seq, n_mem, tm=ROW_TILE):
    m, d = x.shape
    md = wq.shape[1]
    tiles_per_batch = seq // tm
    return pl.pallas_call(
        _mem_attn_kernel,
        grid=(m // tm,),
        in_specs=[
            pl.BlockSpec((tm, d), lambda i: (i, 0)),
            pl.BlockSpec((1, d), lambda i: (0, 0)),
            pl.BlockSpec((d, md), lambda i: (0, 0)),
            pl.BlockSpec((n_mem, md), lambda i: (i // tiles_per_batch, 2 * layer)),
            pl.BlockSpec((n_mem, md), lambda i: (i // tiles_per_batch, 2 * layer + 1)),
            pl.BlockSpec((md, d), lambda i: (0, 0)),
            pl.BlockSpec((1, d), lambda i: (0, 0)),
        ],
        out_specs=pl.BlockSpec((tm, d), lambda i: (i, 0)),
        out_shape=jax.ShapeDtypeStruct((m, d), F32),
        compiler_params=_params("parallel"),
        name="mem_attention",
    )(x, g1.reshape(1, d), wq, kvmem, kvmem, wo, g2.reshape(1, d))


def _conv_ffn_kernel(x_ref, g1_ref, wu_ref, cw_ref, cb_ref, wd_ref, g2_ref, o_ref, h_ref, acc_ref,
                     tail_ref, *, nj, tiles_per_batch):
    i = pl.program_id(0)
    j = pl.program_id(1)
    tm = x_ref.shape[0]
    tf = wd_ref.shape[0]
    hm = tm // FFN_ROW_SPLIT

    @pl.when(j == 0)
    def _():
        h_ref[...] = _rms(x_ref[...], g1_ref[...]).astype(h_ref.dtype)
        acc_ref[...] = jnp.zeros(acc_ref.shape, F32)

    @pl.when(i % tiles_per_batch == 0)
    def _():
        tail_ref[j] = jnp.zeros(tail_ref.shape[1:], F32)

    prev = tail_ref[j]
    cw = cw_ref[...]
    cb = cb_ref[...]

    for r in range(FFN_ROW_SPLIT):
        rows = slice(r * hm, (r + 1) * hm)
        up = _dot(h_ref[rows, :], wu_ref[...])
        ext = jnp.concatenate([prev, up], axis=0)
        prev = up[hm - SUBLANES:, :]
        back1 = pltpu.roll(ext, 1, axis=0)[SUBLANES:, :]
        back2 = pltpu.roll(ext, 2, axis=0)[SUBLANES:, :]
        c = cb + back2 * cw[0:1, :] + back1 * cw[1:2, :] + up * cw[2:3, :]
        gate = c[:, tf:]
        act = (c[:, :tf] * (gate * _sigmoid(gate))).astype(BF16)
        acc_ref[rows, :] += _dot(act, wd_ref[...])

    tail_ref[j] = prev

    @pl.when(j == nj - 1)
    def _():
        o_ref[...] = x_ref[...] + _rms(acc_ref[...], g2_ref[...])


def conv_ffn(x, g1, w_up, conv_w, conv_b, w_down, g2, *, seq, tm=ROW_TILE):
    m, d = x.shape
    nj = w_up.shape[0]
    tf = w_up.shape[2] // 2
    tiles_per_batch = seq // tm
    return pl.pallas_call(
        functools.partial(_conv_ffn_kernel, nj=nj, tiles_per_batch=tiles_per_batch),
        grid=(m // tm, nj),
        in_specs=[
            pl.BlockSpec((tm, d), lambda i, j: (i, 0)),
            pl.BlockSpec((1, d), lambda i, j: (0, 0)),
            pl.BlockSpec((None, d, 2 * tf), lambda i, j: (j, 0, 0)),
            pl.BlockSpec((None, CONV_W, 2 * tf), lambda i, j: (j, 0, 0)),
            pl.BlockSpec((None, 1, 2 * tf), lambda i, j: (j, 0, 0)),
            pl.BlockSpec((tf, d), lambda i, j: (j, 0)),
            pl.BlockSpec((1, d), lambda i, j: (0, 0)),
        ],
        out_specs=pl.BlockSpec((tm, d), lambda i, j: (i, 0)),
        out_shape=jax.ShapeDtypeStruct((m, d), F32),
        scratch_shapes=[
            pltpu.VMEM((tm, d), BF16),
            pltpu.VMEM((tm, d), F32),
            pltpu.VMEM((nj, SUBLANES, 2 * tf), F32),
        ],
        compiler_params=_params("arbitrary", "arbitrary"),
        name="conv_ffn",
    )(x, g1.reshape(1, d), w_up, conv_w, conv_b, w_down, g2.reshape(1, d))


def kernel(x, mem, norm_mix, norm_mem, norm_ffn, mem_in_norm, a_w_in, a_lam_re, a_lam_im, a_log_dt,
           a_b_re, a_b_im, a_c_re, a_c_im, a_d, a_w_glu, kv_norm, w_k, w_v, b_w_q, b_rel_bias, b_w_o,
           m_w_q, m_w_kv, m_w_o, f_w_up, f_conv_w, f_conv_b, f_w_down):
    bsz, seq, d = x.shape
    depth = norm_mix.shape[0]
    n_a = a_w_in.shape[0]
    n_mem = mem.shape[1]
    md = m_w_q.shape[2]
    assert seq % ROW_TILE == 0 and seq % (ATT_QGROUP * ATT_CHUNK) == 0
    assert (seq // S5_CHUNK) % LANES == 0 and (d // S5_GROUP) % 2 == 0

    xf = x.reshape(bsz * seq, d)

    w_kv_all = jnp.transpose(m_w_kv, (1, 0, 2)).reshape(d, depth * 2 * md).astype(BF16)
    kvmem = norm_matmul(mem.reshape(bsz * n_mem, d), mem_in_norm, w_kv_all)

    kv = None
    for l in range(depth):
        if l == n_a:
            w_kv = jnp.concatenate([w_k, w_v], axis=1).astype(BF16)
            kv = norm_matmul(xf, kv_norm, w_kv)
        if l < n_a:
            ut = norm_matmul_t(xf, norm_mix[l, 0], a_w_in[l].T.astype(BF16), bsz=bsz, seq=seq)
            ops = _s5_operators(a_lam_re[l], a_lam_im[l], a_log_dt[l], a_b_re[l], a_b_im[l],
                                a_c_re[l], a_c_im[l], a_d[l], bsz)
            yt = s5_mixer(ut, bsz, seq, ops)
            w_glu = _block_pairs(a_w_glu[l], min(GLU_COL_TILE, d)).astype(BF16)
            xf = glu_res(yt, w_glu, xf, norm_mix[l, 1], bsz=bsz, seq=seq)
        else:
            jb = l - n_a
            w_q = (b_w_q[jb] * (ATT_HEAD_DIM ** -0.5)).astype(BF16)
            q = norm_matmul(xf, norm_mix[l, 0], w_q)
            o = band_attention(q, kv, _band_bias(b_rel_bias[jb]), bsz=bsz, seq=seq)
            xf = proj_res(o, b_w_o[jb].astype(BF16), xf, norm_mix[l, 1])
        xf = mem_attention(xf, norm_mem[l, 0], (m_w_q[l] * (MEM_HEAD_DIM ** -0.5)).astype(BF16),
                           kvmem, l, m_w_o[l].astype(BF16), norm_mem[l, 1], seq=seq, n_mem=n_mem)
        xf = conv_ffn(xf, norm_ffn[l, 0], _block_pairs(f_w_up[l], FFN_COL_TILE).astype(BF16),
                      _block_pairs(f_conv_w[l], FFN_COL_TILE),
                      _block_pairs(f_conv_b[l][None], FFN_COL_TILE),
                      f_w_down[l].astype(BF16), norm_ffn[l, 1], seq=seq)
    return xf.reshape(bsz, seq, d)
```

```python
import functools

import numpy as np
import jax
import jax.numpy as jnp
from jax import lax
from jax.experimental import pallas as pl
from jax.experimental.pallas import tpu as pltpu

EPS = 1e-6
NEG_INF = -1e30
BF16 = jnp.bfloat16
F32 = jnp.float32

S5_GROUP = 16
S5_CHUNK = 16
S5_POS_PER_STEP = 4
ATT_CHUNK = 64
ATT_LEFT = 8
ATT_HEAD_DIM = 128
ATT_MAX_REL = 256
ATT_QGROUP = 4
MEM_HEAD_DIM = 128
CONV_W = 3

VMEM_LIMIT_BYTES = 56 * 1024 * 1024
ROW_TILE = 512
COL_TILE = 2048
FFN_ROW_SPLIT = 2
FFN_COL_TILE = 512
GLU_COL_TILE = 512
SUBLANES = 8
LANES = 128

_NT = (((1,), (1,)), ((), ()))


def _params(*sem):
    return pltpu.CompilerParams(dimension_semantics=sem, vmem_limit_bytes=VMEM_LIMIT_BYTES)


def _rms(x, g):
    ms = jnp.mean(x * x, axis=-1, keepdims=True)
    return x * lax.rsqrt(ms + EPS) * g


def _sigmoid(x):
    return 1.0 / (1.0 + jnp.exp(-x))


def _dot(a, b):
    return jnp.dot(a, b, preferred_element_type=F32)


def _dot_nt(a, b):
    return lax.dot_general(a, b, _NT, preferred_element_type=F32)


def _norm_matmul_kernel(x_ref, g_ref, w_ref, o_ref, h_ref):
    @pl.when(pl.program_id(1) == 0)
    def _():
        h_ref[...] = _rms(x_ref[...], g_ref[...]).astype(h_ref.dtype)

    o_ref[...] = _dot(h_ref[...], w_ref[...]).astype(o_ref.dtype)


def norm_matmul(x, g, w, *, tm=ROW_TILE, tn=COL_TILE, out_dtype=BF16):
    m, k = x.shape
    n = w.shape[1]
    tn = min(tn, n)
    return pl.pallas_call(
        _norm_matmul_kernel,
        grid=(m // tm, n // tn),
        in_specs=[
            pl.BlockSpec((tm, k), lambda i, j: (i, 0)),
            pl.BlockSpec((1, k), lambda i, j: (0, 0)),
            pl.BlockSpec((k, tn), lambda i, j: (0, j)),
        ],
        out_specs=pl.BlockSpec((tm, tn), lambda i, j: (i, j)),
        out_shape=jax.ShapeDtypeStruct((m, n), out_dtype),
        scratch_shapes=[pltpu.VMEM((tm, k), BF16)],
        compiler_params=_params("parallel", "arbitrary"),
        name="norm_matmul",
    )(x, g.reshape(1, k), w)


def _proj_res_kernel(a_ref, w_ref, x_ref, g_ref, o_ref):
    f = _dot(a_ref[...], w_ref[...])
    o_ref[...] = x_ref[...] + _rms(f, g_ref[...])


def proj_res(a, w, x, g, *, tm=ROW_TILE):
    m, k = a.shape
    d = w.shape[1]
    return pl.pallas_call(
        _proj_res_kernel,
        grid=(m // tm,),
        in_specs=[
            pl.BlockSpec((tm, k), lambda i: (i, 0)),
            pl.BlockSpec((k, d), lambda i: (0, 0)),
            pl.BlockSpec((tm, d), lambda i: (i, 0)),
            pl.BlockSpec((1, d), lambda i: (0, 0)),
        ],
        out_specs=pl.BlockSpec((tm, d), lambda i: (i, 0)),
        out_shape=jax.ShapeDtypeStruct((m, d), F32),
        compiler_params=_params("parallel"),
        name="proj_res",
    )(a, w, x, g.reshape(1, d))


def _lag_kernel_body(a_ref, b_ref, o_ref):
    for g in range(a_ref.shape[0]):
        o_ref[g] = jnp.dot(a_ref[g], b_ref[g], precision=lax.Precision.HIGHEST,
                           preferred_element_type=F32)


def _lag_kernels(ca_cat, bb_cat, *, gb=8):
    n_g, lh, p2 = ca_cat.shape
    n_h = bb_cat.shape[-1]
    return pl.pallas_call(
        _lag_kernel_body,
        grid=(n_g // gb,),
        in_specs=[
            pl.BlockSpec((gb, lh, p2), lambda i: (i, 0, 0)),
            pl.BlockSpec((gb, p2, n_h), lambda i: (i, 0, 0)),
        ],
        out_specs=pl.BlockSpec((gb, lh, n_h), lambda i: (i, 0, 0)),
        out_shape=jax.ShapeDtypeStruct((n_g, lh, n_h), F32),
        compiler_params=_params("parallel"),
        name="s5_lag_kernels",
    )(ca_cat, bb_cat)


def _s5_operators(lam_re, lam_im, log_dt, b_re, b_im, c_re, c_im, d_skip, bsz):
    n_g, n_p = lam_re.shape
    n_h = S5_GROUP
    n_l = S5_CHUNK
    lh = n_l * n_h
    lr = lam_re.astype(F32)
    li = lam_im.astype(F32)
    dt = jnp.exp(log_dt.astype(F32))[:, None]
    mag = jnp.exp(lr * dt)
    ang = li * dt
    ab_re = mag * jnp.cos(ang)
    ab_im = mag * jnp.sin(ang)
    den = lr * lr + li * li
    nr = ab_re - 1.0
    f_re = (nr * lr + ab_im * li) / den
    f_im = (ab_im * lr - nr * li) / den
    br = b_re.astype(F32)
    bi = b_im.astype(F32)
    bb_re = f_re[..., None] * br - f_im[..., None] * bi
    bb_im = f_re[..., None] * bi + f_im[..., None] * br
    tau = jnp.arange(n_l + 1, dtype=F32)
    pw_re = jnp.exp((lr * dt)[..., None] * tau) * jnp.cos(ang[..., None] * tau)
    pw_im = jnp.exp((lr * dt)[..., None] * tau) * jnp.sin(ang[..., None] * tau)
    tau_m = tau[None, :, None]
    pt_re = jnp.exp((lr * dt)[:, None, :] * tau_m) * jnp.cos(ang[:, None, :] * tau_m)
    pt_im = jnp.exp((lr * dt)[:, None, :] * tau_m) * jnp.sin(ang[:, None, :] * tau_m)
    cr = c_re.astype(F32)[:, None]
    ci = c_im.astype(F32)[:, None]
    ca_re = cr * pt_re[:, :, None, :] - ci * pt_im[:, :, None, :]
    ca_im = cr * pt_im[:, :, None, :] + ci * pt_re[:, :, None, :]
    cr_o = c_re.astype(F32)[:, :, None, :]
    ci_o = c_im.astype(F32)[:, :, None, :]
    pl_re = pt_re[:, None, :n_l, :]
    pl_im = pt_im[:, None, :n_l, :]
    ca_cat = jnp.concatenate([cr_o * pl_re - ci_o * pl_im, -(cr_o * pl_im + ci_o * pl_re)], axis=-1)
    bb_cat = jnp.concatenate([bb_re, bb_im], axis=1)
    kern = _lag_kernels(ca_cat.reshape(n_g, lh, 2 * n_p), bb_cat).reshape(n_g, n_h, n_l, n_h)
    krev = kern[:, :, ::-1, :].reshape(n_g, n_h, lh)
    rows = []
    for t in range(n_l):
        sh = (n_l - 1 - t) * n_h
        rows.append(jnp.pad(krev[:, :, sh:], ((0, 0), (0, 0), (0, sh))))
    t_mat = jnp.stack(rows, axis=1).reshape(n_g, lh, lh)
    skip = jnp.tile(d_skip.astype(F32).reshape(n_g, n_h), (1, n_l))
    t_mat = t_mat + jnp.eye(lh, dtype=F32)[None] * skip[:, :, None]
    rev_re = pw_re[:, :, n_l - 1::-1]
    rev_im = pw_im[:, :, n_l - 1::-1]
    w_re = (rev_re[..., None] * bb_re[:, :, None, :] - rev_im[..., None] * bb_im[:, :, None, :])
    w_im = (rev_re[..., None] * bb_im[:, :, None, :] + rev_im[..., None] * bb_re[:, :, None, :])
    def slot_rows(w):
        w = w.reshape(n_g // 2, 2, n_p, lh)
        z = jnp.zeros_like(w[:, 0])
        return jnp.stack([jnp.concatenate([w[:, 0], z], 1), jnp.concatenate([z, w[:, 1]], 1)], 1)
    w_pack = jnp.stack([slot_rows(w_re), slot_rows(w_im)], axis=2)
    def slot_cols(v):
        v = v.reshape(n_g // 2, 2, lh, n_p)
        z = jnp.zeros_like(v[:, 0])
        return jnp.stack([jnp.concatenate([v[:, 0], z], -1), jnp.concatenate([z, v[:, 1]], -1)], 1)
    v_pack = jnp.stack([slot_cols(ca_re[:, 1:]), slot_cols(-ca_im[:, 1:])], axis=2)
    def state_cols(a):
        a = jnp.broadcast_to(a.reshape(n_g // 2, 1, 2 * n_p), (n_g // 2, bsz, 2 * n_p))
        return a.reshape(1, n_g * bsz * n_p)
    a_re = state_cols(pw_re[:, :, n_l])
    a_im = state_cols(pw_im[:, :, n_l])
    return t_mat.astype(BF16), w_pack.astype(BF16), v_pack.astype(BF16), a_re, a_im


def _chunk_view(x, bsz, seq):
    d = x.shape[1]
    return x.reshape(bsz, seq // S5_CHUNK, S5_CHUNK * d)


def _norm_matmul_t_kernel(x_ref, g_ref, wt_ref, o_ref, h_ref):
    nc = x_ref.shape[0]
    d = g_ref.shape[1]
    for k in range(S5_POS_PER_STEP):
        h_ref[k * nc:(k + 1) * nc, :] = _rms(x_ref[:, k * d:(k + 1) * d], g_ref[...]).astype(BF16)
    o_ref[...] = _dot_nt(wt_ref[...], h_ref[...]).astype(o_ref.dtype)


def norm_matmul_t(x, g, wt, *, bsz, seq):
    d = x.shape[1]
    n = wt.shape[0]
    nc = seq // S5_CHUNK
    sq = S5_POS_PER_STEP
    steps = S5_CHUNK // sq
    return pl.pallas_call(
        _norm_matmul_t_kernel,
        grid=(bsz, steps),
        in_specs=[
            pl.BlockSpec((None, nc, sq * d), lambda b, s: (b, 0, s)),
            pl.BlockSpec((1, d), lambda b, s: (0, 0)),
            pl.BlockSpec((n, d), lambda b, s: (0, 0)),
        ],
        out_specs=pl.BlockSpec((n, sq * nc), lambda b, s: (0, b * steps + s)),
        out_shape=jax.ShapeDtypeStruct((n, bsz * seq), BF16),
        scratch_shapes=[pltpu.VMEM((sq * nc, d), BF16)],
        compiler_params=_params("parallel", "parallel"),
        name="norm_matmul_t",
    )(_chunk_view(x, bsz, seq), g.reshape(1, d), wt)


def _chunk_operand(u_ref, k, nc):
    h = S5_GROUP
    return jnp.concatenate(
        [u_ref[k * h:(k + 1) * h, s * nc:(s + 1) * nc] for s in range(S5_CHUNK)], axis=0)


def _s5_inject_kernel(u_ref, w_ref, ore_ref, oim_ref, *, bsz, seq):
    nc = ore_ref.shape[0]
    p2 = w_ref.shape[3]
    for b in range(bsz):
        ub = u_ref.at[:, b * seq:(b + 1) * seq]
        u0 = _chunk_operand(ub, 0, nc)
        u1 = _chunk_operand(ub, 1, nc)
        cols = slice(b * p2, (b + 1) * p2)
        ore_ref[:, cols] = (_dot(w_ref[0, 0, 0], u0) + _dot(w_ref[0, 1, 0], u1)).T
        oim_ref[:, cols] = (_dot(w_ref[0, 0, 1], u0) + _dot(w_ref[0, 1, 1], u1)).T


def s5_inject(ut, w_pack, *, bsz, seq):
    n_pair = w_pack.shape[0]
    p2 = w_pack.shape[3]
    nc = seq // S5_CHUNK
    out = jax.ShapeDtypeStruct((nc, n_pair * bsz * p2), F32)
    oblk = pl.BlockSpec((nc, bsz * p2), lambda j: (0, j))
    return pl.pallas_call(
        functools.partial(_s5_inject_kernel, bsz=bsz, seq=seq),
        grid=(n_pair,),
        in_specs=[
            pl.BlockSpec((2 * S5_GROUP, bsz * seq), lambda j: (j, 0)),
            pl.BlockSpec((1,) + w_pack.shape[1:], lambda j: (j, 0, 0, 0, 0)),
        ],
        out_specs=[oblk, oblk],
        out_shape=[out, out],
        compiler_params=_params("parallel"),
        name="s5_inject",
    )(ut, w_pack)


def _s5_scan_kernel(wre_ref, wim_ref, are_ref, aim_ref, xre_ref, xim_ref, *, n_c):
    a_re = are_ref[...]
    a_im = aim_ref[...]

    def body(c, carry):
        s_re, s_im = carry
        row = pl.ds(c, 1)
        xre_ref[row, :] = s_re
        xim_ref[row, :] = s_im
        n_re = a_re * s_re - a_im * s_im + wre_ref[row, :]
        n_im = a_re * s_im + a_im * s_re + wim_ref[row, :]
        return n_re, n_im

    zero = jnp.zeros(a_re.shape, F32)
    lax.fori_loop(0, n_c, body, (zero, zero))


def s5_scan(w_re, w_im, a_re, a_im, *, tl=2048):
    n_c, cols = w_re.shape
    tl = min(tl, cols)
    out = jax.ShapeDtypeStruct((n_c, cols), F32)
    blk = pl.BlockSpec((n_c, tl), lambda j: (0, j))
    coef = pl.BlockSpec((1, tl), lambda j: (0, j))
    return pl.pallas_call(
        functools.partial(_s5_scan_kernel, n_c=n_c),
        grid=(cols // tl,),
        in_specs=[blk, blk, coef, coef],
        out_specs=[blk, blk],
        out_shape=[out, out],
        compiler_params=_params("parallel"),
        name="s5_scan",
    )(w_re, w_im, a_re, a_im)


def _gelu_tanh(y):
    c = np.float32(np.sqrt(2.0 / np.pi))
    return 0.5 * y * (1.0 + jnp.tanh(c * (y + 0.044715 * (y * y * y))))


def _s5_out_kernel(u_ref, t_ref, xre_ref, xim_ref, v_ref, o_ref, *, bsz, seq):
    nc = xre_ref.shape[0]
    p2 = v_ref.shape[-1]
    h = S5_GROUP
    for b in range(bsz):
        ub = u_ref.at[:, b * seq:(b + 1) * seq]
        x_re = xre_ref[:, b * p2:(b + 1) * p2].astype(BF16)
        x_im = xim_ref[:, b * p2:(b + 1) * p2].astype(BF16)
        for k in range(2):
            y = (_dot(t_ref[k], _chunk_operand(ub, k, nc))
                 + _dot_nt(v_ref[0, k, 0], x_re) + _dot_nt(v_ref[0, k, 1], x_im))
            y = _gelu_tanh(y).astype(o_ref.dtype)
            for t in range(S5_CHUNK):
                lanes = slice(b * seq + t * nc, b * seq + (t + 1) * nc)
                o_ref[k * h:(k + 1) * h, lanes] = y[t * h:(t + 1) * h, :]


def s5_out(ut, t_mat, x_re, x_im, v_pack, *, bsz, seq):
    n_pair = v_pack.shape[0]
    p2 = v_pack.shape[-1]
    nc = seq // S5_CHUNK
    lh = t_mat.shape[-1]
    xblk = pl.BlockSpec((nc, bsz * p2), lambda j: (0, j))
    ublk = pl.BlockSpec((2 * S5_GROUP, bsz * seq), lambda j: (j, 0))
    return pl.pallas_call(
        functools.partial(_s5_out_kernel, bsz=bsz, seq=seq),
        grid=(n_pair,),
        in_specs=[
            ublk,
            pl.BlockSpec((2, lh, lh), lambda j: (j, 0, 0)),
            xblk,
            xblk,
            pl.BlockSpec((1,) + v_pack.shape[1:], lambda j: (j, 0, 0, 0, 0)),
        ],
        out_specs=ublk,
        out_shape=jax.ShapeDtypeStruct(ut.shape, BF16),
        compiler_params=_params("parallel"),
        name="s5_out",
    )(ut, t_mat, x_re, x_im, v_pack)


def s5_mixer(ut, bsz, seq, ops):
    t_mat, w_pack, v_pack, a_re, a_im = ops
    w_re, w_im = s5_inject(ut, w_pack, bsz=bsz, seq=seq)
    x_re, x_im = s5_scan(w_re, w_im, a_re, a_im)
    return s5_out(ut, t_mat, x_re, x_im, v_pack, bsz=bsz, seq=seq)


def _glu_res_kernel(yt_ref, wv_ref, wg_ref, x_ref, g_ref, o_ref, y_ref, m_ref, *, nj, tn):
    j = pl.program_id(2)

    @pl.when(j == 0)
    def _():
        y_ref[...] = yt_ref[...].astype(F32).T.astype(BF16)

    y = y_ref[...]
    m_ref[j] = _dot(y, wv_ref[...]) * _sigmoid(_dot(y, wg_ref[...]))

    @pl.when(j == nj - 1)
    def _():
        nc = x_ref.shape[0]
        d = g_ref.shape[1]
        ss = None
        for jj in range(nj):
            mj = m_ref[jj]
            s = jnp.sum(mj * mj, axis=-1, keepdims=True)
            ss = s if ss is None else ss + s
        inv = lax.rsqrt(ss / d + EPS)
        for k in range(S5_POS_PER_STEP):
            rows = slice(k * nc, (k + 1) * nc)
            for jj in range(nj):
                cols = slice(k * d + jj * tn, k * d + (jj + 1) * tn)
                o_ref[:, cols] = (x_ref[:, cols]
                                  + m_ref[jj, rows, :] * inv[rows] * g_ref[:, jj * tn:(jj + 1) * tn])


def glu_res(yt, w_glu, x, g, *, bsz, seq, tn=GLU_COL_TILE):
    k, _ = yt.shape
    d = w_glu.shape[1] // 2
    tn = min(tn, d)
    nj = d // tn
    nc = seq // S5_CHUNK
    sq = S5_POS_PER_STEP
    steps = S5_CHUNK // sq
    xblk = pl.BlockSpec((None, nc, sq * d), lambda b, s, j: (b, 0, s))
    out = pl.pallas_call(
        functools.partial(_glu_res_kernel, nj=nj, tn=tn),
        grid=(bsz, steps, nj),
        in_specs=[
            pl.BlockSpec((k, sq * nc), lambda b, s, j: (0, b * steps + s)),
            pl.BlockSpec((k, tn), lambda b, s, j: (0, j)),
            pl.BlockSpec((k, tn), lambda b, s, j: (0, j + nj)),
            xblk,
            pl.BlockSpec((1, d), lambda b, s, j: (0, 0)),
        ],
        out_specs=xblk,
        out_shape=jax.ShapeDtypeStruct((bsz, nc, S5_CHUNK * d), F32),
        scratch_shapes=[pltpu.VMEM((sq * nc, k), BF16), pltpu.VMEM((nj, sq * nc, tn), F32)],
        compiler_params=_params("parallel", "parallel", "arbitrary"),
        name="glu_res",
    )(yt, w_glu, w_glu, _chunk_view(x, bsz, seq), g.reshape(1, d))
    return out.reshape(bsz * seq, d)


def _band_bias(rel_bias):
    c = ATT_CHUNK
    nq = ATT_QGROUP * c
    nk = (ATT_QGROUP + ATT_LEFT) * c
    r = rel_bias.astype(F32)
    n_h = r.shape[0]
    d_lo = ATT_LEFT * c - (nk - 1)
    d_hi = ATT_LEFT * c + nq - 1
    n_lo = -(c - 1) - d_lo
    n_hi = d_hi - ATT_MAX_REL
    e = jnp.concatenate([jnp.broadcast_to(r[:, :1], (n_h, n_lo)), r,
                         jnp.broadcast_to(r[:, -1:], (n_h, n_hi))], axis=1)
    width = nq + nk - 1
    assert e.shape[1] == width
    frev = jnp.concatenate([e[:, ::-1], jnp.zeros((n_h, 1), F32)], axis=1)
    skew = jnp.tile(frev, (1, nq))[:, :nq * width].reshape(n_h, nq, width)
    bias = skew[:, :, nq - 1:nq - 1 + nk]
    qa = np.arange(nq)[:, None] // c
    kc = np.arange(nk)[None, :] // c
    valid = (kc >= qa) & (kc <= qa + ATT_LEFT)
    return jnp.where(jnp.asarray(valid)[None], bias, NEG_INF)


def _band_attn_kernel(q_ref, k_ref, v_ref, b_ref, o_ref, *, seq):
    c = ATT_CHUNK
    nq = ATT_QGROUP * c
    nk = (ATT_QGROUP + ATT_LEFT) * c
    for gi in range(seq // nq):
        q0 = gi * nq
        k0 = q0 - ATT_LEFT * c
        off = max(0, -k0)
        k0 = k0 + off
        q = q_ref[q0:q0 + nq, :]
        k = k_ref[k0:q0 + nq, :]
        v = v_ref[k0:q0 + nq, :]
        s = _dot_nt(q, k) + b_ref[0, :, off:nk]
        m = jnp.max(s, axis=-1, keepdims=True)
        p = jnp.exp(s - m)
        l = jnp.sum(p, axis=-1, keepdims=True)
        o = _dot(p.astype(BF16), v) / l
        o_ref[q0:q0 + nq, :] = o.astype(o_ref.dtype)


def band_attention(q, kv, bias, *, bsz, seq):
    m, d = q.shape
    n_h = d // ATT_HEAD_DIM
    dh = ATT_HEAD_DIM
    return pl.pallas_call(
        functools.partial(_band_attn_kernel, seq=seq),
        grid=(bsz, n_h),
        in_specs=[
            pl.BlockSpec((seq, dh), lambda b, h: (b, h)),
            pl.BlockSpec((seq, dh), lambda b, h: (b, h)),
            pl.BlockSpec((seq, dh), lambda b, h: (b, h + n_h)),
            pl.BlockSpec((1,) + bias.shape[1:], lambda b, h: (h, 0, 0)),
        ],
        out_specs=pl.BlockSpec((seq, dh), lambda b, h: (b, h)),
        out_shape=jax.ShapeDtypeStruct((m, d), BF16),
        compiler_params=_params("parallel", "parallel"),
        name="band_attention",
    )(q, kv, kv, bias)


def _mem_attn_kernel(x_ref, g1_ref, wq_ref, k_ref, v_ref, wo_ref, g2_ref, o_ref):
    x = x_ref[...]
    h = _rms(x, g1_ref[...]).astype(BF16)
    q = _dot(h, wq_ref[...]).astype(BF16)
    k = k_ref[...]
    v = v_ref[...]
    dh = MEM_HEAD_DIM
    heads = []
    for hd in range(q.shape[1] // dh):
        sl = slice(hd * dh, (hd + 1) * dh)
        s = _dot_nt(q[:, sl], k[:, sl])
        m = jnp.max(s, axis=-1, keepdims=True)
        p = jnp.exp(s - m)
        l = jnp.sum(p, axis=-1, keepdims=True)
        heads.append((_dot(p.astype(BF16), v[:, sl]) / l).astype(BF16))
    o = jnp.concatenate(heads, axis=-1)
    c = _dot(o, wo_ref[...])
    o_ref[...] = x + _rms(c, g2_ref[...])


def mem_attention(x, g1, wq, kvmem, layer, wo, g2, *, seq, n_mem, tm=ROW_TILE):
    m, d = x.shape
    md = wq.shape[1]
    tiles_per_batch = seq // tm
    return pl.pallas_call(
        _mem_attn_kernel,
        grid=(m // tm,),
        in_specs=[
            pl.BlockSpec((tm, d), lambda i: (i, 0)),
            pl.BlockSpec((1, d), lambda i: (0, 0)),
            pl.BlockSpec((d, md), lambda i: (0, 0)),
            pl.BlockSpec((n_mem, md), lambda i: (i // tiles_per_batch, 2 * layer)),
            pl.BlockSpec((n_mem, md), lambda i: (i // tiles_per_batch, 2 * layer + 1)),
            pl.BlockSpec((md, d), lambda i: (0, 0)),
            pl.BlockSpec((1, d), lambda i: (0, 0)),
        ],
        out_specs=pl.BlockSpec((tm, d), lambda i: (i, 0)),
        out_shape=jax.ShapeDtypeStruct((m, d), F32),
        compiler_params=_params("parallel"),
        name="mem_attention",
    )(x, g1.reshape(1, d), wq, kvmem, kvmem, wo, g2.reshape(1, d))


def _conv_ffn_kernel(x_ref, g1_ref, wuv_ref, wug_ref, cwv_ref, cwg_ref, cbv_ref, cbg_ref, wd_ref,
                     g2_ref, o_ref, h_ref, acc_ref, tail_ref, up_ref, *, nj, tiles_per_batch):
    i = pl.program_id(0)
    j = pl.program_id(1)
    tm = x_ref.shape[0]
    hm = tm // FFN_ROW_SPLIT

    @pl.when(j == 0)
    def _():
        h_ref[...] = _rms(x_ref[...], g1_ref[...]).astype(h_ref.dtype)
        acc_ref[...] = jnp.zeros(acc_ref.shape, F32)

    @pl.when(i % tiles_per_batch == 0)
    def _():
        tail_ref[j] = jnp.zeros(tail_ref.shape[1:], F32)

    up_ref[:, 0:SUBLANES, :] = tail_ref[j]

    for r in range(FFN_ROW_SPLIT):
        h = h_ref[r * hm:(r + 1) * hm, :]
        rows = slice(SUBLANES + r * hm, SUBLANES + (r + 1) * hm)
        up_ref[0, rows, :] = _dot(h, wuv_ref[...])
        up_ref[1, rows, :] = _dot(h, wug_ref[...])

    tail_ref[j] = up_ref[:, tm:tm + SUBLANES, :]

    def conv(slot, base, cw, cb):
        taps = [up_ref[slot, base - (CONV_W - 1 - k):base - (CONV_W - 1 - k) + hm, :]
                for k in range(CONV_W)]
        return cb + taps[0] * cw[0:1, :] + taps[1] * cw[1:2, :] + taps[2] * cw[2:3, :]

    for r in range(FFN_ROW_SPLIT):
        base = SUBLANES + r * hm
        val = conv(0, base, cwv_ref[...], cbv_ref[...])
        gate = conv(1, base, cwg_ref[...], cbg_ref[...])
        act = (val * (gate * _sigmoid(gate))).astype(BF16)
        acc_ref[r * hm:(r + 1) * hm, :] += _dot(act, wd_ref[...])

    @pl.when(j == nj - 1)
    def _():
        o_ref[...] = x_ref[...] + _rms(acc_ref[...], g2_ref[...])


def conv_ffn(x, g1, w_up, conv_w, conv_b, w_down, g2, *, seq, tm=ROW_TILE, tf=FFN_COL_TILE):
    m, d = x.shape
    f = w_down.shape[0]
    nj = f // tf
    tiles_per_batch = seq // tm
    conv_b = conv_b.reshape(1, 2 * f)
    return pl.pallas_call(
        functools.partial(_conv_ffn_kernel, nj=nj, tiles_per_batch=tiles_per_batch),
        grid=(m // tm, nj),
        in_specs=[
            pl.BlockSpec((tm, d), lambda i, j: (i, 0)),
            pl.BlockSpec((1, d), lambda i, j: (0, 0)),
            pl.BlockSpec((d, tf), lambda i, j: (0, j)),
            pl.BlockSpec((d, tf), lambda i, j: (0, j + nj)),
            pl.BlockSpec((CONV_W, tf), lambda i, j: (0, j)),
            pl.BlockSpec((CONV_W, tf), lambda i, j: (0, j + nj)),
            pl.BlockSpec((1, tf), lambda i, j: (0, j)),
            pl.BlockSpec((1, tf), lambda i, j: (0, j + nj)),
            pl.BlockSpec((tf, d), lambda i, j: (j, 0)),
            pl.BlockSpec((1, d), lambda i, j: (0, 0)),
        ],
        out_specs=pl.BlockSpec((tm, d), lambda i, j: (i, 0)),
        out_shape=jax.ShapeDtypeStruct((m, d), F32),
        scratch_shapes=[
            pltpu.VMEM((tm, d), BF16),
            pltpu.VMEM((tm, d), F32),
            pltpu.VMEM((nj, 2, SUBLANES, tf), F32),
            pltpu.VMEM((2, SUBLANES + tm, tf), F32),
        ],
        compiler_params=_params("arbitrary", "arbitrary"),
        name="conv_ffn",
    )(x, g1.reshape(1, d), w_up, w_up, conv_w, conv_w, conv_b, conv_b, w_down, g2.reshape(1, d))


def kernel(x, mem, norm_mix, norm_mem, norm_ffn, mem_in_norm, a_w_in, a_lam_re, a_lam_im, a_log_dt,
           a_b_re, a_b_im, a_c_re, a_c_im, a_d, a_w_glu, kv_norm, w_k, w_v, b_w_q, b_rel_bias, b_w_o,
           m_w_q, m_w_kv, m_w_o, f_w_up, f_conv_w, f_conv_b, f_w_down):
    bsz, seq, d = x.shape
    depth = norm_mix.shape[0]
    n_a = a_w_in.shape[0]
    n_mem = mem.shape[1]
    md = m_w_q.shape[2]
    assert seq % ROW_TILE == 0 and seq % (ATT_QGROUP * ATT_CHUNK) == 0
    assert (seq // S5_CHUNK) % LANES == 0 and (d // S5_GROUP) % 2 == 0

    xf = x.reshape(bsz * seq, d)

    w_kv_all = jnp.transpose(m_w_kv, (1, 0, 2)).reshape(d, depth * 2 * md).astype(BF16)
    kvmem = norm_matmul(mem.reshape(bsz * n_mem, d), mem_in_norm, w_kv_all)

    kv = None
    for l in range(depth):
        if l == n_a:
            w_kv = jnp.concatenate([w_k, w_v], axis=1).astype(BF16)
            kv = norm_matmul(xf, kv_norm, w_kv)
        if l < n_a:
            ut = norm_matmul_t(xf, norm_mix[l, 0], a_w_in[l].T.astype(BF16), bsz=bsz, seq=seq)
            ops = _s5_operators(a_lam_re[l], a_lam_im[l], a_log_dt[l], a_b_re[l], a_b_im[l],
                                a_c_re[l], a_c_im[l], a_d[l], bsz)
            yt = s5_mixer(ut, bsz, seq, ops)
            xf = glu_res(yt, a_w_glu[l].astype(BF16), xf, norm_mix[l, 1], bsz=bsz, seq=seq)
        else:
            jb = l - n_a
            w_q = (b_w_q[jb] * (ATT_HEAD_DIM ** -0.5)).astype(BF16)
            q = norm_matmul(xf, norm_mix[l, 0], w_q)
            o = band_attention(q, kv, _band_bias(b_rel_bias[jb]), bsz=bsz, seq=seq)
            xf = proj_res(o, b_w_o[jb].astype(BF16), xf, norm_mix[l, 1])
        xf = mem_attention(xf, norm_mem[l, 0], (m_w_q[l] * (MEM_HEAD_DIM ** -0.5)).astype(BF16),
                           kvmem, l, m_w_o[l].astype(BF16), norm_mem[l, 1], seq=seq, n_mem=n_mem)
        xf = conv_ffn(xf, norm_ffn[l, 0], f_w_up[l].astype(BF16), f_conv_w[l], f_conv_b[l],
                      f_w_down[l].astype(BF16), norm_ffn[l, 1], seq=seq)
    return xf.reshape(bsz, seq, d)
```

```python
import functools

import numpy as np
import jax
import jax.numpy as jnp
from jax import lax
from jax.experimental import pallas as pl
from jax.experimental.pallas import tpu as pltpu

EPS = 1e-6
NEG_INF = -1e30
BF16 = jnp.bfloat16
F32 = jnp.float32

S5_GROUP = 16
S5_CHUNK = 16
S5_POS_PER_STEP = 4
ATT_CHUNK = 64
ATT_LEFT = 8
ATT_HEAD_DIM = 128
ATT_MAX_REL = 256
ATT_QGROUP = 4
MEM_HEAD_DIM = 128
CONV_W = 3

VMEM_LIMIT_BYTES = 56 * 1024 * 1024
ROW_TILE = 512
COL_TILE = 2048
FFN_ROW_SPLIT = 2
FFN_COL_TILE = 512
GLU_COL_TILE = 512
SUBLANES = 8
LANES = 128

_NT = (((1,), (1,)), ((), ()))


def _params(*sem):
    return pltpu.CompilerParams(dimension_semantics=sem, vmem_limit_bytes=VMEM_LIMIT_BYTES)


def _rms(x, g):
    ms = jnp.mean(x * x, axis=-1, keepdims=True)
    return x * lax.rsqrt(ms + EPS) * g


def _sigmoid(x):
    return 1.0 / (1.0 + jnp.exp(-x))


def _dot(a, b):
    return jnp.dot(a, b, preferred_element_type=F32)


def _dot_nt(a, b):
    return lax.dot_general(a, b, _NT, preferred_element_type=F32)


def _layer_spec(layer, block, index_map):
    return pl.BlockSpec((None,) + tuple(block), lambda *ids: (layer,) + tuple(index_map(*ids)))


def _norm_matmul_kernel(x_ref, g_ref, w_ref, o_ref, h_ref):
    @pl.when(pl.program_id(1) == 0)
    def _():
        h_ref[...] = _rms(x_ref[...], g_ref[...]).astype(h_ref.dtype)

    o_ref[...] = _dot(h_ref[...], w_ref[...]).astype(o_ref.dtype)


def norm_matmul(x, g, w, layer, *, tm=ROW_TILE, tn=COL_TILE, out_dtype=BF16):
    m, k = x.shape
    n = w.shape[2]
    tn = min(tn, n)
    return pl.pallas_call(
        _norm_matmul_kernel,
        grid=(m // tm, n // tn),
        in_specs=[
            pl.BlockSpec((tm, k), lambda i, j: (i, 0)),
            pl.BlockSpec((1, k), lambda i, j: (0, 0)),
            _layer_spec(layer, (k, tn), lambda i, j: (0, j)),
        ],
        out_specs=pl.BlockSpec((tm, tn), lambda i, j: (i, j)),
        out_shape=jax.ShapeDtypeStruct((m, n), out_dtype),
        scratch_shapes=[pltpu.VMEM((tm, k), BF16)],
        compiler_params=_params("parallel", "arbitrary"),
        name="norm_matmul",
    )(x, g.reshape(1, k), w)


def _proj_res_kernel(a_ref, w_ref, x_ref, g_ref, o_ref):
    f = _dot(a_ref[...], w_ref[...])
    o_ref[...] = x_ref[...] + _rms(f, g_ref[...])


def proj_res(a, w, layer, x, g, *, tm=ROW_TILE):
    m, k = a.shape
    d = w.shape[2]
    return pl.pallas_call(
        _proj_res_kernel,
        grid=(m // tm,),
        in_specs=[
            pl.BlockSpec((tm, k), lambda i: (i, 0)),
            _layer_spec(layer, (k, d), lambda i: (0, 0)),
            pl.BlockSpec((tm, d), lambda i: (i, 0)),
            pl.BlockSpec((1, d), lambda i: (0, 0)),
        ],
        out_specs=pl.BlockSpec((tm, d), lambda i: (i, 0)),
        out_shape=jax.ShapeDtypeStruct((m, d), F32),
        compiler_params=_params("parallel"),
        name="proj_res",
    )(a, w, x, g.reshape(1, d))


def _lag_kernel_body(a_ref, b_ref, o_ref):
    for g in range(a_ref.shape[0]):
        o_ref[g] = jnp.dot(a_ref[g], b_ref[g], precision=lax.Precision.HIGHEST,
                           preferred_element_type=F32)


def _lag_kernels(ca_cat, bb_cat, *, gb=8):
    n_g, lh, p2 = ca_cat.shape
    n_h = bb_cat.shape[-1]
    return pl.pallas_call(
        _lag_kernel_body,
        grid=(n_g // gb,),
        in_specs=[
            pl.BlockSpec((gb, lh, p2), lambda i: (i, 0, 0)),
            pl.BlockSpec((gb, p2, n_h), lambda i: (i, 0, 0)),
        ],
        out_specs=pl.BlockSpec((gb, lh, n_h), lambda i: (i, 0, 0)),
        out_shape=jax.ShapeDtypeStruct((n_g, lh, n_h), F32),
        compiler_params=_params("parallel"),
        name="s5_lag_kernels",
    )(ca_cat, bb_cat)


def _s5_operators(lam_re, lam_im, log_dt, b_re, b_im, c_re, c_im, d_skip, bsz):
    n_g, n_p = lam_re.shape
    n_h = S5_GROUP
    n_l = S5_CHUNK
    lh = n_l * n_h
    lr = lam_re.astype(F32)
    li = lam_im.astype(F32)
    dt = jnp.exp(log_dt.astype(F32))[:, None]
    mag = jnp.exp(lr * dt)
    ang = li * dt
    ab_re = mag * jnp.cos(ang)
    ab_im = mag * jnp.sin(ang)
    den = lr * lr + li * li
    nr = ab_re - 1.0
    f_re = (nr * lr + ab_im * li) / den
    f_im = (ab_im * lr - nr * li) / den
    br = b_re.astype(F32)
    bi = b_im.astype(F32)
    bb_re = f_re[..., None] * br - f_im[..., None] * bi
    bb_im = f_re[..., None] * bi + f_im[..., None] * br
    tau = jnp.arange(n_l + 1, dtype=F32)
    pw_re = jnp.exp((lr * dt)[..., None] * tau) * jnp.cos(ang[..., None] * tau)
    pw_im = jnp.exp((lr * dt)[..., None] * tau) * jnp.sin(ang[..., None] * tau)
    tau_m = tau[None, :, None]
    pt_re = jnp.exp((lr * dt)[:, None, :] * tau_m) * jnp.cos(ang[:, None, :] * tau_m)
    pt_im = jnp.exp((lr * dt)[:, None, :] * tau_m) * jnp.sin(ang[:, None, :] * tau_m)
    cr = c_re.astype(F32)[:, None]
    ci = c_im.astype(F32)[:, None]
    ca_re = cr * pt_re[:, :, None, :] - ci * pt_im[:, :, None, :]
    ca_im = cr * pt_im[:, :, None, :] + ci * pt_re[:, :, None, :]
    cr_o = c_re.astype(F32)[:, :, None, :]
    ci_o = c_im.astype(F32)[:, :, None, :]
    pl_re = pt_re[:, None, :n_l, :]
    pl_im = pt_im[:, None, :n_l, :]
    ca_cat = jnp.concatenate([cr_o * pl_re - ci_o * pl_im, -(cr_o * pl_im + ci_o * pl_re)], axis=-1)
    bb_cat = jnp.concatenate([bb_re, bb_im], axis=1)
    kern = _lag_kernels(ca_cat.reshape(n_g, lh, 2 * n_p), bb_cat).reshape(n_g, n_h, n_l, n_h)
    krev = kern[:, :, ::-1, :].reshape(n_g, n_h, lh)
    rows = []
    for t in range(n_l):
        sh = (n_l - 1 - t) * n_h
        rows.append(jnp.pad(krev[:, :, sh:], ((0, 0), (0, 0), (0, sh))))
    t_mat = jnp.stack(rows, axis=1).reshape(n_g, lh, lh)
    skip = jnp.tile(d_skip.astype(F32).reshape(n_g, n_h), (1, n_l))
    t_mat = t_mat + jnp.eye(lh, dtype=F32)[None] * skip[:, :, None]
    rev_re = pw_re[:, :, n_l - 1::-1]
    rev_im = pw_im[:, :, n_l - 1::-1]
    w_re = (rev_re[..., None] * bb_re[:, :, None, :] - rev_im[..., None] * bb_im[:, :, None, :])
    w_im = (rev_re[..., None] * bb_im[:, :, None, :] + rev_im[..., None] * bb_re[:, :, None, :])
    def slot_rows(w):
        w = w.reshape(n_g // 2, 2, n_p, lh)
        z = jnp.zeros_like(w[:, 0])
        return jnp.stack([jnp.concatenate([w[:, 0], z], 1), jnp.concatenate([z, w[:, 1]], 1)], 1)
    w_pack = jnp.stack([slot_rows(w_re), slot_rows(w_im)], axis=2)
    def slot_cols(v):
        v = v.reshape(n_g // 2, 2, lh, n_p)
        z = jnp.zeros_like(v[:, 0])
        return jnp.stack([jnp.concatenate([v[:, 0], z], -1), jnp.concatenate([z, v[:, 1]], -1)], 1)
    v_pack = jnp.stack([slot_cols(ca_re[:, 1:]), slot_cols(-ca_im[:, 1:])], axis=2)
    def state_cols(a):
        a = jnp.broadcast_to(a.reshape(n_g // 2, 1, 2 * n_p), (n_g // 2, bsz, 2 * n_p))
        return a.reshape(1, n_g * bsz * n_p)
    a_re = state_cols(pw_re[:, :, n_l])
    a_im = state_cols(pw_im[:, :, n_l])
    return t_mat.astype(BF16), w_pack.astype(BF16), v_pack.astype(BF16), a_re, a_im


def _chunk_view(x, bsz, seq):
    d = x.shape[1]
    return x.reshape(bsz, seq // S5_CHUNK, S5_CHUNK * d)


def _norm_matmul_t_kernel(x_ref, g_ref, wt_ref, o_ref, h_ref):
    nc = x_ref.shape[0]
    d = g_ref.shape[1]
    for k in range(S5_POS_PER_STEP):
        h_ref[k * nc:(k + 1) * nc, :] = _rms(x_ref[:, k * d:(k + 1) * d], g_ref[...]).astype(BF16)
    o_ref[...] = _dot_nt(wt_ref[...], h_ref[...]).astype(o_ref.dtype)


def norm_matmul_t(x, g, wt, layer, *, bsz, seq):
    d = x.shape[1]
    n = wt.shape[1]
    nc = seq // S5_CHUNK
    sq = S5_POS_PER_STEP
    steps = S5_CHUNK // sq
    return pl.pallas_call(
        _norm_matmul_t_kernel,
        grid=(bsz, steps),
        in_specs=[
            pl.BlockSpec((None, nc, sq * d), lambda b, s: (b, 0, s)),
            pl.BlockSpec((1, d), lambda b, s: (0, 0)),
            _layer_spec(layer, (n, d), lambda b, s: (0, 0)),
        ],
        out_specs=pl.BlockSpec((n, sq * nc), lambda b, s: (0, b * steps + s)),
        out_shape=jax.ShapeDtypeStruct((n, bsz * seq), BF16),
        scratch_shapes=[pltpu.VMEM((sq * nc, d), BF16)],
        compiler_params=_params("parallel", "parallel"),
        name="norm_matmul_t",
    )(_chunk_view(x, bsz, seq), g.reshape(1, d), wt)


def _chunk_operand(u_ref, k, nc):
    h = S5_GROUP
    return jnp.concatenate(
        [u_ref[k * h:(k + 1) * h, s * nc:(s + 1) * nc] for s in range(S5_CHUNK)], axis=0)


def _s5_inject_kernel(u_ref, w_ref, ore_ref, oim_ref, *, bsz, seq):
    nc = ore_ref.shape[0]
    p2 = w_ref.shape[3]
    for b in range(bsz):
        ub = u_ref.at[:, b * seq:(b + 1) * seq]
        u0 = _chunk_operand(ub, 0, nc)
        u1 = _chunk_operand(ub, 1, nc)
        cols = slice(b * p2, (b + 1) * p2)
        ore_ref[:, cols] = (_dot(w_ref[0, 0, 0], u0) + _dot(w_ref[0, 1, 0], u1)).T
        oim_ref[:, cols] = (_dot(w_ref[0, 0, 1], u0) + _dot(w_ref[0, 1, 1], u1)).T


def s5_inject(ut, w_pack, layer, *, bsz, seq):
    n_pair = w_pack.shape[1]
    p2 = w_pack.shape[4]
    nc = seq // S5_CHUNK
    out = jax.ShapeDtypeStruct((nc, n_pair * bsz * p2), F32)
    oblk = pl.BlockSpec((nc, bsz * p2), lambda j: (0, j))
    return pl.pallas_call(
        functools.partial(_s5_inject_kernel, bsz=bsz, seq=seq),
        grid=(n_pair,),
        in_specs=[
            pl.BlockSpec((2 * S5_GROUP, bsz * seq), lambda j: (j, 0)),
            _layer_spec(layer, (1,) + w_pack.shape[2:], lambda j: (j, 0, 0, 0, 0)),
        ],
        out_specs=[oblk, oblk],
        out_shape=[out, out],
        compiler_params=_params("parallel"),
        name="s5_inject",
    )(ut, w_pack)


def _s5_scan_kernel(wre_ref, wim_ref, are_ref, aim_ref, xre_ref, xim_ref, *, n_c):
    a_re = are_ref[...]
    a_im = aim_ref[...]

    def body(c, carry):
        s_re, s_im = carry
        row = pl.ds(c, 1)
        xre_ref[row, :] = s_re
        xim_ref[row, :] = s_im
        n_re = a_re * s_re - a_im * s_im + wre_ref[row, :]
        n_im = a_re * s_im + a_im * s_re + wim_ref[row, :]
        return n_re, n_im

    zero = jnp.zeros(a_re.shape, F32)
    lax.fori_loop(0, n_c, body, (zero, zero))


def s5_scan(w_re, w_im, a_re, a_im, layer, *, tl=2048):
    n_c, cols = w_re.shape
    tl = min(tl, cols)
    out = jax.ShapeDtypeStruct((n_c, cols), F32)
    blk = pl.BlockSpec((n_c, tl), lambda j: (0, j))
    coef = _layer_spec(layer, (1, tl), lambda j: (0, j))
    return pl.pallas_call(
        functools.partial(_s5_scan_kernel, n_c=n_c),
        grid=(cols // tl,),
        in_specs=[blk, blk, coef, coef],
        out_specs=[blk, blk],
        out_shape=[out, out],
        compiler_params=_params("parallel"),
        name="s5_scan",
    )(w_re, w_im, a_re, a_im)


def _gelu_tanh(y):
    c = np.float32(np.sqrt(2.0 / np.pi))
    return 0.5 * y * (1.0 + jnp.tanh(c * (y + 0.044715 * (y * y * y))))


def _s5_out_kernel(u_ref, t_ref, xre_ref, xim_ref, v_ref, o_ref, *, bsz, seq):
    nc = xre_ref.shape[0]
    p2 = v_ref.shape[-1]
    h = S5_GROUP
    for b in range(bsz):
        ub = u_ref.at[:, b * seq:(b + 1) * seq]
        x_re = xre_ref[:, b * p2:(b + 1) * p2].astype(BF16)
        x_im = xim_ref[:, b * p2:(b + 1) * p2].astype(BF16)
        for k in range(2):
            y = (_dot(t_ref[k], _chunk_operand(ub, k, nc))
                 + _dot_nt(v_ref[0, k, 0], x_re) + _dot_nt(v_ref[0, k, 1], x_im))
            y = _gelu_tanh(y).astype(o_ref.dtype)
            for t in range(S5_CHUNK):
                lanes = slice(b * seq + t * nc, b * seq + (t + 1) * nc)
                o_ref[k * h:(k + 1) * h, lanes] = y[t * h:(t + 1) * h, :]


def s5_out(ut, t_mat, x_re, x_im, v_pack, layer, *, bsz, seq):
    n_pair = v_pack.shape[1]
    p2 = v_pack.shape[-1]
    nc = seq // S5_CHUNK
    lh = t_mat.shape[-1]
    xblk = pl.BlockSpec((nc, bsz * p2), lambda j: (0, j))
    ublk = pl.BlockSpec((2 * S5_GROUP, bsz * seq), lambda j: (j, 0))
    return pl.pallas_call(
        functools.partial(_s5_out_kernel, bsz=bsz, seq=seq),
        grid=(n_pair,),
        in_specs=[
            ublk,
            _layer_spec(layer, (2, lh, lh), lambda j: (j, 0, 0)),
            xblk,
            xblk,
            _layer_spec(layer, (1,) + v_pack.shape[2:], lambda j: (j, 0, 0, 0, 0)),
        ],
        out_specs=ublk,
        out_shape=jax.ShapeDtypeStruct(ut.shape, BF16),
        compiler_params=_params("parallel"),
        name="s5_out",
    )(ut, t_mat, x_re, x_im, v_pack)


def s5_mixer(ut, bsz, seq, ops, layer):
    t_mat, w_pack, v_pack, a_re, a_im = ops
    w_re, w_im = s5_inject(ut, w_pack, layer, bsz=bsz, seq=seq)
    x_re, x_im = s5_scan(w_re, w_im, a_re, a_im, layer)
    return s5_out(ut, t_mat, x_re, x_im, v_pack, layer, bsz=bsz, seq=seq)


def _glu_res_kernel(yt_ref, wv_ref, wg_ref, x_ref, g_ref, o_ref, y_ref, m_ref, *, nj, tn):
    j = pl.program_id(2)

    @pl.when(j == 0)
    def _():
        y_ref[...] = yt_ref[...].astype(F32).T.astype(BF16)

    y = y_ref[...]
    m_ref[j] = _dot(y, wv_ref[...]) * _sigmoid(_dot(y, wg_ref[...]))

    @pl.when(j == nj - 1)
    def _():
        nc = x_ref.shape[0]
        d = g_ref.shape[1]
        ss = None
        for jj in range(nj):
            mj = m_ref[jj]
            s = jnp.sum(mj * mj, axis=-1, keepdims=True)
            ss = s if ss is None else ss + s
        inv = lax.rsqrt(ss / d + EPS)
        for k in range(S5_POS_PER_STEP):
            rows = slice(k * nc, (k + 1) * nc)
            for jj in range(nj):
                cols = slice(k * d + jj * tn, k * d + (jj + 1) * tn)
                o_ref[:, cols] = (x_ref[:, cols]
                                  + m_ref[jj, rows, :] * inv[rows] * g_ref[:, jj * tn:(jj + 1) * tn])


def glu_res(yt, w_glu, layer, x, g, *, bsz, seq, tn=GLU_COL_TILE):
    k, _ = yt.shape
    d = w_glu.shape[2] // 2
    tn = min(tn, d)
    nj = d // tn
    nc = seq // S5_CHUNK
    sq = S5_POS_PER_STEP
    steps = S5_CHUNK // sq
    xblk = pl.BlockSpec((None, nc, sq * d), lambda b, s, j: (b, 0, s))
    out = pl.pallas_call(
        functools.partial(_glu_res_kernel, nj=nj, tn=tn),
        grid=(bsz, steps, nj),
        in_specs=[
            pl.BlockSpec((k, sq * nc), lambda b, s, j: (0, b * steps + s)),
            _layer_spec(layer, (k, tn), lambda b, s, j: (0, j)),
            _layer_spec(layer, (k, tn), lambda b, s, j: (0, j + nj)),
            xblk,
            pl.BlockSpec((1, d), lambda b, s, j: (0, 0)),
        ],
        out_specs=xblk,
        out_shape=jax.ShapeDtypeStruct((bsz, nc, S5_CHUNK * d), F32),
        scratch_shapes=[pltpu.VMEM((sq * nc, k), BF16), pltpu.VMEM((nj, sq * nc, tn), F32)],
        compiler_params=_params("parallel", "parallel", "arbitrary"),
        name="glu_res",
    )(yt, w_glu, w_glu, _chunk_view(x, bsz, seq), g.reshape(1, d))
    return out.reshape(bsz * seq, d)


def _band_bias(rel_bias):
    c = ATT_CHUNK
    nq = ATT_QGROUP * c
    nk = (ATT_QGROUP + ATT_LEFT) * c
    r = rel_bias.astype(F32)
    n_h = r.shape[0]
    d_lo = ATT_LEFT * c - (nk - 1)
    d_hi = ATT_LEFT * c + nq - 1
    n_lo = -(c - 1) - d_lo
    n_hi = d_hi - ATT_MAX_REL
    e = jnp.concatenate([jnp.broadcast_to(r[:, :1], (n_h, n_lo)), r,
                         jnp.broadcast_to(r[:, -1:], (n_h, n_hi))], axis=1)
    width = nq + nk - 1
    assert e.shape[1] == width
    frev = jnp.concatenate([e[:, ::-1], jnp.zeros((n_h, 1), F32)], axis=1)
    skew = jnp.tile(frev, (1, nq))[:, :nq * width].reshape(n_h, nq, width)
    bias = skew[:, :, nq - 1:nq - 1 + nk]
    qa = np.arange(nq)[:, None] // c
    kc = np.arange(nk)[None, :] // c
    valid = (kc >= qa) & (kc <= qa + ATT_LEFT)
    return jnp.where(jnp.asarray(valid)[None], bias, NEG_INF)


def _band_attn_kernel(q_ref, k_ref, v_ref, b_ref, o_ref, s_ref, p_ref, v1_ref, *, seq):
    c = ATT_CHUNK
    dh = ATT_HEAD_DIM
    nq = ATT_QGROUP * c
    nk = (ATT_QGROUP + ATT_LEFT) * c
    v1_ref[:, :dh] = v_ref[...]
    v1_ref[:, dh:] = jnp.ones((seq, dh), BF16)
    for gi in range(seq // nq):
        q0 = gi * nq
        off = max(0, ATT_LEFT * c - q0)
        k0 = q0 - ATT_LEFT * c + off
        w = nk - off
        s_ref[:, :w] = _dot_nt(q_ref[q0:q0 + nq, :], k_ref[k0:k0 + w, :])
        for a in range(ATT_QGROUP):
            rows = slice(a * c, (a + 1) * c)
            lo = max((a * c) // LANES * LANES, off)
            hi = min(-(-((a + ATT_LEFT + 1) * c) // LANES) * LANES, nk)
            s = s_ref[rows, lo - off:hi - off] + b_ref[0, rows, lo:hi]
            m = jnp.max(s, axis=-1, keepdims=True)
            p_ref[rows, lo - off:hi - off] = jnp.exp(s - m).astype(BF16)
            if lo > off:
                p_ref[rows, 0:lo - off] = jnp.zeros((c, lo - off), BF16)
            if hi < nk:
                p_ref[rows, hi - off:w] = jnp.zeros((c, nk - hi), BF16)
        o2 = _dot(p_ref[:, :w], v1_ref[k0:k0 + w, :])
        o_ref[q0:q0 + nq, :] = (o2[:, :dh] / o2[:, dh:]).astype(o_ref.dtype)


def band_attention(q, kv, bias, layer, *, bsz, seq):
    m, d = q.shape
    n_h = d // ATT_HEAD_DIM
    dh = ATT_HEAD_DIM
    nq, nk = bias.shape[2:]
    return pl.pallas_call(
        functools.partial(_band_attn_kernel, seq=seq),
        grid=(bsz, n_h),
        in_specs=[
            pl.BlockSpec((seq, dh), lambda b, h: (b, h)),
            pl.BlockSpec((seq, dh), lambda b, h: (b, h)),
            pl.BlockSpec((seq, dh), lambda b, h: (b, h + n_h)),
            _layer_spec(layer, (1, nq, nk), lambda b, h: (h, 0, 0)),
        ],
        out_specs=pl.BlockSpec((seq, dh), lambda b, h: (b, h)),
        out_shape=jax.ShapeDtypeStruct((m, d), BF16),
        scratch_shapes=[
            pltpu.VMEM((nq, nk), F32),
            pltpu.VMEM((nq, nk), BF16),
            pltpu.VMEM((seq, 2 * dh), BF16),
        ],
        compiler_params=_params("parallel", "parallel"),
        name="band_attention",
    )(q, kv, kv, bias)


def _mem_attn_kernel(x_ref, g1_ref, wq_ref, k_ref, v_ref, wo_ref, g2_ref, o_ref):
    x = x_ref[...]
    h = _rms(x, g1_ref[...]).astype(BF16)
    q = _dot(h, wq_ref[...]).astype(BF16)
    k = k_ref[...]
    v = v_ref[...]
    dh = MEM_HEAD_DIM
    heads = []
    for hd in range(q.shape[1] // dh):
        sl = slice(hd * dh, (hd + 1) * dh)
        s = _dot_nt(q[:, sl], k[:, sl])
        m = jnp.max(s, axis=-1, keepdims=True)
        p = jnp.exp(s - m)
        l = jnp.sum(p, axis=-1, keepdims=True)
        heads.append((_dot(p.astype(BF16), v[:, sl]) / l).astype(BF16))
    o = jnp.concatenate(heads, axis=-1)
    c = _dot(o, wo_ref[...])
    o_ref[...] = x + _rms(c, g2_ref[...])


def mem_attention(x, g1, wq, kvmem, layer, wo, g2, *, seq, n_mem, tm=ROW_TILE):
    m, d = x.shape
    md = wq.shape[2]
    tiles_per_batch = seq // tm
    return pl.pallas_call(
        _mem_attn_kernel,
        grid=(m // tm,),
        in_specs=[
            pl.BlockSpec((tm, d), lambda i: (i, 0)),
            pl.BlockSpec((1, d), lambda i: (0, 0)),
            _layer_spec(layer, (d, md), lambda i: (0, 0)),
            pl.BlockSpec((n_mem, md), lambda i: (i // tiles_per_batch, 2 * layer)),
            pl.BlockSpec((n_mem, md), lambda i: (i // tiles_per_batch, 2 * layer + 1)),
            _layer_spec(layer, (md, d), lambda i: (0, 0)),
            pl.BlockSpec((1, d), lambda i: (0, 0)),
        ],
        out_specs=pl.BlockSpec((tm, d), lambda i: (i, 0)),
        out_shape=jax.ShapeDtypeStruct((m, d), F32),
        compiler_params=_params("parallel"),
        name="mem_attention",
    )(x, g1.reshape(1, d), wq, kvmem, kvmem, wo, g2.reshape(1, d))


def _conv_ffn_kernel(x_ref, g1_ref, wuv_ref, wug_ref, cwv_ref, cwg_ref, cbv_ref, cbg_ref, wd_ref,
                     g2_ref, o_ref, h_ref, acc_ref, tail_ref, up_ref, *, nj, tiles_per_batch):
    i = pl.program_id(0)
    j = pl.program_id(1)
    tm = x_ref.shape[0]
    hm = tm // FFN_ROW_SPLIT

    @pl.when(j == 0)
    def _():
        h_ref[...] = _rms(x_ref[...], g1_ref[...]).astype(h_ref.dtype)
        acc_ref[...] = jnp.zeros(acc_ref.shape, F32)

    @pl.when(i % tiles_per_batch == 0)
    def _():
        tail_ref[j] = jnp.zeros(tail_ref.shape[1:], F32)

    up_ref[:, 0:SUBLANES, :] = tail_ref[j]

    for r in range(FFN_ROW_SPLIT):
        h = h_ref[r * hm:(r + 1) * hm, :]
        rows = slice(SUBLANES + r * hm, SUBLANES + (r + 1) * hm)
        up_ref[0, rows, :] = _dot(h, wuv_ref[...])
        up_ref[1, rows, :] = _dot(h, wug_ref[...])

    tail_ref[j] = up_ref[:, tm:tm + SUBLANES, :]

    def conv(slot, base, cw, cb):
        taps = [up_ref[slot, base - (CONV_W - 1 - k):base - (CONV_W - 1 - k) + hm, :]
                for k in range(CONV_W)]
        return cb + taps[0] * cw[0:1, :] + taps[1] * cw[1:2, :] + taps[2] * cw[2:3, :]

    for r in range(FFN_ROW_SPLIT):
        base = SUBLANES + r * hm
        val = conv(0, base, cwv_ref[...], cbv_ref[...])
        gate = conv(1, base, cwg_ref[...], cbg_ref[...])
        act = (val * (gate * _sigmoid(gate))).astype(BF16)
        acc_ref[r * hm:(r + 1) * hm, :] += _dot(act, wd_ref[...])

    @pl.when(j == nj - 1)
    def _():
        o_ref[...] = x_ref[...] + _rms(acc_ref[...], g2_ref[...])


def conv_ffn(x, g1, w_up, conv_w, conv_b, w_down, layer, g2, *, seq, tm=ROW_TILE, tf=FFN_COL_TILE):
    m, d = x.shape
    f = w_down.shape[1]
    nj = f // tf
    tiles_per_batch = seq // tm
    conv_b = conv_b.reshape(conv_b.shape[0], 1, 2 * f)
    return pl.pallas_call(
        functools.partial(_conv_ffn_kernel, nj=nj, tiles_per_batch=tiles_per_batch),
        grid=(m // tm, nj),
        in_specs=[
            pl.BlockSpec((tm, d), lambda i, j: (i, 0)),
            pl.BlockSpec((1, d), lambda i, j: (0, 0)),
            _layer_spec(layer, (d, tf), lambda i, j: (0, j)),
            _layer_spec(layer, (d, tf), lambda i, j: (0, j + nj)),
            _layer_spec(layer, (CONV_W, tf), lambda i, j: (0, j)),
            _layer_spec(layer, (CONV_W, tf), lambda i, j: (0, j + nj)),
            _layer_spec(layer, (1, tf), lambda i, j: (0, j)),
            _layer_spec(layer, (1, tf), lambda i, j: (0, j + nj)),
            _layer_spec(layer, (tf, d), lambda i, j: (j, 0)),
            pl.BlockSpec((1, d), lambda i, j: (0, 0)),
        ],
        out_specs=pl.BlockSpec((tm, d), lambda i, j: (i, 0)),
        out_shape=jax.ShapeDtypeStruct((m, d), F32),
        scratch_shapes=[
            pltpu.VMEM((tm, d), BF16),
            pltpu.VMEM((tm, d), F32),
            pltpu.VMEM((nj, 2, SUBLANES, tf), F32),
            pltpu.VMEM((2, SUBLANES + tm, tf), F32),
        ],
        compiler_params=_params("arbitrary", "arbitrary"),
        name="conv_ffn",
    )(x, g1.reshape(1, d), w_up, w_up, conv_w, conv_w, conv_b, conv_b, w_down, g2.reshape(1, d))


def kernel(x, mem, norm_mix, norm_mem, norm_ffn, mem_in_norm, a_w_in, a_lam_re, a_lam_im, a_log_dt,
           a_b_re, a_b_im, a_c_re, a_c_im, a_d, a_w_glu, kv_norm, w_k, w_v, b_w_q, b_rel_bias, b_w_o,
           m_w_q, m_w_kv, m_w_o, f_w_up, f_conv_w, f_conv_b, f_w_down):
    bsz, seq, d = x.shape
    depth = norm_mix.shape[0]
    n_a = a_w_in.shape[0]
    n_mem = mem.shape[1]
    md = m_w_q.shape[2]
    assert seq % ROW_TILE == 0 and seq % (ATT_QGROUP * ATT_CHUNK) == 0
    assert (seq // S5_CHUNK) % LANES == 0 and (d // S5_GROUP) % 2 == 0

    xf = x.reshape(bsz * seq, d)

    w_kv_mem = jnp.transpose(m_w_kv, (1, 0, 2)).reshape(1, d, depth * 2 * md).astype(BF16)
    w_kv = jnp.concatenate([w_k, w_v], axis=1).astype(BF16)[None]
    a_w_in_t = jnp.transpose(a_w_in, (0, 2, 1)).astype(BF16)
    a_w_glu_b = a_w_glu.astype(BF16)
    b_w_q_b = (b_w_q * (ATT_HEAD_DIM ** -0.5)).astype(BF16)
    b_w_o_b = b_w_o.astype(BF16)
    m_w_q_b = (m_w_q * (MEM_HEAD_DIM ** -0.5)).astype(BF16)
    m_w_o_b = m_w_o.astype(BF16)
    f_w_up_b = f_w_up.astype(BF16)
    f_w_down_b = f_w_down.astype(BF16)
    flat = lambda a: a.reshape((-1,) + a.shape[2:])
    s5_ops = _s5_operators(flat(a_lam_re), flat(a_lam_im), flat(a_log_dt), flat(a_b_re),
                           flat(a_b_im), flat(a_c_re), flat(a_c_im), a_d, bsz)
    s5_ops = [o.reshape((n_a, o.shape[0] // n_a) + o.shape[1:]) for o in s5_ops[:3]] + [
        o.reshape(n_a, 1, -1) for o in s5_ops[3:]]
    att_bias = jax.vmap(_band_bias)(b_rel_bias)

    kvmem = norm_matmul(mem.reshape(bsz * n_mem, d), mem_in_norm, w_kv_mem, 0)

    kv = None
    for l in range(depth):
        if l == n_a:
            kv = norm_matmul(xf, kv_norm, w_kv, 0)
        if l < n_a:
            ut = norm_matmul_t(xf, norm_mix[l, 0], a_w_in_t, l, bsz=bsz, seq=seq)
            yt = s5_mixer(ut, bsz, seq, s5_ops, l)
            xf = glu_res(yt, a_w_glu_b, l, xf, norm_mix[l, 1], bsz=bsz, seq=seq)
        else:
            jb = l - n_a
            q = norm_matmul(xf, norm_mix[l, 0], b_w_q_b, jb)
            o = band_attention(q, kv, att_bias, jb, bsz=bsz, seq=seq)
            xf = proj_res(o, b_w_o_b, jb, xf, norm_mix[l, 1])
        xf = mem_attention(xf, norm_mem[l, 0], m_w_q_b, kvmem, l, m_w_o_b, norm_mem[l, 1],
                           seq=seq, n_mem=n_mem)
        xf = conv_ffn(xf, norm_ffn[l, 0], f_w_up_b, f_conv_w, f_conv_b, f_w_down_b, l,
                      norm_ffn[l, 1], seq=seq)
    return xf.reshape(bsz, seq, d)
```

```python
import functools

import numpy as np
import jax
import jax.numpy as jnp
from jax import lax
from jax.experimental import pallas as pl
from jax.experimental.pallas import tpu as pltpu

EPS = 1e-6
NEG_INF = -1e30
BF16 = jnp.bfloat16
F32 = jnp.float32

S5_GROUP = 16
S5_CHUNK = 16
S5_POS_PER_STEP = 4
ATT_CHUNK = 64
ATT_LEFT = 8
ATT_HEAD_DIM = 128
ATT_MAX_REL = 256
ATT_QGROUP = 4
MEM_HEAD_DIM = 128
CONV_W = 3

VMEM_LIMIT_BYTES = 56 * 1024 * 1024
ROW_TILE = 512
COL_TILE = 2048
FFN_ROW_SPLIT = 2
FFN_COL_TILE = 512
GLU_COL_TILE = 512
SUBLANES = 8
LANES = 128

_NT = (((1,), (1,)), ((), ()))


def _params(*sem):
    return pltpu.CompilerParams(dimension_semantics=sem, vmem_limit_bytes=VMEM_LIMIT_BYTES)


def _rms(x, g):
    ms = jnp.mean(x * x, axis=-1, keepdims=True)
    return x * lax.rsqrt(ms + EPS) * g


def _sigmoid(x):
    return 1.0 / (1.0 + jnp.exp(-x))


def _dot(a, b):
    return jnp.dot(a, b, preferred_element_type=F32)


def _dot_nt(a, b):
    return lax.dot_general(a, b, _NT, preferred_element_type=F32)


def _layer_spec(layer, block, index_map):
    return pl.BlockSpec((None,) + tuple(block), lambda *ids: (layer,) + tuple(index_map(*ids)))


def _norm_matmul_kernel(x_ref, g_ref, w_ref, o_ref, h_ref):
    @pl.when(pl.program_id(1) == 0)
    def _():
        h_ref[...] = _rms(x_ref[...], g_ref[...]).astype(h_ref.dtype)

    o_ref[...] = _dot(h_ref[...], w_ref[...]).astype(o_ref.dtype)


def norm_matmul(x, g, w, layer, *, tm=ROW_TILE, tn=COL_TILE, out_dtype=BF16):
    m, k = x.shape
    n = w.shape[2]
    tn = min(tn, n)
    return pl.pallas_call(
        _norm_matmul_kernel,
        grid=(m // tm, n // tn),
        in_specs=[
            pl.BlockSpec((tm, k), lambda i, j: (i, 0)),
            pl.BlockSpec((1, k), lambda i, j: (0, 0)),
            _layer_spec(layer, (k, tn), lambda i, j: (0, j)),
        ],
        out_specs=pl.BlockSpec((tm, tn), lambda i, j: (i, j)),
        out_shape=jax.ShapeDtypeStruct((m, n), out_dtype),
        scratch_shapes=[pltpu.VMEM((tm, k), BF16)],
        compiler_params=_params("parallel", "arbitrary"),
        name="norm_matmul",
    )(x, g.reshape(1, k), w)


def _proj_res_kernel(a_ref, w_ref, x_ref, g_ref, o_ref):
    f = _dot(a_ref[...], w_ref[...])
    o_ref[...] = x_ref[...] + _rms(f, g_ref[...])


def proj_res(a, w, layer, x, g, *, tm=ROW_TILE):
    m, k = a.shape
    d = w.shape[2]
    return pl.pallas_call(
        _proj_res_kernel,
        grid=(m // tm,),
        in_specs=[
            pl.BlockSpec((tm, k), lambda i: (i, 0)),
            _layer_spec(layer, (k, d), lambda i: (0, 0)),
            pl.BlockSpec((tm, d), lambda i: (i, 0)),
            pl.BlockSpec((1, d), lambda i: (0, 0)),
        ],
        out_specs=pl.BlockSpec((tm, d), lambda i: (i, 0)),
        out_shape=jax.ShapeDtypeStruct((m, d), F32),
        compiler_params=_params("parallel"),
        name="proj_res",
    )(a, w, x, g.reshape(1, d))


def _s5_build_kernel(rev_ref, bb_ref, cc_ref, cpad_ref, pt_ref, d_ref, t_ref, w_ref, v_ref):
    n_l = S5_CHUNK
    n_h = S5_GROUP
    lh = n_l * n_h
    n_p = rev_ref.shape[2]

    def exact_dot(a, b):
        return jnp.dot(a, b, precision=lax.Precision.HIGHEST, preferred_element_type=F32)

    lane = lax.broadcasted_iota(jnp.int32, (n_l, lh), 1)
    row = lax.broadcasted_iota(jnp.int32, (n_l, lh), 0)
    per_step = (lane // n_h == row).astype(F32)
    per_chan = (lane % n_h == row).astype(F32)
    sub = lax.broadcasted_iota(jnp.int32, (n_h, lh), 0)
    lan = lax.broadcasted_iota(jnp.int32, (n_h, lh), 1)
    no_state = jnp.zeros((n_p, lh), BF16)
    for k in range(2):
        r_re = exact_dot(rev_ref[k, 0], per_step)
        r_im = exact_dot(rev_ref[k, 1], per_step)
        b_re = exact_dot(bb_ref[k, 0], per_chan)
        b_im = exact_dot(bb_ref[k, 1], per_chan)
        w_re = r_re * b_re - r_im * b_im
        w_im = r_re * b_im + r_im * b_re
        for ri, w in enumerate((w_re, w_im)):
            for slot in range(2):
                w_ref[0, k, ri, slot * n_p:(slot + 1) * n_p, :] = (
                    w.astype(BF16) if slot == k else no_state)
        krev = exact_dot(cc_ref[k, 0], w_re) - exact_dot(cc_ref[k, 1], w_im)
        skip = d_ref[k]
        for t in range(n_l):
            sh = (n_l - 1 - t) * n_h
            slab = krev
            if sh:
                slab = jnp.where(lan < lh - sh, pltpu.roll(krev, lh - sh, axis=1), 0.0)
            slab = slab + jnp.where(lan == t * n_h + sub, skip, 0.0)
            t_ref[k, t * n_h:(t + 1) * n_h, :] = slab.astype(BF16)
        c_re = cpad_ref[k, 0]
        c_im = cpad_ref[k, 1]
        for t in range(n_l):
            p_re = pt_ref[k, 0, t + 1:t + 2, :]
            p_im = pt_ref[k, 1, t + 1:t + 2, :]
            v_ref[0, k, 0, t * n_h:(t + 1) * n_h, :] = (c_re * p_re - c_im * p_im).astype(BF16)
            v_ref[0, k, 1, t * n_h:(t + 1) * n_h, :] = (-(c_re * p_im + c_im * p_re)).astype(BF16)


def _s5_operators(lam_re, lam_im, log_dt, b_re, b_im, c_re, c_im, d_skip, bsz):
    n_g, n_p = lam_re.shape
    n_h = S5_GROUP
    n_l = S5_CHUNK
    lh = n_l * n_h
    lr = lam_re.astype(F32)
    li = lam_im.astype(F32)
    dt = jnp.exp(log_dt.astype(F32))[:, None]
    mag = jnp.exp(lr * dt)
    ang = li * dt
    ab_re = mag * jnp.cos(ang)
    ab_im = mag * jnp.sin(ang)
    den = lr * lr + li * li
    nr = ab_re - 1.0
    f_re = (nr * lr + ab_im * li) / den
    f_im = (ab_im * lr - nr * li) / den
    br = b_re.astype(F32)
    bi = b_im.astype(F32)
    bb = jnp.stack([f_re[..., None] * br - f_im[..., None] * bi,
                    f_re[..., None] * bi + f_im[..., None] * br], axis=1)
    tau = jnp.arange(n_l + 1, dtype=F32)
    pw_re = jnp.exp((lr * dt)[..., None] * tau) * jnp.cos(ang[..., None] * tau)
    pw_im = jnp.exp((lr * dt)[..., None] * tau) * jnp.sin(ang[..., None] * tau)
    tau_m = tau[None, :, None]
    pt = jnp.stack([jnp.exp((lr * dt)[:, None, :] * tau_m) * jnp.cos(ang[:, None, :] * tau_m),
                    jnp.exp((lr * dt)[:, None, :] * tau_m) * jnp.sin(ang[:, None, :] * tau_m)],
                   axis=1)
    rev = jnp.stack([pw_re[:, :, n_l - 1::-1], pw_im[:, :, n_l - 1::-1]], axis=1)
    cc = jnp.stack([c_re.astype(F32), c_im.astype(F32)], axis=1)
    slot = (jnp.arange(n_g) % 2)[:, None, None, None]

    def lane_place(a):
        z = jnp.zeros_like(a)
        return jnp.where(slot == 0, jnp.concatenate([a, z], -1), jnp.concatenate([z, a], -1))

    skip = jnp.broadcast_to(d_skip.astype(F32).reshape(n_g, n_h, 1), (n_g, n_h, lh))
    n_pair = n_g // 2
    pair = lambda shape: pl.BlockSpec((2,) + shape, lambda j: (j,) + (0,) * len(shape))
    t_mat, w_pack, v_pack = pl.pallas_call(
        _s5_build_kernel,
        grid=(n_pair,),
        in_specs=[pair((2, n_p, n_l)), pair((2, n_p, n_h)), pair((2, n_h, n_p)),
                  pair((2, n_h, 2 * n_p)), pair((2, n_l + 1, 2 * n_p)), pair((n_h, lh))],
        out_specs=[
            pl.BlockSpec((2, lh, lh), lambda j: (j, 0, 0)),
            pl.BlockSpec((1, 2, 2, 2 * n_p, lh), lambda j: (j, 0, 0, 0, 0)),
            pl.BlockSpec((1, 2, 2, lh, 2 * n_p), lambda j: (j, 0, 0, 0, 0)),
        ],
        out_shape=[
            jax.ShapeDtypeStruct((n_g, lh, lh), BF16),
            jax.ShapeDtypeStruct((n_pair, 2, 2, 2 * n_p, lh), BF16),
            jax.ShapeDtypeStruct((n_pair, 2, 2, lh, 2 * n_p), BF16),
        ],
        compiler_params=_params("parallel"),
        name="s5_build_operators",
    )(rev, bb, cc, lane_place(cc), lane_place(pt), skip)

    def state_cols(a):
        a = jnp.broadcast_to(a.reshape(n_pair, 1, 2 * n_p), (n_pair, bsz, 2 * n_p))
        return a.reshape(1, n_g * bsz * n_p)
    return t_mat, w_pack, v_pack, state_cols(pw_re[:, :, n_l]), state_cols(pw_im[:, :, n_l])


def _chunk_view(x, bsz, seq):
    d = x.shape[1]
    return x.reshape(bsz, seq // S5_CHUNK, S5_CHUNK * d)


def _norm_matmul_t_kernel(x_ref, g_ref, wt_ref, o_ref, h_ref):
    nc = x_ref.shape[0]
    d = g_ref.shape[1]
    for k in range(S5_POS_PER_STEP):
        h_ref[k * nc:(k + 1) * nc, :] = _rms(x_ref[:, k * d:(k + 1) * d], g_ref[...]).astype(BF16)
    o_ref[...] = _dot_nt(wt_ref[...], h_ref[...]).astype(o_ref.dtype)


def norm_matmul_t(x, g, wt, layer, *, bsz, seq):
    d = x.shape[1]
    n = wt.shape[1]
    nc = seq // S5_CHUNK
    sq = S5_POS_PER_STEP
    steps = S5_CHUNK // sq
    return pl.pallas_call(
        _norm_matmul_t_kernel,
        grid=(bsz, steps),
        in_specs=[
            pl.BlockSpec((None, nc, sq * d), lambda b, s: (b, 0, s)),
            pl.BlockSpec((1, d), lambda b, s: (0, 0)),
            _layer_spec(layer, (n, d), lambda b, s: (0, 0)),
        ],
        out_specs=pl.BlockSpec((n, sq * nc), lambda b, s: (0, b * steps + s)),
        out_shape=jax.ShapeDtypeStruct((n, bsz * seq), BF16),
        scratch_shapes=[pltpu.VMEM((sq * nc, d), BF16)],
        compiler_params=_params("parallel", "parallel"),
        name="norm_matmul_t",
    )(_chunk_view(x, bsz, seq), g.reshape(1, d), wt)


def _chunk_operand(u_ref, k, nc):
    h = S5_GROUP
    return jnp.concatenate(
        [u_ref[k * h:(k + 1) * h, s * nc:(s + 1) * nc] for s in range(S5_CHUNK)], axis=0)


def _s5_inject_kernel(u_ref, w_ref, ore_ref, oim_ref, *, bsz, seq):
    nc = ore_ref.shape[0]
    p2 = w_ref.shape[3]
    for b in range(bsz):
        ub = u_ref.at[:, b * seq:(b + 1) * seq]
        u0 = _chunk_operand(ub, 0, nc)
        u1 = _chunk_operand(ub, 1, nc)
        cols = slice(b * p2, (b + 1) * p2)
        ore_ref[:, cols] = (_dot(w_ref[0, 0, 0], u0) + _dot(w_ref[0, 1, 0], u1)).T
        oim_ref[:, cols] = (_dot(w_ref[0, 0, 1], u0) + _dot(w_ref[0, 1, 1], u1)).T


def s5_inject(ut, w_pack, layer, *, bsz, seq):
    n_pair = w_pack.shape[1]
    p2 = w_pack.shape[4]
    nc = seq // S5_CHUNK
    out = jax.ShapeDtypeStruct((nc, n_pair * bsz * p2), F32)
    oblk = pl.BlockSpec((nc, bsz * p2), lambda j: (0, j))
    return pl.pallas_call(
        functools.partial(_s5_inject_kernel, bsz=bsz, seq=seq),
        grid=(n_pair,),
        in_specs=[
            pl.BlockSpec((2 * S5_GROUP, bsz * seq), lambda j: (j, 0)),
            _layer_spec(layer, (1,) + w_pack.shape[2:], lambda j: (j, 0, 0, 0, 0)),
        ],
        out_specs=[oblk, oblk],
        out_shape=[out, out],
        compiler_params=_params("parallel"),
        name="s5_inject",
    )(ut, w_pack)


def _s5_scan_kernel(wre_ref, wim_ref, are_ref, aim_ref, xre_ref, xim_ref, *, n_c):
    a_re = are_ref[...]
    a_im = aim_ref[...]

    def body(c, carry):
        s_re, s_im = carry
        row = pl.ds(c, 1)
        xre_ref[row, :] = s_re
        xim_ref[row, :] = s_im
        n_re = a_re * s_re - a_im * s_im + wre_ref[row, :]
        n_im = a_re * s_im + a_im * s_re + wim_ref[row, :]
        return n_re, n_im

    zero = jnp.zeros(a_re.shape, F32)
    lax.fori_loop(0, n_c, body, (zero, zero))


def s5_scan(w_re, w_im, a_re, a_im, layer, *, tl=2048):
    n_c, cols = w_re.shape
    tl = min(tl, cols)
    out = jax.ShapeDtypeStruct((n_c, cols), F32)
    blk = pl.BlockSpec((n_c, tl), lambda j: (0, j))
    coef = _layer_spec(layer, (1, tl), lambda j: (0, j))
    return pl.pallas_call(
        functools.partial(_s5_scan_kernel, n_c=n_c),
        grid=(cols // tl,),
        in_specs=[blk, blk, coef, coef],
        out_specs=[blk, blk],
        out_shape=[out, out],
        compiler_params=_params("parallel"),
        name="s5_scan",
    )(w_re, w_im, a_re, a_im)


def _gelu_tanh(y):
    c = np.float32(np.sqrt(2.0 / np.pi))
    return 0.5 * y * (1.0 + jnp.tanh(c * (y + 0.044715 * (y * y * y))))


def _s5_out_kernel(u_ref, t_ref, xre_ref, xim_ref, v_ref, o_ref, *, bsz, seq):
    nc = xre_ref.shape[0]
    p2 = v_ref.shape[-1]
    h = S5_GROUP
    for b in range(bsz):
        ub = u_ref.at[:, b * seq:(b + 1) * seq]
        x_re = xre_ref[:, b * p2:(b + 1) * p2].astype(BF16)
        x_im = xim_ref[:, b * p2:(b + 1) * p2].astype(BF16)
        for k in range(2):
            y = (_dot(t_ref[k], _chunk_operand(ub, k, nc))
                 + _dot_nt(v_ref[0, k, 0], x_re) + _dot_nt(v_ref[0, k, 1], x_im))
            y = _gelu_tanh(y).astype(o_ref.dtype)
            for t in range(S5_CHUNK):
                lanes = slice(b * seq + t * nc, b * seq + (t + 1) * nc)
                o_ref[k * h:(k + 1) * h, lanes] = y[t * h:(t + 1) * h, :]


def s5_out(ut, t_mat, x_re, x_im, v_pack, layer, *, bsz, seq):
    n_pair = v_pack.shape[1]
    p2 = v_pack.shape[-1]
    nc = seq // S5_CHUNK
    lh = t_mat.shape[-1]
    xblk = pl.BlockSpec((nc, bsz * p2), lambda j: (0, j))
    ublk = pl.BlockSpec((2 * S5_GROUP, bsz * seq), lambda j: (j, 0))
    return pl.pallas_call(
        functools.partial(_s5_out_kernel, bsz=bsz, seq=seq),
        grid=(n_pair,),
        in_specs=[
            ublk,
            _layer_spec(layer, (2, lh, lh), lambda j: (j, 0, 0)),
            xblk,
            xblk,
            _layer_spec(layer, (1,) + v_pack.shape[2:], lambda j: (j, 0, 0, 0, 0)),
        ],
        out_specs=ublk,
        out_shape=jax.ShapeDtypeStruct(ut.shape, BF16),
        compiler_params=_params("parallel"),
        name="s5_out",
    )(ut, t_mat, x_re, x_im, v_pack)


def s5_mixer(ut, bsz, seq, ops, layer):
    t_mat, w_pack, v_pack, a_re, a_im = ops
    w_re, w_im = s5_inject(ut, w_pack, layer, bsz=bsz, seq=seq)
    x_re, x_im = s5_scan(w_re, w_im, a_re, a_im, layer)
    return s5_out(ut, t_mat, x_re, x_im, v_pack, layer, bsz=bsz, seq=seq)


def _glu_res_kernel(yt_ref, wv_ref, wg_ref, x_ref, g_ref, o_ref, y_ref, m_ref, *, nj, tn):
    j = pl.program_id(2)

    @pl.when(j == 0)
    def _():
        y_ref[...] = yt_ref[...].astype(F32).T.astype(BF16)

    y = y_ref[...]
    m_ref[j] = _dot(y, wv_ref[...]) * _sigmoid(_dot(y, wg_ref[...]))

    @pl.when(j == nj - 1)
    def _():
        nc = x_ref.shape[0]
        d = g_ref.shape[1]
        ss = None
        for jj in range(nj):
            mj = m_ref[jj]
            s = jnp.sum(mj * mj, axis=-1, keepdims=True)
            ss = s if ss is None else ss + s
        inv = lax.rsqrt(ss / d + EPS)
        for k in range(S5_POS_PER_STEP):
            rows = slice(k * nc, (k + 1) * nc)
            for jj in range(nj):
                cols = slice(k * d + jj * tn, k * d + (jj + 1) * tn)
                o_ref[:, cols] = (x_ref[:, cols]
                                  + m_ref[jj, rows, :] * inv[rows] * g_ref[:, jj * tn:(jj + 1) * tn])


def glu_res(yt, w_glu, layer, x, g, *, bsz, seq, tn=GLU_COL_TILE):
    k, _ = yt.shape
    d = w_glu.shape[2] // 2
    tn = min(tn, d)
    nj = d // tn
    nc = seq // S5_CHUNK
    sq = S5_POS_PER_STEP
    steps = S5_CHUNK // sq
    xblk = pl.BlockSpec((None, nc, sq * d), lambda b, s, j: (b, 0, s))
    out = pl.pallas_call(
        functools.partial(_glu_res_kernel, nj=nj, tn=tn),
        grid=(bsz, steps, nj),
        in_specs=[
            pl.BlockSpec((k, sq * nc), lambda b, s, j: (0, b * steps + s)),
            _layer_spec(layer, (k, tn), lambda b, s, j: (0, j)),
            _layer_spec(layer, (k, tn), lambda b, s, j: (0, j + nj)),
            xblk,
            pl.BlockSpec((1, d), lambda b, s, j: (0, 0)),
        ],
        out_specs=xblk,
        out_shape=jax.ShapeDtypeStruct((bsz, nc, S5_CHUNK * d), F32),
        scratch_shapes=[pltpu.VMEM((sq * nc, k), BF16), pltpu.VMEM((nj, sq * nc, tn), F32)],
        compiler_params=_params("parallel", "parallel", "arbitrary"),
        name="glu_res",
    )(yt, w_glu, w_glu, _chunk_view(x, bsz, seq), g.reshape(1, d))
    return out.reshape(bsz * seq, d)


def _band_bias(rel_bias):
    c = ATT_CHUNK
    nq = ATT_QGROUP * c
    nk = (ATT_QGROUP + ATT_LEFT) * c
    r = rel_bias.astype(F32)
    n_h = r.shape[0]
    d_lo = ATT_LEFT * c - (nk - 1)
    d_hi = ATT_LEFT * c + nq - 1
    n_lo = -(c - 1) - d_lo
    n_hi = d_hi - ATT_MAX_REL
    e = jnp.concatenate([jnp.broadcast_to(r[:, :1], (n_h, n_lo)), r,
                         jnp.broadcast_to(r[:, -1:], (n_h, n_hi))], axis=1)
    width = nq + nk - 1
    assert e.shape[1] == width
    frev = jnp.concatenate([e[:, ::-1], jnp.zeros((n_h, 1), F32)], axis=1)
    skew = jnp.tile(frev, (1, nq))[:, :nq * width].reshape(n_h, nq, width)
    bias = skew[:, :, nq - 1:nq - 1 + nk]
    qa = np.arange(nq)[:, None] // c
    kc = np.arange(nk)[None, :] // c
    valid = (kc >= qa) & (kc <= qa + ATT_LEFT)
    return jnp.where(jnp.asarray(valid)[None], bias, NEG_INF)


def _band_attn_kernel(q_ref, k_ref, v_ref, b_ref, o_ref, s_ref, p_ref, v1_ref, *, seq):
    c = ATT_CHUNK
    dh = ATT_HEAD_DIM
    nq = ATT_QGROUP * c
    nk = (ATT_QGROUP + ATT_LEFT) * c
    v1_ref[:, :dh] = v_ref[...]
    v1_ref[:, dh:] = jnp.ones((seq, dh), BF16)
    for gi in range(seq // nq):
        q0 = gi * nq
        off = max(0, ATT_LEFT * c - q0)
        k0 = q0 - ATT_LEFT * c + off
        w = nk - off
        s_ref[:, :w] = _dot_nt(q_ref[q0:q0 + nq, :], k_ref[k0:k0 + w, :])
        for a in range(ATT_QGROUP):
            rows = slice(a * c, (a + 1) * c)
            lo = max((a * c) // LANES * LANES, off)
            hi = min(-(-((a + ATT_LEFT + 1) * c) // LANES) * LANES, nk)
            s = s_ref[rows, lo - off:hi - off] + b_ref[0, rows, lo:hi]
            m = jnp.max(s, axis=-1, keepdims=True)
            p_ref[rows, lo - off:hi - off] = jnp.exp(s - m).astype(BF16)
            if lo > off:
                p_ref[rows, 0:lo - off] = jnp.zeros((c, lo - off), BF16)
            if hi < nk:
                p_ref[rows, hi - off:w] = jnp.zeros((c, nk - hi), BF16)
        o2 = _dot(p_ref[:, :w], v1_ref[k0:k0 + w, :])
        o_ref[q0:q0 + nq, :] = (o2[:, :dh] / o2[:, dh:]).astype(o_ref.dtype)


def band_attention(q, kv, bias, layer, *, bsz, seq):
    m, d = q.shape
    n_h = d // ATT_HEAD_DIM
    dh = ATT_HEAD_DIM
    nq, nk = bias.shape[2:]
    return pl.pallas_call(
        functools.partial(_band_attn_kernel, seq=seq),
        grid=(bsz, n_h),
        in_specs=[
            pl.BlockSpec((seq, dh), lambda b, h: (b, h)),
            pl.BlockSpec((seq, dh), lambda b, h: (b, h)),
            pl.BlockSpec((seq, dh), lambda b, h: (b, h + n_h)),
            _layer_spec(layer, (1, nq, nk), lambda b, h: (h, 0, 0)),
        ],
        out_specs=pl.BlockSpec((seq, dh), lambda b, h: (b, h)),
        out_shape=jax.ShapeDtypeStruct((m, d), BF16),
        scratch_shapes=[
            pltpu.VMEM((nq, nk), F32),
            pltpu.VMEM((nq, nk), BF16),
            pltpu.VMEM((seq, 2 * dh), BF16),
        ],
        compiler_params=_params("parallel", "parallel"),
        name="band_attention",
    )(q, kv, kv, bias)


def _mem_attn_kernel(x_ref, g1_ref, wq_ref, k_ref, v_ref, wo_ref, g2_ref, o_ref):
    x = x_ref[...]
    h = _rms(x, g1_ref[...]).astype(BF16)
    q = _dot(h, wq_ref[...]).astype(BF16)
    k = k_ref[...]
    v = v_ref[...]
    dh = MEM_HEAD_DIM
    n_heads = q.shape[1] // dh
    scores = [_dot_nt(q[:, hd * dh:(hd + 1) * dh], k[:, hd * dh:(hd + 1) * dh])
              for hd in range(n_heads)]
    ones = jnp.ones((v.shape[0], dh), BF16)
    heads = []
    for hd in range(n_heads):
        s = scores[hd]
        p = jnp.exp(s - jnp.max(s, axis=-1, keepdims=True)).astype(BF16)
        o2 = _dot(p, jnp.concatenate([v[:, hd * dh:(hd + 1) * dh], ones], axis=-1))
        heads.append((o2[:, :dh] / o2[:, dh:]).astype(BF16))
    o = jnp.concatenate(heads, axis=-1)
    c = _dot(o, wo_ref[...])
    o_ref[...] = x + _rms(c, g2_ref[...])


def mem_attention(x, g1, wq, kvmem, layer, wo, g2, *, seq, n_mem, tm=ROW_TILE):
    m, d = x.shape
    md = wq.shape[2]
    tiles_per_batch = seq // tm
    return pl.pallas_call(
        _mem_attn_kernel,
        grid=(m // tm,),
        in_specs=[
            pl.BlockSpec((tm, d), lambda i: (i, 0)),
            pl.BlockSpec((1, d), lambda i: (0, 0)),
            _layer_spec(layer, (d, md), lambda i: (0, 0)),
            pl.BlockSpec((n_mem, md), lambda i: (i // tiles_per_batch, 2 * layer)),
            pl.BlockSpec((n_mem, md), lambda i: (i // tiles_per_batch, 2 * layer + 1)),
            _layer_spec(layer, (md, d), lambda i: (0, 0)),
            pl.BlockSpec((1, d), lambda i: (0, 0)),
        ],
        out_specs=pl.BlockSpec((tm, d), lambda i: (i, 0)),
        out_shape=jax.ShapeDtypeStruct((m, d), F32),
        compiler_params=_params("parallel"),
        name="mem_attention",
    )(x, g1.reshape(1, d), wq, kvmem, kvmem, wo, g2.reshape(1, d))


def _conv_ffn_kernel(x_ref, g1_ref, wuv_ref, wug_ref, cwv_ref, cwg_ref, cbv_ref, cbg_ref, wd_ref,
                     g2_ref, o_ref, h_ref, acc_ref, tail_ref, up_ref, *, nj, tiles_per_batch):
    i = pl.program_id(0)
    j = pl.program_id(1)
    tm = x_ref.shape[0]
    hm = tm // FFN_ROW_SPLIT

    @pl.when(j == 0)
    def _():
        h_ref[...] = _rms(x_ref[...], g1_ref[...]).astype(h_ref.dtype)
        acc_ref[...] = jnp.zeros(acc_ref.shape, F32)

    @pl.when(i % tiles_per_batch == 0)
    def _():
        tail_ref[j] = jnp.zeros(tail_ref.shape[1:], F32)

    up_ref[:, 0:SUBLANES, :] = tail_ref[j]

    for r in range(FFN_ROW_SPLIT):
        h = h_ref[r * hm:(r + 1) * hm, :]
        rows = slice(SUBLANES + r * hm, SUBLANES + (r + 1) * hm)
        up_ref[0, rows, :] = _dot(h, wuv_ref[...])
        up_ref[1, rows, :] = _dot(h, wug_ref[...])

    tail_ref[j] = up_ref[:, tm:tm + SUBLANES, :]

    def conv(slot, base, cw, cb):
        taps = [up_ref[slot, base - (CONV_W - 1 - k):base - (CONV_W - 1 - k) + hm, :]
                for k in range(CONV_W)]
        return cb + taps[0] * cw[0:1, :] + taps[1] * cw[1:2, :] + taps[2] * cw[2:3, :]

    for r in range(FFN_ROW_SPLIT):
        base = SUBLANES + r * hm
        val = conv(0, base, cwv_ref[...], cbv_ref[...])
        gate = conv(1, base, cwg_ref[...], cbg_ref[...])
        act = (val * (gate * _sigmoid(gate))).astype(BF16)
        acc_ref[r * hm:(r + 1) * hm, :] += _dot(act, wd_ref[...])

    @pl.when(j == nj - 1)
    def _():
        o_ref[...] = x_ref[...] + _rms(acc_ref[...], g2_ref[...])


def conv_ffn(x, g1, w_up, conv_w, conv_b, w_down, layer, g2, *, seq, tm=ROW_TILE, tf=FFN_COL_TILE):
    m, d = x.shape
    f = w_down.shape[1]
    nj = f // tf
    tiles_per_batch = seq // tm
    conv_b = conv_b.reshape(conv_b.shape[0], 1, 2 * f)
    return pl.pallas_call(
        functools.partial(_conv_ffn_kernel, nj=nj, tiles_per_batch=tiles_per_batch),
        grid=(m // tm, nj),
        in_specs=[
            pl.BlockSpec((tm, d), lambda i, j: (i, 0)),
            pl.BlockSpec((1, d), lambda i, j: (0, 0)),
            _layer_spec(layer, (d, tf), lambda i, j: (0, j)),
            _layer_spec(layer, (d, tf), lambda i, j: (0, j + nj)),
            _layer_spec(layer, (CONV_W, tf), lambda i, j: (0, j)),
            _layer_spec(layer, (CONV_W, tf), lambda i, j: (0, j + nj)),
            _layer_spec(layer, (1, tf), lambda i, j: (0, j)),
            _layer_spec(layer, (1, tf), lambda i, j: (0, j + nj)),
            _layer_spec(layer, (tf, d), lambda i, j: (j, 0)),
            pl.BlockSpec((1, d), lambda i, j: (0, 0)),
        ],
        out_specs=pl.BlockSpec((tm, d), lambda i, j: (i, 0)),
        out_shape=jax.ShapeDtypeStruct((m, d), F32),
        scratch_shapes=[
            pltpu.VMEM((tm, d), BF16),
            pltpu.VMEM((tm, d), F32),
            pltpu.VMEM((nj, 2, SUBLANES, tf), F32),
            pltpu.VMEM((2, SUBLANES + tm, tf), F32),
        ],
        compiler_params=_params("arbitrary", "arbitrary"),
        name="conv_ffn",
    )(x, g1.reshape(1, d), w_up, w_up, conv_w, conv_w, conv_b, conv_b, w_down, g2.reshape(1, d))


def kernel(x, mem, norm_mix, norm_mem, norm_ffn, mem_in_norm, a_w_in, a_lam_re, a_lam_im, a_log_dt,
           a_b_re, a_b_im, a_c_re, a_c_im, a_d, a_w_glu, kv_norm, w_k, w_v, b_w_q, b_rel_bias, b_w_o,
           m_w_q, m_w_kv, m_w_o, f_w_up, f_conv_w, f_conv_b, f_w_down):
    bsz, seq, d = x.shape
    depth = norm_mix.shape[0]
    n_a = a_w_in.shape[0]
    n_mem = mem.shape[1]
    md = m_w_q.shape[2]
    assert seq % ROW_TILE == 0 and seq % (ATT_QGROUP * ATT_CHUNK) == 0
    assert (seq // S5_CHUNK) % LANES == 0 and (d // S5_GROUP) % 2 == 0

    xf = x.reshape(bsz * seq, d)

    w_kv_mem = jnp.transpose(m_w_kv, (1, 0, 2)).reshape(1, d, depth * 2 * md).astype(BF16)
    w_kv = jnp.concatenate([w_k, w_v], axis=1).astype(BF16)[None]
    a_w_in_t = jnp.transpose(a_w_in, (0, 2, 1)).astype(BF16)
    a_w_glu_b = a_w_glu.astype(BF16)
    b_w_q_b = (b_w_q * (ATT_HEAD_DIM ** -0.5)).astype(BF16)
    b_w_o_b = b_w_o.astype(BF16)
    m_w_q_b = (m_w_q * (MEM_HEAD_DIM ** -0.5)).astype(BF16)
    m_w_o_b = m_w_o.astype(BF16)
    f_w_up_b = f_w_up.astype(BF16)
    f_w_down_b = f_w_down.astype(BF16)
    flat = lambda a: a.reshape((-1,) + a.shape[2:])
    s5_ops = _s5_operators(flat(a_lam_re), flat(a_lam_im), flat(a_log_dt), flat(a_b_re),
                           flat(a_b_im), flat(a_c_re), flat(a_c_im), a_d, bsz)
    s5_ops = [o.reshape((n_a, o.shape[0] // n_a) + o.shape[1:]) for o in s5_ops[:3]] + [
        o.reshape(n_a, 1, -1) for o in s5_ops[3:]]
    att_bias = jax.vmap(_band_bias)(b_rel_bias)

    kvmem = norm_matmul(mem.reshape(bsz * n_mem, d), mem_in_norm, w_kv_mem, 0)

    kv = None
    for l in range(depth):
        if l == n_a:
            kv = norm_matmul(xf, kv_norm, w_kv, 0)
        if l < n_a:
            ut = norm_matmul_t(xf, norm_mix[l, 0], a_w_in_t, l, bsz=bsz, seq=seq)
            yt = s5_mixer(ut, bsz, seq, s5_ops, l)
            xf = glu_res(yt, a_w_glu_b, l, xf, norm_mix[l, 1], bsz=bsz, seq=seq)
        else:
            jb = l - n_a
            q = norm_matmul(xf, norm_mix[l, 0], b_w_q_b, jb)
            o = band_attention(q, kv, att_bias, jb, bsz=bsz, seq=seq)
            xf = proj_res(o, b_w_o_b, jb, xf, norm_mix[l, 1])
        xf = mem_attention(xf, norm_mem[l, 0], m_w_q_b, kvmem, l, m_w_o_b, norm_mem[l, 1],
                           seq=seq, n_mem=n_mem)
        xf = conv_ffn(xf, norm_ffn[l, 0], f_w_up_b, f_conv_w, f_conv_b, f_w_down_b, l,
                      norm_ffn[l, 1], seq=seq)
    return xf.reshape(bsz, seq, d)
```

```python
import functools

import numpy as np
import jax
import jax.numpy as jnp
from jax import lax
from jax.experimental import pallas as pl
from jax.experimental.pallas import tpu as pltpu

EPS = 1e-6
NEG_INF = -1e30
BF16 = jnp.bfloat16
F32 = jnp.float32

S5_GROUP = 16
S5_CHUNK = 16
S5_POS_PER_STEP = 8
ATT_CHUNK = 64
ATT_LEFT = 8
ATT_HEAD_DIM = 128
ATT_MAX_REL = 256
ATT_QGROUP = 4
MEM_HEAD_DIM = 128
CONV_W = 3

VMEM_LIMIT_BYTES = 56 * 1024 * 1024
ROW_TILE = 512
COL_TILE = 2048
FFN_ROW_SPLIT = 2
FFN_COL_TILE = 512
GLU_COL_TILE = 512
SUBLANES = 8
LANES = 128

_NT = (((1,), (1,)), ((), ()))


def _params(*sem):
    return pltpu.CompilerParams(dimension_semantics=sem, vmem_limit_bytes=VMEM_LIMIT_BYTES)


def _rms(x, g):
    ms = jnp.mean(x * x, axis=-1, keepdims=True)
    return x * lax.rsqrt(ms + EPS) * g


def _sigmoid(x):
    return 1.0 / (1.0 + jnp.exp(-x))


def _dot(a, b):
    return jnp.dot(a, b, preferred_element_type=F32)


def _dot_nt(a, b):
    return lax.dot_general(a, b, _NT, preferred_element_type=F32)


def _layer_spec(layer, block, index_map):
    return pl.BlockSpec((None,) + tuple(block), lambda *ids: (layer,) + tuple(index_map(*ids)))


def _norm_matmul_kernel(x_ref, g_ref, w_ref, o_ref, h_ref):
    @pl.when(pl.program_id(1) == 0)
    def _():
        h_ref[...] = _rms(x_ref[...], g_ref[...]).astype(h_ref.dtype)

    o_ref[...] = _dot(h_ref[...], w_ref[...]).astype(o_ref.dtype)


def norm_matmul(x, g, w, layer, *, tm=ROW_TILE, tn=COL_TILE, out_dtype=BF16):
    m, k = x.shape
    n = w.shape[2]
    tn = min(tn, n)
    return pl.pallas_call(
        _norm_matmul_kernel,
        grid=(m // tm, n // tn),
        in_specs=[
            pl.BlockSpec((tm, k), lambda i, j: (i, 0)),
            pl.BlockSpec((1, k), lambda i, j: (0, 0)),
            _layer_spec(layer, (k, tn), lambda i, j: (0, j)),
        ],
        out_specs=pl.BlockSpec((tm, tn), lambda i, j: (i, j)),
        out_shape=jax.ShapeDtypeStruct((m, n), out_dtype),
        scratch_shapes=[pltpu.VMEM((tm, k), BF16)],
        compiler_params=_params("parallel", "arbitrary"),
        name="norm_matmul",
    )(x, g.reshape(1, k), w)


def _proj_res_kernel(a_ref, w_ref, x_ref, g_ref, o_ref):
    f = _dot(a_ref[...], w_ref[...])
    o_ref[...] = x_ref[...] + _rms(f, g_ref[...])


def proj_res(a, w, layer, x, g, *, tm=ROW_TILE):
    m, k = a.shape
    d = w.shape[2]
    return pl.pallas_call(
        _proj_res_kernel,
        grid=(m // tm,),
        in_specs=[
            pl.BlockSpec((tm, k), lambda i: (i, 0)),
            _layer_spec(layer, (k, d), lambda i: (0, 0)),
            pl.BlockSpec((tm, d), lambda i: (i, 0)),
            pl.BlockSpec((1, d), lambda i: (0, 0)),
        ],
        out_specs=pl.BlockSpec((tm, d), lambda i: (i, 0)),
        out_shape=jax.ShapeDtypeStruct((m, d), F32),
        compiler_params=_params("parallel"),
        name="proj_res",
    )(a, w, x, g.reshape(1, d))


def _s5_build_kernel(rev_ref, bb_ref, cc_ref, cpad_ref, pt_ref, d_ref, t_ref, w_ref, v_ref):
    n_l = S5_CHUNK
    n_h = S5_GROUP
    lh = n_l * n_h
    n_p = rev_ref.shape[2]

    def exact_dot(a, b):
        return jnp.dot(a, b, precision=lax.Precision.HIGHEST, preferred_element_type=F32)

    lane = lax.broadcasted_iota(jnp.int32, (n_l, lh), 1)
    row = lax.broadcasted_iota(jnp.int32, (n_l, lh), 0)
    per_step = (lane // n_h == row).astype(F32)
    per_chan = (lane % n_h == row).astype(F32)
    sub = lax.broadcasted_iota(jnp.int32, (n_h, lh), 0)
    lan = lax.broadcasted_iota(jnp.int32, (n_h, lh), 1)
    no_state = jnp.zeros((n_p, lh), BF16)
    for k in range(2):
        r_re = exact_dot(rev_ref[k, 0], per_step)
        r_im = exact_dot(rev_ref[k, 1], per_step)
        b_re = exact_dot(bb_ref[k, 0], per_chan)
        b_im = exact_dot(bb_ref[k, 1], per_chan)
        w_re = r_re * b_re - r_im * b_im
        w_im = r_re * b_im + r_im * b_re
        for ri, w in enumerate((w_re, w_im)):
            for slot in range(2):
                w_ref[0, k, ri, slot * n_p:(slot + 1) * n_p, :] = (
                    w.astype(BF16) if slot == k else no_state)
        krev = exact_dot(cc_ref[k, 0], w_re) - exact_dot(cc_ref[k, 1], w_im)
        skip = d_ref[k]
        for t in range(n_l):
            sh = (n_l - 1 - t) * n_h
            slab = krev
            if sh:
                slab = jnp.where(lan < lh - sh, pltpu.roll(krev, lh - sh, axis=1), 0.0)
            slab = slab + jnp.where(lan == t * n_h + sub, skip, 0.0)
            t_ref[k, t * n_h:(t + 1) * n_h, :] = slab.astype(BF16)
        c_re = cpad_ref[k, 0]
        c_im = cpad_ref[k, 1]
        for t in range(n_l):
            p_re = pt_ref[k, 0, t + 1:t + 2, :]
            p_im = pt_ref[k, 1, t + 1:t + 2, :]
            v_ref[0, k, 0, t * n_h:(t + 1) * n_h, :] = (c_re * p_re - c_im * p_im).astype(BF16)
            v_ref[0, k, 1, t * n_h:(t + 1) * n_h, :] = (-(c_re * p_im + c_im * p_re)).astype(BF16)


def _s5_operators(lam_re, lam_im, log_dt, b_re, b_im, c_re, c_im, d_skip, bsz):
    n_g, n_p = lam_re.shape
    n_h = S5_GROUP
    n_l = S5_CHUNK
    lh = n_l * n_h
    lr = lam_re.astype(F32)
    li = lam_im.astype(F32)
    dt = jnp.exp(log_dt.astype(F32))[:, None]
    mag = jnp.exp(lr * dt)
    ang = li * dt
    ab_re = mag * jnp.cos(ang)
    ab_im = mag * jnp.sin(ang)
    den = lr * lr + li * li
    nr = ab_re - 1.0
    f_re = (nr * lr + ab_im * li) / den
    f_im = (ab_im * lr - nr * li) / den
    br = b_re.astype(F32)
    bi = b_im.astype(F32)
    bb = jnp.stack([f_re[..., None] * br - f_im[..., None] * bi,
                    f_re[..., None] * bi + f_im[..., None] * br], axis=1)
    tau = jnp.arange(n_l + 1, dtype=F32)
    pw_re = jnp.exp((lr * dt)[..., None] * tau) * jnp.cos(ang[..., None] * tau)
    pw_im = jnp.exp((lr * dt)[..., None] * tau) * jnp.sin(ang[..., None] * tau)
    tau_m = tau[None, :, None]
    pt = jnp.stack([jnp.exp((lr * dt)[:, None, :] * tau_m) * jnp.cos(ang[:, None, :] * tau_m),
                    jnp.exp((lr * dt)[:, None, :] * tau_m) * jnp.sin(ang[:, None, :] * tau_m)],
                   axis=1)
    rev = jnp.stack([pw_re[:, :, n_l - 1::-1], pw_im[:, :, n_l - 1::-1]], axis=1)
    cc = jnp.stack([c_re.astype(F32), c_im.astype(F32)], axis=1)
    slot = (jnp.arange(n_g) % 2)[:, None, None, None]

    def lane_place(a):
        z = jnp.zeros_like(a)
        return jnp.where(slot == 0, jnp.concatenate([a, z], -1), jnp.concatenate([z, a], -1))

    skip = jnp.broadcast_to(d_skip.astype(F32).reshape(n_g, n_h, 1), (n_g, n_h, lh))
    n_pair = n_g // 2
    pair = lambda shape: pl.BlockSpec((2,) + shape, lambda j: (j,) + (0,) * len(shape))
    t_mat, w_pack, v_pack = pl.pallas_call(
        _s5_build_kernel,
        grid=(n_pair,),
        in_specs=[pair((2, n_p, n_l)), pair((2, n_p, n_h)), pair((2, n_h, n_p)),
                  pair((2, n_h, 2 * n_p)), pair((2, n_l + 1, 2 * n_p)), pair((n_h, lh))],
        out_specs=[
            pl.BlockSpec((2, lh, lh), lambda j: (j, 0, 0)),
            pl.BlockSpec((1, 2, 2, 2 * n_p, lh), lambda j: (j, 0, 0, 0, 0)),
            pl.BlockSpec((1, 2, 2, lh, 2 * n_p), lambda j: (j, 0, 0, 0, 0)),
        ],
        out_shape=[
            jax.ShapeDtypeStruct((n_g, lh, lh), BF16),
            jax.ShapeDtypeStruct((n_pair, 2, 2, 2 * n_p, lh), BF16),
            jax.ShapeDtypeStruct((n_pair, 2, 2, lh, 2 * n_p), BF16),
        ],
        compiler_params=_params("parallel"),
        name="s5_build_operators",
    )(rev, bb, cc, lane_place(cc), lane_place(pt), skip)

    def state_cols(a):
        a = jnp.broadcast_to(a.reshape(n_pair, 1, 2 * n_p), (n_pair, bsz, 2 * n_p))
        return a.reshape(1, n_g * bsz * n_p)
    return t_mat, w_pack, v_pack, state_cols(pw_re[:, :, n_l]), state_cols(pw_im[:, :, n_l])


def _chunk_view(x, bsz, seq):
    return x.reshape(bsz, seq // S5_CHUNK, S5_CHUNK * x.shape[1])


def _norm_matmul_t_kernel(x_ref, g_ref, wt_ref, o_ref, h_ref):
    nc = x_ref.shape[0]
    d = g_ref.shape[1]

    @pl.when(pl.program_id(2) == 0)
    def _():
        for k in range(S5_POS_PER_STEP):
            h_ref[k * nc:(k + 1) * nc, :] = (
                _rms(x_ref[:, k * d:(k + 1) * d], g_ref[...]).astype(BF16))

    o_ref[...] = _dot_nt(wt_ref[...], h_ref[...]).astype(o_ref.dtype)


def norm_matmul_t(x, g, wt, layer, *, bsz, seq, tn=1024):
    d = x.shape[1]
    n = wt.shape[1]
    tn = min(tn, n)
    nc = seq // S5_CHUNK
    sq = S5_POS_PER_STEP
    steps = S5_CHUNK // sq
    return pl.pallas_call(
        _norm_matmul_t_kernel,
        grid=(bsz, steps, n // tn),
        in_specs=[
            pl.BlockSpec((None, nc, sq * d), lambda b, s, j: (b, 0, s)),
            pl.BlockSpec((1, d), lambda b, s, j: (0, 0)),
            _layer_spec(layer, (tn, d), lambda b, s, j: (j, 0)),
        ],
        out_specs=pl.BlockSpec((tn, sq * nc), lambda b, s, j: (j, b * steps + s)),
        out_shape=jax.ShapeDtypeStruct((n, bsz * seq), BF16),
        scratch_shapes=[pltpu.VMEM((sq * nc, d), BF16)],
        compiler_params=_params("parallel", "parallel", "arbitrary"),
        name="norm_matmul_t",
    )(_chunk_view(x, bsz, seq), g.reshape(1, d), wt)


def _chunk_operand(u_ref, k, nc):
    h = S5_GROUP
    return jnp.concatenate(
        [u_ref[k * h:(k + 1) * h, s * nc:(s + 1) * nc] for s in range(S5_CHUNK)], axis=0)


def _s5_inject_kernel(u_ref, w_ref, ore_ref, oim_ref, *, bsz, seq):
    nc = ore_ref.shape[0]
    p2 = w_ref.shape[3]
    for b in range(bsz):
        ub = u_ref.at[:, b * seq:(b + 1) * seq]
        u0 = _chunk_operand(ub, 0, nc)
        u1 = _chunk_operand(ub, 1, nc)
        cols = slice(b * p2, (b + 1) * p2)
        ore_ref[:, cols] = (_dot(w_ref[0, 0, 0], u0) + _dot(w_ref[0, 1, 0], u1)).T
        oim_ref[:, cols] = (_dot(w_ref[0, 0, 1], u0) + _dot(w_ref[0, 1, 1], u1)).T


def s5_inject(ut, w_pack, layer, *, bsz, seq):
    n_pair = w_pack.shape[1]
    p2 = w_pack.shape[4]
    nc = seq // S5_CHUNK
    out = jax.ShapeDtypeStruct((nc, n_pair * bsz * p2), F32)
    oblk = pl.BlockSpec((nc, bsz * p2), lambda j: (0, j))
    return pl.pallas_call(
        functools.partial(_s5_inject_kernel, bsz=bsz, seq=seq),
        grid=(n_pair,),
        in_specs=[
            pl.BlockSpec((2 * S5_GROUP, bsz * seq), lambda j: (j, 0)),
            _layer_spec(layer, (1,) + w_pack.shape[2:], lambda j: (j, 0, 0, 0, 0)),
        ],
        out_specs=[oblk, oblk],
        out_shape=[out, out],
        compiler_params=_params("parallel"),
        name="s5_inject",
    )(ut, w_pack)


def _s5_scan_kernel(wre_ref, wim_ref, are_ref, aim_ref, xre_ref, xim_ref, *, n_c):
    a_re = are_ref[...]
    a_im = aim_ref[...]

    def body(c, carry):
        s_re, s_im = carry
        row = pl.ds(c, 1)
        xre_ref[row, :] = s_re
        xim_ref[row, :] = s_im
        n_re = a_re * s_re - a_im * s_im + wre_ref[row, :]
        n_im = a_re * s_im + a_im * s_re + wim_ref[row, :]
        return n_re, n_im

    zero = jnp.zeros(a_re.shape, F32)
    lax.fori_loop(0, n_c, body, (zero, zero))


def s5_scan(w_re, w_im, a_re, a_im, layer, *, tl=2048):
    n_c, cols = w_re.shape
    tl = min(tl, cols)
    out = jax.ShapeDtypeStruct((n_c, cols), F32)
    blk = pl.BlockSpec((n_c, tl), lambda j: (0, j))
    coef = _layer_spec(layer, (1, tl), lambda j: (0, j))
    return pl.pallas_call(
        functools.partial(_s5_scan_kernel, n_c=n_c),
        grid=(cols // tl,),
        in_specs=[blk, blk, coef, coef],
        out_specs=[blk, blk],
        out_shape=[out, out],
        compiler_params=_params("parallel"),
        name="s5_scan",
    )(w_re, w_im, a_re, a_im)


def _gelu_tanh(y):
    c = np.float32(np.sqrt(2.0 / np.pi))
    return 0.5 * y * (1.0 + jnp.tanh(c * (y + 0.044715 * (y * y * y))))


def _s5_out_kernel(u_ref, t_ref, xre_ref, xim_ref, v_ref, o_ref, *, bsz, seq):
    nc = xre_ref.shape[0]
    p2 = v_ref.shape[-1]
    h = S5_GROUP
    for b in range(bsz):
        ub = u_ref.at[:, b * seq:(b + 1) * seq]
        x_re = xre_ref[:, b * p2:(b + 1) * p2].astype(BF16)
        x_im = xim_ref[:, b * p2:(b + 1) * p2].astype(BF16)
        for k in range(2):
            y = (_dot(t_ref[k], _chunk_operand(ub, k, nc))
                 + _dot_nt(v_ref[0, k, 0], x_re) + _dot_nt(v_ref[0, k, 1], x_im))
            y = _gelu_tanh(y).astype(o_ref.dtype)
            for t in range(S5_CHUNK):
                lanes = slice(b * seq + t * nc, b * seq + (t + 1) * nc)
                o_ref[k * h:(k + 1) * h, lanes] = y[t * h:(t + 1) * h, :]


def s5_out(ut, t_mat, x_re, x_im, v_pack, layer, *, bsz, seq):
    n_pair = v_pack.shape[1]
    p2 = v_pack.shape[-1]
    nc = seq // S5_CHUNK
    lh = t_mat.shape[-1]
    xblk = pl.BlockSpec((nc, bsz * p2), lambda j: (0, j))
    ublk = pl.BlockSpec((2 * S5_GROUP, bsz * seq), lambda j: (j, 0))
    return pl.pallas_call(
        functools.partial(_s5_out_kernel, bsz=bsz, seq=seq),
        grid=(n_pair,),
        in_specs=[
            ublk,
            _layer_spec(layer, (2, lh, lh), lambda j: (j, 0, 0)),
            xblk,
            xblk,
            _layer_spec(layer, (1,) + v_pack.shape[2:], lambda j: (j, 0, 0, 0, 0)),
        ],
        out_specs=ublk,
        out_shape=jax.ShapeDtypeStruct(ut.shape, BF16),
        compiler_params=_params("parallel"),
        name="s5_out",
    )(ut, t_mat, x_re, x_im, v_pack)


def s5_mixer(ut, bsz, seq, ops, layer):
    t_mat, w_pack, v_pack, a_re, a_im = ops
    w_re, w_im = s5_inject(ut, w_pack, layer, bsz=bsz, seq=seq)
    x_re, x_im = s5_scan(w_re, w_im, a_re, a_im, layer)
    return s5_out(ut, t_mat, x_re, x_im, v_pack, layer, bsz=bsz, seq=seq)


def _glu_kernel(yt_ref, wv_ref, wg_ref, g_ref, o_ref, y_ref, m_ref, *, nj, tn):
    j = pl.program_id(2)
    nc = o_ref.shape[0]

    @pl.when(j == 0)
    def _():
        for k in range(S5_POS_PER_STEP):
            y_ref[k * nc:(k + 1) * nc, :] = (
                yt_ref[:, k * nc:(k + 1) * nc].astype(F32).T.astype(BF16))

    y = y_ref[...]
    m_ref[j] = _dot(y, wv_ref[...]) * _sigmoid(_dot(y, wg_ref[...]))

    @pl.when(j == nj - 1)
    def _():
        d = g_ref.shape[1]
        ss = None
        for jj in range(nj):
            mj = m_ref[jj]
            s = jnp.sum(mj * mj, axis=-1, keepdims=True)
            ss = s if ss is None else ss + s
        inv = lax.rsqrt(ss / d + EPS)
        for k in range(S5_POS_PER_STEP):
            rows = slice(k * nc, (k + 1) * nc)
            for jj in range(nj):
                cols = slice(jj * tn, (jj + 1) * tn)
                o_ref[:, k, cols] = m_ref[jj, rows, :] * inv[rows] * g_ref[:, cols]


def glu_norm(yt, w_glu, layer, g, *, bsz, seq, tn=GLU_COL_TILE):
    k, _ = yt.shape
    d = w_glu.shape[2] // 2
    tn = min(tn, d)
    nj = d // tn
    nc = seq // S5_CHUNK
    sq = S5_POS_PER_STEP
    steps = S5_CHUNK // sq
    out = pl.pallas_call(
        functools.partial(_glu_kernel, nj=nj, tn=tn),
        grid=(bsz, steps, nj),
        in_specs=[
            pl.BlockSpec((k, sq * nc), lambda b, s, j: (0, b * steps + s)),
            _layer_spec(layer, (k, tn), lambda b, s, j: (0, j)),
            _layer_spec(layer, (k, tn), lambda b, s, j: (0, j + nj)),
            pl.BlockSpec((1, d), lambda b, s, j: (0, 0)),
        ],
        out_specs=pl.BlockSpec((None, nc, sq, d), lambda b, s, j: (b, 0, s, 0)),
        out_shape=jax.ShapeDtypeStruct((bsz, nc, S5_CHUNK, d), F32),
        scratch_shapes=[pltpu.VMEM((sq * nc, k), BF16), pltpu.VMEM((nj, sq * nc, tn), F32)],
        compiler_params=_params("parallel", "parallel", "arbitrary"),
        name="glu_norm",
    )(yt, w_glu, w_glu, g.reshape(1, d))
    return out.reshape(bsz * seq, d)


def _band_bias(rel_bias):
    c = ATT_CHUNK
    nq = ATT_QGROUP * c
    nk = (ATT_QGROUP + ATT_LEFT) * c
    r = rel_bias.astype(F32)
    n_h = r.shape[0]
    d_lo = ATT_LEFT * c - (nk - 1)
    d_hi = ATT_LEFT * c + nq - 1
    n_lo = -(c - 1) - d_lo
    n_hi = d_hi - ATT_MAX_REL
    e = jnp.concatenate([jnp.broadcast_to(r[:, :1], (n_h, n_lo)), r,
                         jnp.broadcast_to(r[:, -1:], (n_h, n_hi))], axis=1)
    width = nq + nk - 1
    assert e.shape[1] == width
    frev = jnp.concatenate([e[:, ::-1], jnp.zeros((n_h, 1), F32)], axis=1)
    skew = jnp.tile(frev, (1, nq))[:, :nq * width].reshape(n_h, nq, width)
    bias = skew[:, :, nq - 1:nq - 1 + nk]
    qa = np.arange(nq)[:, None] // c
    kc = np.arange(nk)[None, :] // c
    valid = (kc >= qa) & (kc <= qa + ATT_LEFT)
    return jnp.where(jnp.asarray(valid)[None], bias, NEG_INF)


def _band_attn_kernel(q_ref, k_ref, v_ref, b_ref, o_ref, s_ref, p_ref, v1_ref, *, seq):
    c = ATT_CHUNK
    dh = ATT_HEAD_DIM
    nq = ATT_QGROUP * c
    nk = (ATT_QGROUP + ATT_LEFT) * c
    v1_ref[:, :dh] = v_ref[...]
    v1_ref[:, dh:] = jnp.ones((seq, dh), BF16)
    for gi in range(seq // nq):
        q0 = gi * nq
        off = max(0, ATT_LEFT * c - q0)
        k0 = q0 - ATT_LEFT * c + off
        w = nk - off
        s_ref[:, :w] = _dot_nt(q_ref[q0:q0 + nq, :], k_ref[k0:k0 + w, :])
        for a in range(ATT_QGROUP):
            rows = slice(a * c, (a + 1) * c)
            lo = max((a * c) // LANES * LANES, off)
            hi = min(-(-((a + ATT_LEFT + 1) * c) // LANES) * LANES, nk)
            s = s_ref[rows, lo - off:hi - off] + b_ref[0, rows, lo:hi]
            m = jnp.max(s, axis=-1, keepdims=True)
            p_ref[rows, lo - off:hi - off] = jnp.exp(s - m).astype(BF16)
            if lo > off:
                p_ref[rows, 0:lo - off] = jnp.zeros((c, lo - off), BF16)
            if hi < nk:
                p_ref[rows, hi - off:w] = jnp.zeros((c, nk - hi), BF16)
        o2 = _dot(p_ref[:, :w], v1_ref[k0:k0 + w, :])
        o_ref[q0:q0 + nq, :] = (o2[:, :dh] / o2[:, dh:]).astype(o_ref.dtype)


def band_attention(q, kv, bias, layer, *, bsz, seq):
    m, d = q.shape
    n_h = d // ATT_HEAD_DIM
    dh = ATT_HEAD_DIM
    nq, nk = bias.shape[2:]
    return pl.pallas_call(
        functools.partial(_band_attn_kernel, seq=seq),
        grid=(bsz, n_h),
        in_specs=[
            pl.BlockSpec((seq, dh), lambda b, h: (b, h)),
            pl.BlockSpec((seq, dh), lambda b, h: (b, h)),
            pl.BlockSpec((seq, dh), lambda b, h: (b, h + n_h)),
            _layer_spec(layer, (1, nq, nk), lambda b, h: (h, 0, 0)),
        ],
        out_specs=pl.BlockSpec((seq, dh), lambda b, h: (b, h)),
        out_shape=jax.ShapeDtypeStruct((m, d), BF16),
        scratch_shapes=[
            pltpu.VMEM((nq, nk), F32),
            pltpu.VMEM((nq, nk), BF16),
            pltpu.VMEM((seq, 2 * dh), BF16),
        ],
        compiler_params=_params("parallel", "parallel"),
        name="band_attention",
    )(q, kv, kv, bias)


def _mem_attn_kernel(*refs, has_update):
    if has_update:
        x_ref, r_ref, g1_ref, wq_ref, k_ref, v_ref, wo_ref, g2_ref, o_ref = refs
        x = x_ref[...] + r_ref[...]
    else:
        x_ref, g1_ref, wq_ref, k_ref, v_ref, wo_ref, g2_ref, o_ref = refs
        x = x_ref[...]
    h = _rms(x, g1_ref[...]).astype(BF16)
    q = _dot(h, wq_ref[...]).astype(BF16)
    k = k_ref[...]
    v = v_ref[...]
    dh = MEM_HEAD_DIM
    n_heads = q.shape[1] // dh
    scores = [_dot_nt(q[:, hd * dh:(hd + 1) * dh], k[:, hd * dh:(hd + 1) * dh])
              for hd in range(n_heads)]
    ones = jnp.ones((v.shape[0], dh), BF16)
    heads = []
    for hd in range(n_heads):
        s = scores[hd]
        p = jnp.exp(s - jnp.max(s, axis=-1, keepdims=True)).astype(BF16)
        o2 = _dot(p, jnp.concatenate([v[:, hd * dh:(hd + 1) * dh], ones], axis=-1))
        heads.append((o2[:, :dh] / o2[:, dh:]).astype(BF16))
    o = jnp.concatenate(heads, axis=-1)
    c = _dot(o, wo_ref[...])
    o_ref[...] = x + _rms(c, g2_ref[...])


def mem_attention(x, g1, wq, kvmem, layer, wo, g2, *, seq, n_mem, update=None, tm=ROW_TILE):
    m, d = x.shape
    md = wq.shape[2]
    tiles_per_batch = seq // tm
    row = pl.BlockSpec((tm, d), lambda i: (i, 0))
    acts = [x] if update is None else [x, update]
    return pl.pallas_call(
        functools.partial(_mem_attn_kernel, has_update=update is not None),
        grid=(m // tm,),
        in_specs=[row] * len(acts) + [
            pl.BlockSpec((1, d), lambda i: (0, 0)),
            _layer_spec(layer, (d, md), lambda i: (0, 0)),
            pl.BlockSpec((n_mem, md), lambda i: (i // tiles_per_batch, 2 * layer)),
            pl.BlockSpec((n_mem, md), lambda i: (i // tiles_per_batch, 2 * layer + 1)),
            _layer_spec(layer, (md, d), lambda i: (0, 0)),
            pl.BlockSpec((1, d), lambda i: (0, 0)),
        ],
        out_specs=row,
        out_shape=jax.ShapeDtypeStruct((m, d), F32),
        compiler_params=_params("parallel"),
        name="mem_attention",
    )(*acts, g1.reshape(1, d), wq, kvmem, kvmem, wo, g2.reshape(1, d))


def _conv_ffn_kernel(x_ref, g1_ref, wuv_ref, wug_ref, cwv_ref, cwg_ref, cbv_ref, cbg_ref, wd_ref,
                     g2_ref, o_ref, h_ref, acc_ref, tail_ref, up_ref, *, nj, tiles_per_batch):
    i = pl.program_id(0)
    j = pl.program_id(1)
    tm = x_ref.shape[0]
    hm = tm // FFN_ROW_SPLIT

    @pl.when(j == 0)
    def _():
        h_ref[...] = _rms(x_ref[...], g1_ref[...]).astype(h_ref.dtype)
        acc_ref[...] = jnp.zeros(acc_ref.shape, F32)

    @pl.when(i % tiles_per_batch == 0)
    def _():
        tail_ref[j] = jnp.zeros(tail_ref.shape[1:], F32)

    up_ref[:, 0:SUBLANES, :] = tail_ref[j]

    for r in range(FFN_ROW_SPLIT):
        h = h_ref[r * hm:(r + 1) * hm, :]
        rows = slice(SUBLANES + r * hm, SUBLANES + (r + 1) * hm)
        up_ref[0, rows, :] = _dot(h, wuv_ref[...])
        up_ref[1, rows, :] = _dot(h, wug_ref[...])

    tail_ref[j] = up_ref[:, tm:tm + SUBLANES, :]

    def conv(slot, base, cw, cb):
        taps = [up_ref[slot, base - (CONV_W - 1 - k):base - (CONV_W - 1 - k) + hm, :]
                for k in range(CONV_W)]
        return cb + taps[0] * cw[0:1, :] + taps[1] * cw[1:2, :] + taps[2] * cw[2:3, :]

    for r in range(FFN_ROW_SPLIT):
        base = SUBLANES + r * hm
        val = conv(0, base, cwv_ref[...], cbv_ref[...])
        gate = conv(1, base, cwg_ref[...], cbg_ref[...])
        act = (val * (gate * _sigmoid(gate))).astype(BF16)
        acc_ref[r * hm:(r + 1) * hm, :] += _dot(act, wd_ref[...])

    @pl.when(j == nj - 1)
    def _():
        o_ref[...] = x_ref[...] + _rms(acc_ref[...], g2_ref[...])


def conv_ffn(x, g1, w_up, conv_w, conv_b, w_down, layer, g2, *, seq, tm=ROW_TILE, tf=FFN_COL_TILE):
    m, d = x.shape
    f = w_down.shape[1]
    nj = f // tf
    tiles_per_batch = seq // tm
    conv_b = conv_b.reshape(conv_b.shape[0], 1, 2 * f)
    return pl.pallas_call(
        functools.partial(_conv_ffn_kernel, nj=nj, tiles_per_batch=tiles_per_batch),
        grid=(m // tm, nj),
        in_specs=[
            pl.BlockSpec((tm, d), lambda i, j: (i, 0)),
            pl.BlockSpec((1, d), lambda i, j: (0, 0)),
            _layer_spec(layer, (d, tf), lambda i, j: (0, j)),
            _layer_spec(layer, (d, tf), lambda i, j: (0, j + nj)),
            _layer_spec(layer, (CONV_W, tf), lambda i, j: (0, j)),
            _layer_spec(layer, (CONV_W, tf), lambda i, j: (0, j + nj)),
            _layer_spec(layer, (1, tf), lambda i, j: (0, j)),
            _layer_spec(layer, (1, tf), lambda i, j: (0, j + nj)),
            _layer_spec(layer, (tf, d), lambda i, j: (j, 0)),
            pl.BlockSpec((1, d), lambda i, j: (0, 0)),
        ],
        out_specs=pl.BlockSpec((tm, d), lambda i, j: (i, 0)),
        out_shape=jax.ShapeDtypeStruct((m, d), F32),
        scratch_shapes=[
            pltpu.VMEM((tm, d), BF16),
            pltpu.VMEM((tm, d), F32),
            pltpu.VMEM((nj, 2, SUBLANES, tf), F32),
            pltpu.VMEM((2, SUBLANES + tm, tf), F32),
        ],
        compiler_params=_params("arbitrary", "arbitrary"),
        name="conv_ffn",
    )(x, g1.reshape(1, d), w_up, w_up, conv_w, conv_w, conv_b, conv_b, w_down, g2.reshape(1, d))


def kernel(x, mem, norm_mix, norm_mem, norm_ffn, mem_in_norm, a_w_in, a_lam_re, a_lam_im, a_log_dt,
           a_b_re, a_b_im, a_c_re, a_c_im, a_d, a_w_glu, kv_norm, w_k, w_v, b_w_q, b_rel_bias, b_w_o,
           m_w_q, m_w_kv, m_w_o, f_w_up, f_conv_w, f_conv_b, f_w_down):
    bsz, seq, d = x.shape
    depth = norm_mix.shape[0]
    n_a = a_w_in.shape[0]
    n_mem = mem.shape[1]
    md = m_w_q.shape[2]
    assert seq % ROW_TILE == 0 and seq % (ATT_QGROUP * ATT_CHUNK) == 0
    assert (seq // S5_CHUNK) % LANES == 0 and (d // S5_GROUP) % 2 == 0

    xf = x.reshape(bsz * seq, d)

    w_kv_mem = jnp.transpose(m_w_kv, (1, 0, 2)).reshape(1, d, depth * 2 * md).astype(BF16)
    w_kv = jnp.concatenate([w_k, w_v], axis=1).astype(BF16)[None]
    a_w_in_t = jnp.transpose(a_w_in, (0, 2, 1)).astype(BF16)
    a_w_glu_b = a_w_glu.astype(BF16)
    b_w_q_b = (b_w_q * (ATT_HEAD_DIM ** -0.5)).astype(BF16)
    b_w_o_b = b_w_o.astype(BF16)
    m_w_q_b = (m_w_q * (MEM_HEAD_DIM ** -0.5)).astype(BF16)
    m_w_o_b = m_w_o.astype(BF16)
    f_w_up_b = f_w_up.astype(BF16)
    f_w_down_b = f_w_down.astype(BF16)
    flat = lambda a: a.reshape((-1,) + a.shape[2:])
    s5_ops = _s5_operators(flat(a_lam_re), flat(a_lam_im), flat(a_log_dt), flat(a_b_re),
                           flat(a_b_im), flat(a_c_re), flat(a_c_im), a_d, bsz)
    s5_ops = [o.reshape((n_a, o.shape[0] // n_a) + o.shape[1:]) for o in s5_ops[:3]] + [
        o.reshape(n_a, 1, -1) for o in s5_ops[3:]]
    att_bias = jax.vmap(_band_bias)(b_rel_bias)

    kvmem = norm_matmul(mem.reshape(bsz * n_mem, d), mem_in_norm, w_kv_mem, 0)

    kv = None
    for l in range(depth):
        update = None
        if l == n_a:
            kv = norm_matmul(xf, kv_norm, w_kv, 0)
        if l < n_a:
            ut = norm_matmul_t(xf, norm_mix[l, 0], a_w_in_t, l, bsz=bsz, seq=seq)
            yt = s5_mixer(ut, bsz, seq, s5_ops, l)
            update = glu_norm(yt, a_w_glu_b, l, norm_mix[l, 1], bsz=bsz, seq=seq)
        else:
            jb = l - n_a
            q = norm_matmul(xf, norm_mix[l, 0], b_w_q_b, jb)
            o = band_attention(q, kv, att_bias, jb, bsz=bsz, seq=seq)
            xf = proj_res(o, b_w_o_b, jb, xf, norm_mix[l, 1])
        xf = mem_attention(xf, norm_mem[l, 0], m_w_q_b, kvmem, l, m_w_o_b, norm_mem[l, 1],
                           seq=seq, n_mem=n_mem, update=update)
        xf = conv_ffn(xf, norm_ffn[l, 0], f_w_up_b, f_conv_w, f_conv_b, f_w_down_b, l,
                      norm_ffn[l, 1], seq=seq)
    return xf.reshape(bsz, seq, d)
```

```python
import functools

import numpy as np
import jax
import jax.numpy as jnp
from jax import lax
from jax.experimental import pallas as pl
from jax.experimental.pallas import tpu as pltpu

EPS = 1e-6
NEG_INF = -1e30
BF16 = jnp.bfloat16
F32 = jnp.float32

S5_GROUP = 16
S5_CHUNK = 16
S5_POS_PER_STEP = 8
ATT_CHUNK = 64
ATT_LEFT = 8
ATT_HEAD_DIM = 128
ATT_MAX_REL = 256
ATT_QGROUP = 4
MEM_HEAD_DIM = 128
CONV_W = 3

VMEM_LIMIT_BYTES = 56 * 1024 * 1024
ROW_TILE = 512
COL_TILE = 2048
FFN_ROW_SPLIT = 2
FFN_COL_TILE = 512
GLU_COL_TILE = 512
SUBLANES = 8
LANES = 128

_NT = (((1,), (1,)), ((), ()))


def _params(*sem):
    return pltpu.CompilerParams(dimension_semantics=sem, vmem_limit_bytes=VMEM_LIMIT_BYTES)


def _rms(x, g):
    ms = jnp.mean(x * x, axis=-1, keepdims=True)
    return x * lax.rsqrt(ms + EPS) * g


def _sigmoid(x):
    return 1.0 / (1.0 + jnp.exp(-x))


def _dot(a, b):
    return jnp.dot(a, b, preferred_element_type=F32)


def _dot_nt(a, b):
    return lax.dot_general(a, b, _NT, preferred_element_type=F32)


def _layer_spec(layer, block, index_map):
    return pl.BlockSpec((None,) + tuple(block), lambda *ids: (layer,) + tuple(index_map(*ids)))


def _norm_matmul_kernel(x_ref, g_ref, w_ref, o_ref, h_ref):
    @pl.when(pl.program_id(1) == 0)
    def _():
        h_ref[...] = _rms(x_ref[...], g_ref[...]).astype(h_ref.dtype)

    o_ref[...] = _dot(h_ref[...], w_ref[...]).astype(o_ref.dtype)


def norm_matmul(x, g, w, layer, *, tm=ROW_TILE, tn=COL_TILE, out_dtype=BF16):
    m, k = x.shape
    n = w.shape[2]
    tn = min(tn, n)
    return pl.pallas_call(
        _norm_matmul_kernel,
        grid=(m // tm, n // tn),
        in_specs=[
            pl.BlockSpec((tm, k), lambda i, j: (i, 0)),
            pl.BlockSpec((1, k), lambda i, j: (0, 0)),
            _layer_spec(layer, (k, tn), lambda i, j: (0, j)),
        ],
        out_specs=pl.BlockSpec((tm, tn), lambda i, j: (i, j)),
        out_shape=jax.ShapeDtypeStruct((m, n), out_dtype),
        scratch_shapes=[pltpu.VMEM((tm, k), BF16)],
        compiler_params=_params("parallel", "arbitrary"),
        name="norm_matmul",
    )(x, g.reshape(1, k), w)


def _proj_res_kernel(a_ref, w_ref, x_ref, g_ref, o_ref):
    f = _dot(a_ref[...], w_ref[...])
    o_ref[...] = x_ref[...] + _rms(f, g_ref[...])


def proj_res(a, w, layer, x, g, *, tm=ROW_TILE):
    m, k = a.shape
    d = w.shape[2]
    return pl.pallas_call(
        _proj_res_kernel,
        grid=(m // tm,),
        in_specs=[
            pl.BlockSpec((tm, k), lambda i: (i, 0)),
            _layer_spec(layer, (k, d), lambda i: (0, 0)),
            pl.BlockSpec((tm, d), lambda i: (i, 0)),
            pl.BlockSpec((1, d), lambda i: (0, 0)),
        ],
        out_specs=pl.BlockSpec((tm, d), lambda i: (i, 0)),
        out_shape=jax.ShapeDtypeStruct((m, d), F32),
        compiler_params=_params("parallel"),
        name="proj_res",
    )(a, w, x, g.reshape(1, d))


def _s5_build_kernel(rev_ref, bb_ref, cc_ref, cpad_ref, pt_ref, d_ref, t_ref, w_ref, v_ref):
    n_l = S5_CHUNK
    n_h = S5_GROUP
    lh = n_l * n_h
    n_p = rev_ref.shape[2]

    def exact_dot(a, b):
        return jnp.dot(a, b, precision=lax.Precision.HIGHEST, preferred_element_type=F32)

    lane = lax.broadcasted_iota(jnp.int32, (n_l, lh), 1)
    row = lax.broadcasted_iota(jnp.int32, (n_l, lh), 0)
    per_step = (lane // n_h == row).astype(F32)
    per_chan = (lane % n_h == row).astype(F32)
    sub = lax.broadcasted_iota(jnp.int32, (n_h, lh), 0)
    lan = lax.broadcasted_iota(jnp.int32, (n_h, lh), 1)
    no_state = jnp.zeros((n_p, lh), BF16)
    for k in range(2):
        r_re = exact_dot(rev_ref[k, 0], per_step)
        r_im = exact_dot(rev_ref[k, 1], per_step)
        b_re = exact_dot(bb_ref[k, 0], per_chan)
        b_im = exact_dot(bb_ref[k, 1], per_chan)
        w_re = r_re * b_re - r_im * b_im
        w_im = r_re * b_im + r_im * b_re
        for ri, w in enumerate((w_re, w_im)):
            for slot in range(2):
                w_ref[0, k, ri, slot * n_p:(slot + 1) * n_p, :] = (
                    w.astype(BF16) if slot == k else no_state)
        krev = exact_dot(cc_ref[k, 0], w_re) - exact_dot(cc_ref[k, 1], w_im)
        skip = d_ref[k]
        for t in range(n_l):
            sh = (n_l - 1 - t) * n_h
            slab = krev
            if sh:
                slab = jnp.where(lan < lh - sh, pltpu.roll(krev, lh - sh, axis=1), 0.0)
            slab = slab + jnp.where(lan == t * n_h + sub, skip, 0.0)
            t_ref[k, t * n_h:(t + 1) * n_h, :] = slab.astype(BF16)
        c_re = cpad_ref[k, 0]
        c_im = cpad_ref[k, 1]
        for t in range(n_l):
            p_re = pt_ref[k, 0, t + 1:t + 2, :]
            p_im = pt_ref[k, 1, t + 1:t + 2, :]
            v_ref[0, k, 0, t * n_h:(t + 1) * n_h, :] = (c_re * p_re - c_im * p_im).astype(BF16)
            v_ref[0, k, 1, t * n_h:(t + 1) * n_h, :] = (-(c_re * p_im + c_im * p_re)).astype(BF16)


def _s5_operators(lam_re, lam_im, log_dt, b_re, b_im, c_re, c_im, d_skip, bsz):
    n_g, n_p = lam_re.shape
    n_h = S5_GROUP
    n_l = S5_CHUNK
    lh = n_l * n_h
    lr = lam_re.astype(F32)
    li = lam_im.astype(F32)
    dt = jnp.exp(log_dt.astype(F32))[:, None]
    mag = jnp.exp(lr * dt)
    ang = li * dt
    ab_re = mag * jnp.cos(ang)
    ab_im = mag * jnp.sin(ang)
    den = lr * lr + li * li
    nr = ab_re - 1.0
    f_re = (nr * lr + ab_im * li) / den
    f_im = (ab_im * lr - nr * li) / den
    br = b_re.astype(F32)
    bi = b_im.astype(F32)
    bb = jnp.stack([f_re[..., None] * br - f_im[..., None] * bi,
                    f_re[..., None] * bi + f_im[..., None] * br], axis=1)
    tau = jnp.arange(n_l + 1, dtype=F32)
    pw_re = jnp.exp((lr * dt)[..., None] * tau) * jnp.cos(ang[..., None] * tau)
    pw_im = jnp.exp((lr * dt)[..., None] * tau) * jnp.sin(ang[..., None] * tau)
    tau_m = tau[None, :, None]
    pt = jnp.stack([jnp.exp((lr * dt)[:, None, :] * tau_m) * jnp.cos(ang[:, None, :] * tau_m),
                    jnp.exp((lr * dt)[:, None, :] * tau_m) * jnp.sin(ang[:, None, :] * tau_m)],
                   axis=1)
    rev = jnp.stack([pw_re[:, :, n_l - 1::-1], pw_im[:, :, n_l - 1::-1]], axis=1)
    cc = jnp.stack([c_re.astype(F32), c_im.astype(F32)], axis=1)
    slot = (jnp.arange(n_g) % 2)[:, None, None, None]

    def lane_place(a):
        z = jnp.zeros_like(a)
        return jnp.where(slot == 0, jnp.concatenate([a, z], -1), jnp.concatenate([z, a], -1))

    skip = jnp.broadcast_to(d_skip.astype(F32).reshape(n_g, n_h, 1), (n_g, n_h, lh))
    n_pair = n_g // 2
    pair = lambda shape: pl.BlockSpec((2,) + shape, lambda j: (j,) + (0,) * len(shape))
    t_mat, w_pack, v_pack = pl.pallas_call(
        _s5_build_kernel,
        grid=(n_pair,),
        in_specs=[pair((2, n_p, n_l)), pair((2, n_p, n_h)), pair((2, n_h, n_p)),
                  pair((2, n_h, 2 * n_p)), pair((2, n_l + 1, 2 * n_p)), pair((n_h, lh))],
        out_specs=[
            pl.BlockSpec((2, lh, lh), lambda j: (j, 0, 0)),
            pl.BlockSpec((1, 2, 2, 2 * n_p, lh), lambda j: (j, 0, 0, 0, 0)),
            pl.BlockSpec((1, 2, 2, lh, 2 * n_p), lambda j: (j, 0, 0, 0, 0)),
        ],
        out_shape=[
            jax.ShapeDtypeStruct((n_g, lh, lh), BF16),
            jax.ShapeDtypeStruct((n_pair, 2, 2, 2 * n_p, lh), BF16),
            jax.ShapeDtypeStruct((n_pair, 2, 2, lh, 2 * n_p), BF16),
        ],
        compiler_params=_params("parallel"),
        name="s5_build_operators",
    )(rev, bb, cc, lane_place(cc), lane_place(pt), skip)

    def state_cols(a):
        a = jnp.broadcast_to(a.reshape(n_pair, 1, 2 * n_p), (n_pair, bsz, 2 * n_p))
        return a.reshape(1, n_g * bsz * n_p)
    return t_mat, w_pack, v_pack, state_cols(pw_re[:, :, n_l]), state_cols(pw_im[:, :, n_l])


def _chunk_view(x, bsz, seq):
    return x.reshape(bsz, seq // S5_CHUNK, S5_CHUNK, x.shape[1])


def _position_slab_copy(x_hbm, slab_ref, sem, b, pos, k):
    return pltpu.make_async_copy(x_hbm.at[b, :, pos, :], slab_ref.at[k], sem.at[k])


def _norm_matmul_t_kernel(x_hbm, g_ref, wt_ref, o_ref, slab_ref, h_ref, sem):
    b = pl.program_id(0)
    s = pl.program_id(1)
    sq, nc, _ = slab_ref.shape

    @pl.when(pl.program_id(2) == 0)
    def _():
        copies = [_position_slab_copy(x_hbm, slab_ref, sem, b, s * sq + k, k) for k in range(sq)]
        for c in copies:
            c.start()
        for k, c in enumerate(copies):
            c.wait()
            h_ref[k * nc:(k + 1) * nc, :] = _rms(slab_ref[k], g_ref[...]).astype(BF16)

    o_ref[...] = _dot_nt(wt_ref[...], h_ref[...]).astype(o_ref.dtype)


def norm_matmul_t(x, g, wt, layer, *, bsz, seq, tn=1024):
    d = x.shape[1]
    n = wt.shape[1]
    tn = min(tn, n)
    nc = seq // S5_CHUNK
    sq = S5_POS_PER_STEP
    steps = S5_CHUNK // sq
    return pl.pallas_call(
        _norm_matmul_t_kernel,
        grid=(bsz, steps, n // tn),
        in_specs=[
            pl.BlockSpec(memory_space=pl.ANY),
            pl.BlockSpec((1, d), lambda b, s, j: (0, 0)),
            _layer_spec(layer, (tn, d), lambda b, s, j: (j, 0)),
        ],
        out_specs=pl.BlockSpec((tn, sq * nc), lambda b, s, j: (j, b * steps + s)),
        out_shape=jax.ShapeDtypeStruct((n, bsz * seq), BF16),
        scratch_shapes=[
            pltpu.VMEM((sq, nc, d), x.dtype),
            pltpu.VMEM((sq * nc, d), BF16),
            pltpu.SemaphoreType.DMA((sq,)),
        ],
        compiler_params=_params("arbitrary", "arbitrary", "arbitrary"),
        name="norm_matmul_t",
    )(_chunk_view(x, bsz, seq), g.reshape(1, d), wt)


def _chunk_operand(u_ref, k, nc):
    h = S5_GROUP
    return jnp.concatenate(
        [u_ref[k * h:(k + 1) * h, s * nc:(s + 1) * nc] for s in range(S5_CHUNK)], axis=0)


def _s5_inject_kernel(u_ref, w_ref, ore_ref, oim_ref, *, bsz, seq):
    nc = ore_ref.shape[0]
    p2 = w_ref.shape[3]
    for b in range(bsz):
        ub = u_ref.at[:, b * seq:(b + 1) * seq]
        u0 = _chunk_operand(ub, 0, nc)
        u1 = _chunk_operand(ub, 1, nc)
        cols = slice(b * p2, (b + 1) * p2)
        ore_ref[:, cols] = (_dot(w_ref[0, 0, 0], u0) + _dot(w_ref[0, 1, 0], u1)).T
        oim_ref[:, cols] = (_dot(w_ref[0, 0, 1], u0) + _dot(w_ref[0, 1, 1], u1)).T


def s5_inject(ut, w_pack, layer, *, bsz, seq):
    n_pair = w_pack.shape[1]
    p2 = w_pack.shape[4]
    nc = seq // S5_CHUNK
    out = jax.ShapeDtypeStruct((nc, n_pair * bsz * p2), F32)
    oblk = pl.BlockSpec((nc, bsz * p2), lambda j: (0, j))
    return pl.pallas_call(
        functools.partial(_s5_inject_kernel, bsz=bsz, seq=seq),
        grid=(n_pair,),
        in_specs=[
            pl.BlockSpec((2 * S5_GROUP, bsz * seq), lambda j: (j, 0)),
            _layer_spec(layer, (1,) + w_pack.shape[2:], lambda j: (j, 0, 0, 0, 0)),
        ],
        out_specs=[oblk, oblk],
        out_shape=[out, out],
        compiler_params=_params("parallel"),
        name="s5_inject",
    )(ut, w_pack)


def _s5_scan_kernel(wre_ref, wim_ref, are_ref, aim_ref, xre_ref, xim_ref, *, n_c):
    a_re = are_ref[...]
    a_im = aim_ref[...]

    def body(c, carry):
        s_re, s_im = carry
        row = pl.ds(c, 1)
        xre_ref[row, :] = s_re
        xim_ref[row, :] = s_im
        n_re = a_re * s_re - a_im * s_im + wre_ref[row, :]
        n_im = a_re * s_im + a_im * s_re + wim_ref[row, :]
        return n_re, n_im

    zero = jnp.zeros(a_re.shape, F32)
    lax.fori_loop(0, n_c, body, (zero, zero))


def s5_scan(w_re, w_im, a_re, a_im, layer, *, tl=2048):
    n_c, cols = w_re.shape
    tl = min(tl, cols)
    out = jax.ShapeDtypeStruct((n_c, cols), F32)
    blk = pl.BlockSpec((n_c, tl), lambda j: (0, j))
    coef = _layer_spec(layer, (1, tl), lambda j: (0, j))
    return pl.pallas_call(
        functools.partial(_s5_scan_kernel, n_c=n_c),
        grid=(cols // tl,),
        in_specs=[blk, blk, coef, coef],
        out_specs=[blk, blk],
        out_shape=[out, out],
        compiler_params=_params("parallel"),
        name="s5_scan",
    )(w_re, w_im, a_re, a_im)


def _gelu_tanh(y):
    c = np.float32(np.sqrt(2.0 / np.pi))
    return 0.5 * y * (1.0 + jnp.tanh(c * (y + 0.044715 * (y * y * y))))


def _s5_out_kernel(u_ref, t_ref, xre_ref, xim_ref, v_ref, o_ref, *, bsz, seq):
    nc = xre_ref.shape[0]
    p2 = v_ref.shape[-1]
    h = S5_GROUP
    for b in range(bsz):
        ub = u_ref.at[:, b * seq:(b + 1) * seq]
        x_re = xre_ref[:, b * p2:(b + 1) * p2].astype(BF16)
        x_im = xim_ref[:, b * p2:(b + 1) * p2].astype(BF16)
        for k in range(2):
            y = (_dot(t_ref[k], _chunk_operand(ub, k, nc))
                 + _dot_nt(v_ref[0, k, 0], x_re) + _dot_nt(v_ref[0, k, 1], x_im))
            y = _gelu_tanh(y).astype(o_ref.dtype)
            for t in range(S5_CHUNK):
                lanes = slice(b * seq + t * nc, b * seq + (t + 1) * nc)
                o_ref[k * h:(k + 1) * h, lanes] = y[t * h:(t + 1) * h, :]


def s5_out(ut, t_mat, x_re, x_im, v_pack, layer, *, bsz, seq):
    n_pair = v_pack.shape[1]
    p2 = v_pack.shape[-1]
    nc = seq // S5_CHUNK
    lh = t_mat.shape[-1]
    xblk = pl.BlockSpec((nc, bsz * p2), lambda j: (0, j))
    ublk = pl.BlockSpec((2 * S5_GROUP, bsz * seq), lambda j: (j, 0))
    return pl.pallas_call(
        functools.partial(_s5_out_kernel, bsz=bsz, seq=seq),
        grid=(n_pair,),
        in_specs=[
            ublk,
            _layer_spec(layer, (2, lh, lh), lambda j: (j, 0, 0)),
            xblk,
            xblk,
            _layer_spec(layer, (1,) + v_pack.shape[2:], lambda j: (j, 0, 0, 0, 0)),
        ],
        out_specs=ublk,
        out_shape=jax.ShapeDtypeStruct(ut.shape, BF16),
        compiler_params=_params("parallel"),
        name="s5_out",
    )(ut, t_mat, x_re, x_im, v_pack)


def s5_mixer(ut, bsz, seq, ops, layer):
    t_mat, w_pack, v_pack, a_re, a_im = ops
    w_re, w_im = s5_inject(ut, w_pack, layer, bsz=bsz, seq=seq)
    x_re, x_im = s5_scan(w_re, w_im, a_re, a_im, layer)
    return s5_out(ut, t_mat, x_re, x_im, v_pack, layer, bsz=bsz, seq=seq)


def _glu_kernel(yt_ref, wv_ref, wg_ref, g_ref, o_ref, y_ref, m_ref, *, nj, tn):
    j = pl.program_id(2)
    nc = o_ref.shape[0]

    @pl.when(j == 0)
    def _():
        for k in range(S5_POS_PER_STEP):
            y_ref[k * nc:(k + 1) * nc, :] = (
                yt_ref[:, k * nc:(k + 1) * nc].astype(F32).T.astype(BF16))

    y = y_ref[...]
    m_ref[j] = _dot(y, wv_ref[...]) * _sigmoid(_dot(y, wg_ref[...]))

    @pl.when(j == nj - 1)
    def _():
        d = g_ref.shape[1]
        ss = None
        for jj in range(nj):
            mj = m_ref[jj]
            s = jnp.sum(mj * mj, axis=-1, keepdims=True)
            ss = s if ss is None else ss + s
        inv = lax.rsqrt(ss / d + EPS)
        for k in range(S5_POS_PER_STEP):
            rows = slice(k * nc, (k + 1) * nc)
            for jj in range(nj):
                cols = slice(jj * tn, (jj + 1) * tn)
                o_ref[:, k, cols] = m_ref[jj, rows, :] * inv[rows] * g_ref[:, cols]


def glu_norm(yt, w_glu, layer, g, *, bsz, seq, tn=GLU_COL_TILE):
    k, _ = yt.shape
    d = w_glu.shape[2] // 2
    tn = min(tn, d)
    nj = d // tn
    nc = seq // S5_CHUNK
    sq = S5_POS_PER_STEP
    steps = S5_CHUNK // sq
    out = pl.pallas_call(
        functools.partial(_glu_kernel, nj=nj, tn=tn),
        grid=(bsz, steps, nj),
        in_specs=[
            pl.BlockSpec((k, sq * nc), lambda b, s, j: (0, b * steps + s)),
            _layer_spec(layer, (k, tn), lambda b, s, j: (0, j)),
            _layer_spec(layer, (k, tn), lambda b, s, j: (0, j + nj)),
            pl.BlockSpec((1, d), lambda b, s, j: (0, 0)),
        ],
        out_specs=pl.BlockSpec((None, nc, sq, d), lambda b, s, j: (b, 0, s, 0)),
        out_shape=jax.ShapeDtypeStruct((bsz, nc, S5_CHUNK, d), F32),
        scratch_shapes=[pltpu.VMEM((sq * nc, k), BF16), pltpu.VMEM((nj, sq * nc, tn), F32)],
        compiler_params=_params("parallel", "parallel", "arbitrary"),
        name="glu_norm",
    )(yt, w_glu, w_glu, g.reshape(1, d))
    return out.reshape(bsz * seq, d)


def _band_bias(rel_bias):
    c = ATT_CHUNK
    nq = ATT_QGROUP * c
    nk = (ATT_QGROUP + ATT_LEFT) * c
    r = rel_bias.astype(F32)
    n_h = r.shape[0]
    d_lo = ATT_LEFT * c - (nk - 1)
    d_hi = ATT_LEFT * c + nq - 1
    n_lo = -(c - 1) - d_lo
    n_hi = d_hi - ATT_MAX_REL
    e = jnp.concatenate([jnp.broadcast_to(r[:, :1], (n_h, n_lo)), r,
                         jnp.broadcast_to(r[:, -1:], (n_h, n_hi))], axis=1)
    width = nq + nk - 1
    assert e.shape[1] == width
    frev = jnp.concatenate([e[:, ::-1], jnp.zeros((n_h, 1), F32)], axis=1)
    skew = jnp.tile(frev, (1, nq))[:, :nq * width].reshape(n_h, nq, width)
    bias = skew[:, :, nq - 1:nq - 1 + nk]
    qa = np.arange(nq)[:, None] // c
    kc = np.arange(nk)[None, :] // c
    valid = (kc >= qa) & (kc <= qa + ATT_LEFT)
    return jnp.where(jnp.asarray(valid)[None], bias, NEG_INF)


def _band_attn_kernel(q_ref, k_ref, v_ref, b_ref, o_ref, s_ref, p_ref, v1_ref, *, seq):
    c = ATT_CHUNK
    dh = ATT_HEAD_DIM
    nq = ATT_QGROUP * c
    nk = (ATT_QGROUP + ATT_LEFT) * c
    v1_ref[:, :dh] = v_ref[...]
    v1_ref[:, dh:] = jnp.ones((seq, dh), BF16)
    for gi in range(seq // nq):
        q0 = gi * nq
        off = max(0, ATT_LEFT * c - q0)
        k0 = q0 - ATT_LEFT * c + off
        w = nk - off
        s_ref[:, :w] = _dot_nt(q_ref[q0:q0 + nq, :], k_ref[k0:k0 + w, :])
        for a in range(ATT_QGROUP):
            rows = slice(a * c, (a + 1) * c)
            lo = max((a * c) // LANES * LANES, off)
            hi = min(-(-((a + ATT_LEFT + 1) * c) // LANES) * LANES, nk)
            s = s_ref[rows, lo - off:hi - off] + b_ref[0, rows, lo:hi]
            m = jnp.max(s, axis=-1, keepdims=True)
            p_ref[rows, lo - off:hi - off] = jnp.exp(s - m).astype(BF16)
            if lo > off:
                p_ref[rows, 0:lo - off] = jnp.zeros((c, lo - off), BF16)
            if hi < nk:
                p_ref[rows, hi - off:w] = jnp.zeros((c, nk - hi), BF16)
        o2 = _dot(p_ref[:, :w], v1_ref[k0:k0 + w, :])
        o_ref[q0:q0 + nq, :] = (o2[:, :dh] / o2[:, dh:]).astype(o_ref.dtype)


def band_attention(q, kv, bias, layer, *, bsz, seq):
    m, d = q.shape
    n_h = d // ATT_HEAD_DIM
    dh = ATT_HEAD_DIM
    nq, nk = bias.shape[2:]
    return pl.pallas_call(
        functools.partial(_band_attn_kernel, seq=seq),
        grid=(bsz, n_h),
        in_specs=[
            pl.BlockSpec((seq, dh), lambda b, h: (b, h)),
            pl.BlockSpec((seq, dh), lambda b, h: (b, h)),
            pl.BlockSpec((seq, dh), lambda b, h: (b, h + n_h)),
            _layer_spec(layer, (1, nq, nk), lambda b, h: (h, 0, 0)),
        ],
        out_specs=pl.BlockSpec((seq, dh), lambda b, h: (b, h)),
        out_shape=jax.ShapeDtypeStruct((m, d), BF16),
        scratch_shapes=[
            pltpu.VMEM((nq, nk), F32),
            pltpu.VMEM((nq, nk), BF16),
            pltpu.VMEM((seq, 2 * dh), BF16),
        ],
        compiler_params=_params("parallel", "parallel"),
        name="band_attention",
    )(q, kv, kv, bias)


def _mem_attn_kernel(*refs, has_update):
    if has_update:
        x_ref, r_ref, g1_ref, wq_ref, k_ref, v_ref, wo_ref, g2_ref, o_ref = refs
        x = x_ref[...] + r_ref[...]
    else:
        x_ref, g1_ref, wq_ref, k_ref, v_ref, wo_ref, g2_ref, o_ref = refs
        x = x_ref[...]
    h = _rms(x, g1_ref[...]).astype(BF16)
    q = _dot(h, wq_ref[...]).astype(BF16)
    k = k_ref[...]
    v = v_ref[...]
    dh = MEM_HEAD_DIM
    n_heads = q.shape[1] // dh
    scores = [_dot_nt(q[:, hd * dh:(hd + 1) * dh], k[:, hd * dh:(hd + 1) * dh])
              for hd in range(n_heads)]
    ones = jnp.ones((v.shape[0], dh), BF16)
    heads = []
    for hd in range(n_heads):
        s = scores[hd]
        p = jnp.exp(s - jnp.max(s, axis=-1, keepdims=True)).astype(BF16)
        o2 = _dot(p, jnp.concatenate([v[:, hd * dh:(hd + 1) * dh], ones], axis=-1))
        heads.append((o2[:, :dh] / o2[:, dh:]).astype(BF16))
    o = jnp.concatenate(heads, axis=-1)
    c = _dot(o, wo_ref[...])
    o_ref[...] = x + _rms(c, g2_ref[...])


def mem_attention(x, g1, wq, kvmem, layer, wo, g2, *, seq, n_mem, update=None, tm=ROW_TILE):
    m, d = x.shape
    md = wq.shape[2]
    tiles_per_batch = seq // tm
    row = pl.BlockSpec((tm, d), lambda i: (i, 0))
    acts = [x] if update is None else [x, update]
    return pl.pallas_call(
        functools.partial(_mem_attn_kernel, has_update=update is not None),
        grid=(m // tm,),
        in_specs=[row] * len(acts) + [
            pl.BlockSpec((1, d), lambda i: (0, 0)),
            _layer_spec(layer, (d, md), lambda i: (0, 0)),
            pl.BlockSpec((n_mem, md), lambda i: (i // tiles_per_batch, 2 * layer)),
            pl.BlockSpec((n_mem, md), lambda i: (i // tiles_per_batch, 2 * layer + 1)),
            _layer_spec(layer, (md, d), lambda i: (0, 0)),
            pl.BlockSpec((1, d), lambda i: (0, 0)),
        ],
        out_specs=row,
        out_shape=jax.ShapeDtypeStruct((m, d), F32),
        compiler_params=_params("parallel"),
        name="mem_attention",
    )(*acts, g1.reshape(1, d), wq, kvmem, kvmem, wo, g2.reshape(1, d))


def _conv_ffn_kernel(x_ref, g1_ref, wuv_ref, wug_ref, cwv_ref, cwg_ref, cbv_ref, cbg_ref, wd_ref,
                     g2_ref, o_ref, h_ref, acc_ref, tail_ref, up_ref, wd_bf_ref, *, nj,
                     tiles_per_batch):
    i = pl.program_id(0)
    j = pl.program_id(1)
    tm = x_ref.shape[0]
    hm = tm // FFN_ROW_SPLIT

    @pl.when(j == 0)
    def _():
        h_ref[...] = _rms(x_ref[...], g1_ref[...]).astype(h_ref.dtype)
        acc_ref[...] = jnp.zeros(acc_ref.shape, F32)

    @pl.when(i % tiles_per_batch == 0)
    def _():
        tail_ref[j] = jnp.zeros(tail_ref.shape[1:], F32)

    up_ref[:, 0:SUBLANES, :] = tail_ref[j]
    wd_bf_ref[...] = wd_ref[...].astype(BF16)

    for r in range(FFN_ROW_SPLIT):
        h = h_ref[r * hm:(r + 1) * hm, :]
        rows = slice(SUBLANES + r * hm, SUBLANES + (r + 1) * hm)
        up_ref[0, rows, :] = _dot(h, wuv_ref[...])
        up_ref[1, rows, :] = _dot(h, wug_ref[...])

    tail_ref[j] = up_ref[:, tm:tm + SUBLANES, :]

    def conv(slot, base, cw, cb):
        taps = [up_ref[slot, base - (CONV_W - 1 - k):base - (CONV_W - 1 - k) + hm, :]
                for k in range(CONV_W)]
        return cb + taps[0] * cw[0:1, :] + taps[1] * cw[1:2, :] + taps[2] * cw[2:3, :]

    for r in range(FFN_ROW_SPLIT):
        base = SUBLANES + r * hm
        val = conv(0, base, cwv_ref[...], cbv_ref[...])
        gate = conv(1, base, cwg_ref[...], cbg_ref[...])
        act = (val * (gate * _sigmoid(gate))).astype(BF16)
        acc_ref[r * hm:(r + 1) * hm, :] += _dot(act, wd_bf_ref[...])

    @pl.when(j == nj - 1)
    def _():
        o_ref[...] = x_ref[...] + _rms(acc_ref[...], g2_ref[...])


def conv_ffn(x, g1, w_up, conv_w, conv_b, w_down, layer, g2, *, seq, tm=ROW_TILE, tf=FFN_COL_TILE):
    m, d = x.shape
    f = w_down.shape[1]
    nj = f // tf
    tiles_per_batch = seq // tm
    conv_b = conv_b.reshape(conv_b.shape[0], 1, 2 * f)
    return pl.pallas_call(
        functools.partial(_conv_ffn_kernel, nj=nj, tiles_per_batch=tiles_per_batch),
        grid=(m // tm, nj),
        in_specs=[
            pl.BlockSpec((tm, d), lambda i, j: (i, 0)),
            pl.BlockSpec((1, d), lambda i, j: (0, 0)),
            _layer_spec(layer, (d, tf), lambda i, j: (0, j)),
            _layer_spec(layer, (d, tf), lambda i, j: (0, j + nj)),
            _layer_spec(layer, (CONV_W, tf), lambda i, j: (0, j)),
            _layer_spec(layer, (CONV_W, tf), lambda i, j: (0, j + nj)),
            _layer_spec(layer, (1, tf), lambda i, j: (0, j)),
            _layer_spec(layer, (1, tf), lambda i, j: (0, j + nj)),
            _layer_spec(layer, (tf, d), lambda i, j: (j, 0)),
            pl.BlockSpec((1, d), lambda i, j: (0, 0)),
        ],
        out_specs=pl.BlockSpec((tm, d), lambda i, j: (i, 0)),
        out_shape=jax.ShapeDtypeStruct((m, d), F32),
        scratch_shapes=[
            pltpu.VMEM((tm, d), BF16),
            pltpu.VMEM((tm, d), F32),
            pltpu.VMEM((nj, 2, SUBLANES, tf), F32),
            pltpu.VMEM((2, SUBLANES + tm, tf), F32),
            pltpu.VMEM((tf, d), BF16),
        ],
        compiler_params=_params("arbitrary", "arbitrary"),
        name="conv_ffn",
    )(x, g1.reshape(1, d), w_up, w_up, conv_w, conv_w, conv_b, conv_b, w_down, g2.reshape(1, d))


def kernel(x, mem, norm_mix, norm_mem, norm_ffn, mem_in_norm, a_w_in, a_lam_re, a_lam_im, a_log_dt,
           a_b_re, a_b_im, a_c_re, a_c_im, a_d, a_w_glu, kv_norm, w_k, w_v, b_w_q, b_rel_bias, b_w_o,
           m_w_q, m_w_kv, m_w_o, f_w_up, f_conv_w, f_conv_b, f_w_down):
    bsz, seq, d = x.shape
    depth = norm_mix.shape[0]
    n_a = a_w_in.shape[0]
    n_mem = mem.shape[1]
    md = m_w_q.shape[2]
    assert seq % ROW_TILE == 0 and seq % (ATT_QGROUP * ATT_CHUNK) == 0
    assert (seq // S5_CHUNK) % LANES == 0 and (d // S5_GROUP) % 2 == 0

    xf = x.reshape(bsz * seq, d)

    w_kv_mem = jnp.transpose(m_w_kv, (1, 0, 2)).reshape(1, d, depth * 2 * md).astype(BF16)
    w_kv = jnp.concatenate([w_k, w_v], axis=1).astype(BF16)[None]
    a_w_in_t = jnp.transpose(a_w_in, (0, 2, 1)).astype(BF16)
    a_w_glu_b = a_w_glu.astype(BF16)
    b_w_q_b = (b_w_q * (ATT_HEAD_DIM ** -0.5)).astype(BF16)
    b_w_o_b = b_w_o.astype(BF16)
    m_w_q_b = (m_w_q * (MEM_HEAD_DIM ** -0.5)).astype(BF16)
    m_w_o_b = m_w_o.astype(BF16)
    f_w_up_b = f_w_up.astype(BF16)
    flat = lambda a: a.reshape((-1,) + a.shape[2:])
    s5_ops = _s5_operators(flat(a_lam_re), flat(a_lam_im), flat(a_log_dt), flat(a_b_re),
                           flat(a_b_im), flat(a_c_re), flat(a_c_im), a_d, bsz)
    s5_ops = [o.reshape((n_a, o.shape[0] // n_a) + o.shape[1:]) for o in s5_ops[:3]] + [
        o.reshape(n_a, 1, -1) for o in s5_ops[3:]]
    att_bias = jax.vmap(_band_bias)(b_rel_bias)

    kvmem = norm_matmul(mem.reshape(bsz * n_mem, d), mem_in_norm, w_kv_mem, 0)

    kv = None
    for l in range(depth):
        update = None
        if l == n_a:
            kv = norm_matmul(xf, kv_norm, w_kv, 0)
        if l < n_a:
            ut = norm_matmul_t(xf, norm_mix[l, 0], a_w_in_t, l, bsz=bsz, seq=seq)
            yt = s5_mixer(ut, bsz, seq, s5_ops, l)
            update = glu_norm(yt, a_w_glu_b, l, norm_mix[l, 1], bsz=bsz, seq=seq)
        else:
            jb = l - n_a
            q = norm_matmul(xf, norm_mix[l, 0], b_w_q_b, jb)
            o = band_attention(q, kv, att_bias, jb, bsz=bsz, seq=seq)
            xf = proj_res(o, b_w_o_b, jb, xf, norm_mix[l, 1])
        xf = mem_attention(xf, norm_mem[l, 0], m_w_q_b, kvmem, l, m_w_o_b, norm_mem[l, 1],
                           seq=seq, n_mem=n_mem, update=update)
        xf = conv_ffn(xf, norm_ffn[l, 0], f_w_up_b, f_conv_w, f_conv_b, f_w_down, l,
                      norm_ffn[l, 1], seq=seq)
    return xf.reshape(bsz, seq, d)
```

```python
import functools

import numpy as np
import jax
import jax.numpy as jnp
from jax import lax
from jax.experimental import pallas as pl
from jax.experimental.pallas import tpu as pltpu

EPS = 1e-6
NEG_INF = -1e30
BF16 = jnp.bfloat16
F32 = jnp.float32

S5_GROUP = 16
S5_CHUNK = 16
S5_POS_PER_STEP = 8
ATT_CHUNK = 64
ATT_LEFT = 8
ATT_HEAD_DIM = 128
ATT_MAX_REL = 256
ATT_QGROUP = 4
MEM_HEAD_DIM = 128
CONV_W = 3

VMEM_LIMIT_BYTES = 56 * 1024 * 1024
ROW_TILE = 512
COL_TILE = 2048
FFN_ROW_SPLIT = 2
FFN_COL_TILE = 512
GLU_COL_TILE = 512
SUBLANES = 8
LANES = 128

_NT = (((1,), (1,)), ((), ()))


def _params(*sem):
    return pltpu.CompilerParams(dimension_semantics=sem, vmem_limit_bytes=VMEM_LIMIT_BYTES)


def _rms(x, g):
    ms = jnp.mean(x * x, axis=-1, keepdims=True)
    return x * lax.rsqrt(ms + EPS) * g


def _sigmoid(x):
    return 1.0 / (1.0 + jnp.exp(-x))


def _dot(a, b):
    return jnp.dot(a, b, preferred_element_type=F32)


def _dot_nt(a, b):
    return lax.dot_general(a, b, _NT, preferred_element_type=F32)


def _layer_spec(layer, block, index_map):
    return pl.BlockSpec((None,) + tuple(block), lambda *ids: (layer,) + tuple(index_map(*ids)))


def _norm_matmul_kernel(x_ref, g_ref, w_ref, o_ref, h_ref):
    @pl.when(pl.program_id(1) == 0)
    def _():
        h_ref[...] = _rms(x_ref[...], g_ref[...]).astype(h_ref.dtype)

    o_ref[...] = _dot(h_ref[...], w_ref[...]).astype(o_ref.dtype)


def norm_matmul(x, g, w, layer, *, tm=ROW_TILE, tn=COL_TILE, out_dtype=BF16):
    m, k = x.shape
    if layer is None:
        tn = w.shape[2]
        n = w.shape[0] * tn
        w_spec = pl.BlockSpec((None, k, tn), lambda i, j: (j, 0, 0))
    else:
        n = w.shape[2]
        tn = min(tn, n)
        w_spec = _layer_spec(layer, (k, tn), lambda i, j: (0, j))
    return pl.pallas_call(
        _norm_matmul_kernel,
        grid=(m // tm, n // tn),
        in_specs=[
            pl.BlockSpec((tm, k), lambda i, j: (i, 0)),
            pl.BlockSpec((1, k), lambda i, j: (0, 0)),
            w_spec,
        ],
        out_specs=pl.BlockSpec((tm, tn), lambda i, j: (i, j)),
        out_shape=jax.ShapeDtypeStruct((m, n), out_dtype),
        scratch_shapes=[pltpu.VMEM((tm, k), BF16)],
        compiler_params=_params("parallel", "arbitrary"),
        name="norm_matmul",
    )(x, g.reshape(1, k), w)


def _proj_res_kernel(a_ref, w_ref, x_ref, g_ref, o_ref):
    f = _dot(a_ref[...], w_ref[...])
    o_ref[...] = x_ref[...] + _rms(f, g_ref[...])


def proj_res(a, w, layer, x, g, *, tm=ROW_TILE):
    m, k = a.shape
    d = w.shape[2]
    return pl.pallas_call(
        _proj_res_kernel,
        grid=(m // tm,),
        in_specs=[
            pl.BlockSpec((tm, k), lambda i: (i, 0)),
            _layer_spec(layer, (k, d), lambda i: (0, 0)),
            pl.BlockSpec((tm, d), lambda i: (i, 0)),
            pl.BlockSpec((1, d), lambda i: (0, 0)),
        ],
        out_specs=pl.BlockSpec((tm, d), lambda i: (i, 0)),
        out_shape=jax.ShapeDtypeStruct((m, d), F32),
        compiler_params=_params("parallel"),
        name="proj_res",
    )(a, w, x, g.reshape(1, d))


def _s5_build_kernel(rev_ref, bb_ref, cc_ref, cpad_ref, pt_ref, d_ref, t_ref, w_ref, v_ref):
    n_l = S5_CHUNK
    n_h = S5_GROUP
    lh = n_l * n_h
    n_p = rev_ref.shape[2]

    def exact_dot(a, b):
        return jnp.dot(a, b, precision=lax.Precision.HIGHEST, preferred_element_type=F32)

    def spread(a, pattern):
        hi = a.astype(BF16)
        rest = a - hi.astype(F32)
        mid = rest.astype(BF16)
        lo = (rest - mid.astype(F32)).astype(BF16)
        return _dot(hi, pattern) + _dot(mid, pattern) + _dot(lo, pattern)

    lane = lax.broadcasted_iota(jnp.int32, (n_l, lh), 1)
    row = lax.broadcasted_iota(jnp.int32, (n_l, lh), 0)
    per_step = jnp.where(lane // n_h == row, 1.0, 0.0).astype(BF16)
    per_chan = jnp.where(lane % n_h == row, 1.0, 0.0).astype(BF16)
    sub = lax.broadcasted_iota(jnp.int32, (n_h, lh), 0)
    lan = lax.broadcasted_iota(jnp.int32, (n_h, lh), 1)
    no_state = jnp.zeros((n_p, lh), BF16)
    for k in range(2):
        r_re = spread(rev_ref[k, 0], per_step)
        r_im = spread(rev_ref[k, 1], per_step)
        b_re = spread(bb_ref[k, 0], per_chan)
        b_im = spread(bb_ref[k, 1], per_chan)
        w_re = r_re * b_re - r_im * b_im
        w_im = r_re * b_im + r_im * b_re
        for ri, w in enumerate((w_re, w_im)):
            for slot in range(2):
                w_ref[0, k, ri, slot * n_p:(slot + 1) * n_p, :] = (
                    w.astype(BF16) if slot == k else no_state)
        krev = exact_dot(cc_ref[k, 0], w_re) - exact_dot(cc_ref[k, 1], w_im)
        skip = d_ref[k]
        for t in range(n_l):
            sh = (n_l - 1 - t) * n_h
            slab = krev
            if sh:
                slab = jnp.where(lan < lh - sh, pltpu.roll(krev, lh - sh, axis=1), 0.0)
            slab = slab + jnp.where(lan == t * n_h + sub, skip, 0.0)
            t_ref[k, t * n_h:(t + 1) * n_h, :] = slab.astype(BF16)
        c_re = cpad_ref[k, 0]
        c_im = cpad_ref[k, 1]
        for t in range(n_l):
            p_re = pt_ref[k, 0, t + 1:t + 2, :]
            p_im = pt_ref[k, 1, t + 1:t + 2, :]
            v_ref[0, k, 0, t * n_h:(t + 1) * n_h, :] = (c_re * p_re - c_im * p_im).astype(BF16)
            v_ref[0, k, 1, t * n_h:(t + 1) * n_h, :] = (-(c_re * p_im + c_im * p_re)).astype(BF16)


def _s5_operators(lam_re, lam_im, log_dt, b_re, b_im, c_re, c_im, d_skip, bsz):
    n_g, n_p = lam_re.shape
    n_h = S5_GROUP
    n_l = S5_CHUNK
    lh = n_l * n_h
    lr = lam_re.astype(F32)
    li = lam_im.astype(F32)
    dt = jnp.exp(log_dt.astype(F32))[:, None]
    mag = jnp.exp(lr * dt)
    ang = li * dt
    ab_re = mag * jnp.cos(ang)
    ab_im = mag * jnp.sin(ang)
    den = lr * lr + li * li
    nr = ab_re - 1.0
    f_re = (nr * lr + ab_im * li) / den
    f_im = (ab_im * lr - nr * li) / den
    br = b_re.astype(F32)
    bi = b_im.astype(F32)
    bb = jnp.stack([f_re[..., None] * br - f_im[..., None] * bi,
                    f_re[..., None] * bi + f_im[..., None] * br], axis=1)
    tau = jnp.arange(n_l + 1, dtype=F32)
    pw_re = jnp.exp((lr * dt)[..., None] * tau) * jnp.cos(ang[..., None] * tau)
    pw_im = jnp.exp((lr * dt)[..., None] * tau) * jnp.sin(ang[..., None] * tau)
    tau_m = tau[None, :, None]
    pt = jnp.stack([jnp.exp((lr * dt)[:, None, :] * tau_m) * jnp.cos(ang[:, None, :] * tau_m),
                    jnp.exp((lr * dt)[:, None, :] * tau_m) * jnp.sin(ang[:, None, :] * tau_m)],
                   axis=1)
    rev = jnp.stack([pw_re[:, :, n_l - 1::-1], pw_im[:, :, n_l - 1::-1]], axis=1)
    cc = jnp.stack([c_re.astype(F32), c_im.astype(F32)], axis=1)
    slot = (jnp.arange(n_g) % 2)[:, None, None, None]

    def lane_place(a):
        z = jnp.zeros_like(a)
        return jnp.where(slot == 0, jnp.concatenate([a, z], -1), jnp.concatenate([z, a], -1))

    skip = jnp.broadcast_to(d_skip.astype(F32).reshape(n_g, n_h, 1), (n_g, n_h, lh))
    n_pair = n_g // 2
    pair = lambda shape: pl.BlockSpec((2,) + shape, lambda j: (j,) + (0,) * len(shape))
    t_mat, w_pack, v_pack = pl.pallas_call(
        _s5_build_kernel,
        grid=(n_pair,),
        in_specs=[pair((2, n_p, n_l)), pair((2, n_p, n_h)), pair((2, n_h, n_p)),
                  pair((2, n_h, 2 * n_p)), pair((2, n_l + 1, 2 * n_p)), pair((n_h, lh))],
        out_specs=[
            pl.BlockSpec((2, lh, lh), lambda j: (j, 0, 0)),
            pl.BlockSpec((1, 2, 2, 2 * n_p, lh), lambda j: (j, 0, 0, 0, 0)),
            pl.BlockSpec((1, 2, 2, lh, 2 * n_p), lambda j: (j, 0, 0, 0, 0)),
        ],
        out_shape=[
            jax.ShapeDtypeStruct((n_g, lh, lh), BF16),
            jax.ShapeDtypeStruct((n_pair, 2, 2, 2 * n_p, lh), BF16),
            jax.ShapeDtypeStruct((n_pair, 2, 2, lh, 2 * n_p), BF16),
        ],
        compiler_params=_params("parallel"),
        name="s5_build_operators",
    )(rev, bb, cc, lane_place(cc), lane_place(pt), skip)

    def state_cols(a):
        a = jnp.broadcast_to(a.reshape(n_pair, 1, 2 * n_p), (n_pair, bsz, 2 * n_p))
        return a.reshape(1, n_g * bsz * n_p)
    return t_mat, w_pack, v_pack, state_cols(pw_re[:, :, n_l]), state_cols(pw_im[:, :, n_l])


def _chunk_view(x, bsz, seq):
    return x.reshape(bsz, seq // S5_CHUNK, S5_CHUNK, x.shape[1])


def _position_slab_copy(x_hbm, slab_ref, sem, b, pos, k):
    return pltpu.make_async_copy(x_hbm.at[b, :, pos, :], slab_ref.at[k], sem.at[k])


def _norm_matmul_t_kernel(x_hbm, g_ref, wt_ref, o_ref, slab_ref, h_ref, sem):
    b = pl.program_id(0)
    s = pl.program_id(1)
    sq, nc, _ = slab_ref.shape

    @pl.when(pl.program_id(2) == 0)
    def _():
        copies = [_position_slab_copy(x_hbm, slab_ref, sem, b, s * sq + k, k) for k in range(sq)]
        for c in copies:
            c.start()
        for k, c in enumerate(copies):
            c.wait()
            h_ref[k * nc:(k + 1) * nc, :] = _rms(slab_ref[k], g_ref[...]).astype(BF16)

    o_ref[...] = _dot_nt(wt_ref[...], h_ref[...]).astype(o_ref.dtype)


def norm_matmul_t(x, g, wt, layer, *, bsz, seq, tn=1024):
    d = x.shape[1]
    n = wt.shape[1]
    tn = min(tn, n)
    nc = seq // S5_CHUNK
    sq = S5_POS_PER_STEP
    steps = S5_CHUNK // sq
    return pl.pallas_call(
        _norm_matmul_t_kernel,
        grid=(bsz, steps, n // tn),
        in_specs=[
            pl.BlockSpec(memory_space=pl.ANY),
            pl.BlockSpec((1, d), lambda b, s, j: (0, 0)),
            _layer_spec(layer, (tn, d), lambda b, s, j: (j, 0)),
        ],
        out_specs=pl.BlockSpec((tn, sq * nc), lambda b, s, j: (j, b * steps + s)),
        out_shape=jax.ShapeDtypeStruct((n, bsz * seq), BF16),
        scratch_shapes=[
            pltpu.VMEM((sq, nc, d), x.dtype),
            pltpu.VMEM((sq * nc, d), BF16),
            pltpu.SemaphoreType.DMA((sq,)),
        ],
        compiler_params=_params("arbitrary", "arbitrary", "arbitrary"),
        name="norm_matmul_t",
    )(_chunk_view(x, bsz, seq), g.reshape(1, d), wt)


def _chunk_operand(u_ref, k, nc):
    h = S5_GROUP
    return jnp.concatenate(
        [u_ref[k * h:(k + 1) * h, s * nc:(s + 1) * nc] for s in range(S5_CHUNK)], axis=0)


def _s5_inject_kernel(u_ref, w_ref, ore_ref, oim_ref, *, bsz, seq):
    nc = ore_ref.shape[0]
    p2 = w_ref.shape[3]
    for b in range(bsz):
        ub = u_ref.at[:, b * seq:(b + 1) * seq]
        u0 = _chunk_operand(ub, 0, nc)
        u1 = _chunk_operand(ub, 1, nc)
        cols = slice(b * p2, (b + 1) * p2)
        ore_ref[:, cols] = (_dot(w_ref[0, 0, 0], u0) + _dot(w_ref[0, 1, 0], u1)).T
        oim_ref[:, cols] = (_dot(w_ref[0, 0, 1], u0) + _dot(w_ref[0, 1, 1], u1)).T


def s5_inject(ut, w_pack, layer, *, bsz, seq):
    n_pair = w_pack.shape[1]
    p2 = w_pack.shape[4]
    nc = seq // S5_CHUNK
    out = jax.ShapeDtypeStruct((nc, n_pair * bsz * p2), F32)
    oblk = pl.BlockSpec((nc, bsz * p2), lambda j: (0, j))
    return pl.pallas_call(
        functools.partial(_s5_inject_kernel, bsz=bsz, seq=seq),
        grid=(n_pair,),
        in_specs=[
            pl.BlockSpec((2 * S5_GROUP, bsz * seq), lambda j: (j, 0)),
            _layer_spec(layer, (1,) + w_pack.shape[2:], lambda j: (j, 0, 0, 0, 0)),
        ],
        out_specs=[oblk, oblk],
        out_shape=[out, out],
        compiler_params=_params("parallel"),
        name="s5_inject",
    )(ut, w_pack)


def _s5_scan_kernel(wre_ref, wim_ref, are_ref, aim_ref, xre_ref, xim_ref, *, n_c):
    a_re = are_ref[...]
    a_im = aim_ref[...]

    def body(c, carry):
        s_re, s_im = carry
        row = pl.ds(c, 1)
        xre_ref[row, :] = s_re
        xim_ref[row, :] = s_im
        n_re = a_re * s_re - a_im * s_im + wre_ref[row, :]
        n_im = a_re * s_im + a_im * s_re + wim_ref[row, :]
        return n_re, n_im

    zero = jnp.zeros(a_re.shape, F32)
    lax.fori_loop(0, n_c, body, (zero, zero))


def s5_scan(w_re, w_im, a_re, a_im, layer, *, tl=2048):
    n_c, cols = w_re.shape
    tl = min(tl, cols)
    out = jax.ShapeDtypeStruct((n_c, cols), F32)
    blk = pl.BlockSpec((n_c, tl), lambda j: (0, j))
    coef = _layer_spec(layer, (1, tl), lambda j: (0, j))
    return pl.pallas_call(
        functools.partial(_s5_scan_kernel, n_c=n_c),
        grid=(cols // tl,),
        in_specs=[blk, blk, coef, coef],
        out_specs=[blk, blk],
        out_shape=[out, out],
        compiler_params=_params("parallel"),
        name="s5_scan",
    )(w_re, w_im, a_re, a_im)


def _gelu_tanh(y):
    c = np.float32(np.sqrt(2.0 / np.pi))
    return 0.5 * y * (1.0 + jnp.tanh(c * (y + 0.044715 * (y * y * y))))


def _s5_out_kernel(u_ref, t_ref, xre_ref, xim_ref, v_ref, o_ref, *, bsz, seq):
    nc = xre_ref.shape[0]
    p2 = v_ref.shape[-1]
    h = S5_GROUP
    for b in range(bsz):
        ub = u_ref.at[:, b * seq:(b + 1) * seq]
        x_re = xre_ref[:, b * p2:(b + 1) * p2].astype(BF16)
        x_im = xim_ref[:, b * p2:(b + 1) * p2].astype(BF16)
        for k in range(2):
            y = (_dot(t_ref[k], _chunk_operand(ub, k, nc))
                 + _dot_nt(v_ref[0, k, 0], x_re) + _dot_nt(v_ref[0, k, 1], x_im))
            y = _gelu_tanh(y).astype(o_ref.dtype)
            for t in range(S5_CHUNK):
                lanes = slice(b * seq + t * nc, b * seq + (t + 1) * nc)
                o_ref[k * h:(k + 1) * h, lanes] = y[t * h:(t + 1) * h, :]


def s5_out(ut, t_mat, x_re, x_im, v_pack, layer, *, bsz, seq):
    n_pair = v_pack.shape[1]
    p2 = v_pack.shape[-1]
    nc = seq // S5_CHUNK
    lh = t_mat.shape[-1]
    xblk = pl.BlockSpec((nc, bsz * p2), lambda j: (0, j))
    ublk = pl.BlockSpec((2 * S5_GROUP, bsz * seq), lambda j: (j, 0))
    return pl.pallas_call(
        functools.partial(_s5_out_kernel, bsz=bsz, seq=seq),
        grid=(n_pair,),
        in_specs=[
            ublk,
            _layer_spec(layer, (2, lh, lh), lambda j: (j, 0, 0)),
            xblk,
            xblk,
            _layer_spec(layer, (1,) + v_pack.shape[2:], lambda j: (j, 0, 0, 0, 0)),
        ],
        out_specs=ublk,
        out_shape=jax.ShapeDtypeStruct(ut.shape, BF16),
        compiler_params=_params("parallel"),
        name="s5_out",
    )(ut, t_mat, x_re, x_im, v_pack)


def s5_mixer(ut, bsz, seq, ops, layer):
    t_mat, w_pack, v_pack, a_re, a_im = ops
    w_re, w_im = s5_inject(ut, w_pack, layer, bsz=bsz, seq=seq)
    x_re, x_im = s5_scan(w_re, w_im, a_re, a_im, layer)
    return s5_out(ut, t_mat, x_re, x_im, v_pack, layer, bsz=bsz, seq=seq)


def _glu_kernel(yt_ref, wv_ref, wg_ref, g_ref, o_ref, y_ref, m_ref, *, nj, tn):
    j = pl.program_id(2)
    nc = o_ref.shape[0]

    @pl.when(j == 0)
    def _():
        for k in range(S5_POS_PER_STEP):
            y_ref[k * nc:(k + 1) * nc, :] = (
                yt_ref[:, k * nc:(k + 1) * nc].astype(F32).T.astype(BF16))

    y = y_ref[...]
    m_ref[j] = _dot(y, wv_ref[...]) * _sigmoid(_dot(y, wg_ref[...]))

    @pl.when(j == nj - 1)
    def _():
        d = g_ref.shape[1]
        ss = None
        for jj in range(nj):
            mj = m_ref[jj]
            s = jnp.sum(mj * mj, axis=-1, keepdims=True)
            ss = s if ss is None else ss + s
        inv = lax.rsqrt(ss / d + EPS)
        for k in range(S5_POS_PER_STEP):
            rows = slice(k * nc, (k + 1) * nc)
            for jj in range(nj):
                cols = slice(jj * tn, (jj + 1) * tn)
                o_ref[:, k, cols] = m_ref[jj, rows, :] * inv[rows] * g_ref[:, cols]


def glu_norm(yt, w_glu, layer, g, *, bsz, seq, tn=GLU_COL_TILE):
    k, _ = yt.shape
    d = w_glu.shape[2] // 2
    tn = min(tn, d)
    nj = d // tn
    nc = seq // S5_CHUNK
    sq = S5_POS_PER_STEP
    steps = S5_CHUNK // sq
    out = pl.pallas_call(
        functools.partial(_glu_kernel, nj=nj, tn=tn),
        grid=(bsz, steps, nj),
        in_specs=[
            pl.BlockSpec((k, sq * nc), lambda b, s, j: (0, b * steps + s)),
            _layer_spec(layer, (k, tn), lambda b, s, j: (0, j)),
            _layer_spec(layer, (k, tn), lambda b, s, j: (0, j + nj)),
            pl.BlockSpec((1, d), lambda b, s, j: (0, 0)),
        ],
        out_specs=pl.BlockSpec((None, nc, sq, d), lambda b, s, j: (b, 0, s, 0)),
        out_shape=jax.ShapeDtypeStruct((bsz, nc, S5_CHUNK, d), F32),
        scratch_shapes=[pltpu.VMEM((sq * nc, k), BF16), pltpu.VMEM((nj, sq * nc, tn), F32)],
        compiler_params=_params("parallel", "parallel", "arbitrary"),
        name="glu_norm",
    )(yt, w_glu, w_glu, g.reshape(1, d))
    return out.reshape(bsz * seq, d)


def _band_bias(rel_bias):
    c = ATT_CHUNK
    nq = ATT_QGROUP * c
    nk = (ATT_QGROUP + ATT_LEFT) * c
    r = rel_bias.astype(F32)
    n_h = r.shape[0]
    d_lo = ATT_LEFT * c - (nk - 1)
    d_hi = ATT_LEFT * c + nq - 1
    n_lo = -(c - 1) - d_lo
    n_hi = d_hi - ATT_MAX_REL
    e = jnp.concatenate([jnp.broadcast_to(r[:, :1], (n_h, n_lo)), r,
                         jnp.broadcast_to(r[:, -1:], (n_h, n_hi))], axis=1)
    assert e.shape[1] == nq + nk - 1
    width = 2 * c - 1
    starts = [nk - 1 - (c - 1) - n * c for n in range(ATT_LEFT + 1)]
    v = jnp.stack([e[:, st:st + width] for st in starts], axis=1)
    frev = jnp.concatenate([v[:, :, ::-1], jnp.zeros(v.shape[:2] + (1,), F32)], axis=2)
    skew = jnp.tile(frev, (1, 1, c))[:, :, :c * width].reshape(n_h, ATT_LEFT + 1, c, width)
    blocks = skew[:, :, :, c - 1:]
    masked = jnp.full((n_h, c, c), NEG_INF, F32)
    rows = [jnp.concatenate([blocks[:, kb - qa] if 0 <= kb - qa <= ATT_LEFT else masked
                             for kb in range(ATT_QGROUP + ATT_LEFT)], axis=-1)
            for qa in range(ATT_QGROUP)]
    return jnp.concatenate(rows, axis=-2)


def _band_attn_kernel(q_ref, k_ref, v_ref, b_ref, o_ref, s_ref, p_ref, v1_ref, *, seq):
    c = ATT_CHUNK
    dh = ATT_HEAD_DIM
    nq = ATT_QGROUP * c
    nk = (ATT_QGROUP + ATT_LEFT) * c
    v1_ref[:, :dh] = v_ref[...]
    v1_ref[:, dh:] = jnp.ones((seq, dh), BF16)
    for gi in range(seq // nq):
        q0 = gi * nq
        off = max(0, ATT_LEFT * c - q0)
        k0 = q0 - ATT_LEFT * c + off
        w = nk - off
        s_ref[:, :w] = _dot_nt(q_ref[q0:q0 + nq, :], k_ref[k0:k0 + w, :])
        for a in range(ATT_QGROUP):
            rows = slice(a * c, (a + 1) * c)
            lo = max((a * c) // LANES * LANES, off)
            hi = min(-(-((a + ATT_LEFT + 1) * c) // LANES) * LANES, nk)
            s = s_ref[rows, lo - off:hi - off] + b_ref[0, rows, lo:hi]
            m = jnp.max(s, axis=-1, keepdims=True)
            p_ref[rows, lo - off:hi - off] = jnp.exp(s - m).astype(BF16)
            if lo > off:
                p_ref[rows, 0:lo - off] = jnp.zeros((c, lo - off), BF16)
            if hi < nk:
                p_ref[rows, hi - off:w] = jnp.zeros((c, nk - hi), BF16)
        o2 = _dot(p_ref[:, :w], v1_ref[k0:k0 + w, :])
        o_ref[q0:q0 + nq, :] = (o2[:, :dh] / o2[:, dh:]).astype(o_ref.dtype)


def band_attention(q, kv, bias, layer, *, bsz, seq):
    m, d = q.shape
    n_h = d // ATT_HEAD_DIM
    dh = ATT_HEAD_DIM
    nq, nk = bias.shape[2:]
    return pl.pallas_call(
        functools.partial(_band_attn_kernel, seq=seq),
        grid=(bsz, n_h),
        in_specs=[
            pl.BlockSpec((seq, dh), lambda b, h: (b, h)),
            pl.BlockSpec((seq, dh), lambda b, h: (b, h)),
            pl.BlockSpec((seq, dh), lambda b, h: (b, h + n_h)),
            _layer_spec(layer, (1, nq, nk), lambda b, h: (h, 0, 0)),
        ],
        out_specs=pl.BlockSpec((seq, dh), lambda b, h: (b, h)),
        out_shape=jax.ShapeDtypeStruct((m, d), BF16),
        scratch_shapes=[
            pltpu.VMEM((nq, nk), F32),
            pltpu.VMEM((nq, nk), BF16),
            pltpu.VMEM((seq, 2 * dh), BF16),
        ],
        compiler_params=_params("parallel", "parallel"),
        name="band_attention",
    )(q, kv, kv, bias)


def _mem_attn_kernel(*refs, has_update):
    if has_update:
        x_ref, r_ref, g1_ref, wq_ref, k_ref, v_ref, wo_ref, g2_ref, o_ref = refs
        x = x_ref[...] + r_ref[...]
    else:
        x_ref, g1_ref, wq_ref, k_ref, v_ref, wo_ref, g2_ref, o_ref = refs
        x = x_ref[...]
    h = _rms(x, g1_ref[...]).astype(BF16)
    q = _dot(h, wq_ref[...]).astype(BF16)
    k = k_ref[...]
    v = v_ref[...]
    dh = MEM_HEAD_DIM
    n_heads = q.shape[1] // dh
    scores = [_dot_nt(q[:, hd * dh:(hd + 1) * dh], k[:, hd * dh:(hd + 1) * dh])
              for hd in range(n_heads)]
    ones = jnp.ones((v.shape[0], dh), BF16)
    heads = []
    for hd in range(n_heads):
        s = scores[hd]
        p = jnp.exp(s - jnp.max(s, axis=-1, keepdims=True)).astype(BF16)
        o2 = _dot(p, jnp.concatenate([v[:, hd * dh:(hd + 1) * dh], ones], axis=-1))
        heads.append((o2[:, :dh] / o2[:, dh:]).astype(BF16))
    o = jnp.concatenate(heads, axis=-1)
    c = _dot(o, wo_ref[...])
    o_ref[...] = x + _rms(c, g2_ref[...])


def mem_attention(x, g1, wq, kvmem, layer, wo, g2, *, seq, n_mem, update=None, tm=ROW_TILE):
    m, d = x.shape
    md = wq.shape[2]
    tiles_per_batch = seq // tm
    row = pl.BlockSpec((tm, d), lambda i: (i, 0))
    acts = [x] if update is None else [x, update]
    return pl.pallas_call(
        functools.partial(_mem_attn_kernel, has_update=update is not None),
        grid=(m // tm,),
        in_specs=[row] * len(acts) + [
            pl.BlockSpec((1, d), lambda i: (0, 0)),
            _layer_spec(layer, (d, md), lambda i: (0, 0)),
            pl.BlockSpec((n_mem, md), lambda i: (i // tiles_per_batch, 2 * layer)),
            pl.BlockSpec((n_mem, md), lambda i: (i // tiles_per_batch, 2 * layer + 1)),
            _layer_spec(layer, (md, d), lambda i: (0, 0)),
            pl.BlockSpec((1, d), lambda i: (0, 0)),
        ],
        out_specs=row,
        out_shape=jax.ShapeDtypeStruct((m, d), F32),
        compiler_params=_params("parallel"),
        name="mem_attention",
    )(*acts, g1.reshape(1, d), wq, kvmem, kvmem, wo, g2.reshape(1, d))


def _conv_ffn_kernel(x_ref, g1_ref, wuv_ref, wug_ref, cwv_ref, cwg_ref, cbv_ref, cbg_ref, wd_ref,
                     g2_ref, o_ref, h_ref, acc_ref, tail_ref, up_ref, wd_bf_ref, *, nj,
                     tiles_per_batch):
    i = pl.program_id(0)
    j = pl.program_id(1)
    tm = x_ref.shape[0]
    hm = tm // FFN_ROW_SPLIT

    @pl.when(j == 0)
    def _():
        h_ref[...] = _rms(x_ref[...], g1_ref[...]).astype(h_ref.dtype)
        acc_ref[...] = jnp.zeros(acc_ref.shape, F32)

    @pl.when(i % tiles_per_batch == 0)
    def _():
        tail_ref[j] = jnp.zeros(tail_ref.shape[1:], F32)

    up_ref[:, 0:SUBLANES, :] = tail_ref[j]
    wd_bf_ref[...] = wd_ref[...].astype(BF16)

    for r in range(FFN_ROW_SPLIT):
        h = h_ref[r * hm:(r + 1) * hm, :]
        rows = slice(SUBLANES + r * hm, SUBLANES + (r + 1) * hm)
        up_ref[0, rows, :] = _dot(h, wuv_ref[...])
        up_ref[1, rows, :] = _dot(h, wug_ref[...])

    tail_ref[j] = up_ref[:, tm:tm + SUBLANES, :]

    def conv(slot, base, cw, cb):
        taps = [up_ref[slot, base - (CONV_W - 1 - k):base - (CONV_W - 1 - k) + hm, :]
                for k in range(CONV_W)]
        return cb + taps[0] * cw[0:1, :] + taps[1] * cw[1:2, :] + taps[2] * cw[2:3, :]

    for r in range(FFN_ROW_SPLIT):
        base = SUBLANES + r * hm
        val = conv(0, base, cwv_ref[...], cbv_ref[...])
        gate = conv(1, base, cwg_ref[...], cbg_ref[...])
        act = (val * (gate * _sigmoid(gate))).astype(BF16)
        acc_ref[r * hm:(r + 1) * hm, :] += _dot(act, wd_bf_ref[...])

    @pl.when(j == nj - 1)
    def _():
        o_ref[...] = x_ref[...] + _rms(acc_ref[...], g2_ref[...])


def conv_ffn(x, g1, w_up, conv_w, conv_b, w_down, layer, g2, *, seq, tm=ROW_TILE, tf=FFN_COL_TILE):
    m, d = x.shape
    f = w_down.shape[1]
    nj = f // tf
    tiles_per_batch = seq // tm
    conv_b = conv_b.reshape(conv_b.shape[0], 1, 2 * f)
    return pl.pallas_call(
        functools.partial(_conv_ffn_kernel, nj=nj, tiles_per_batch=tiles_per_batch),
        grid=(m // tm, nj),
        in_specs=[
            pl.BlockSpec((tm, d), lambda i, j: (i, 0)),
            pl.BlockSpec((1, d), lambda i, j: (0, 0)),
            _layer_spec(layer, (d, tf), lambda i, j: (0, j)),
            _layer_spec(layer, (d, tf), lambda i, j: (0, j + nj)),
            _layer_spec(layer, (CONV_W, tf), lambda i, j: (0, j)),
            _layer_spec(layer, (CONV_W, tf), lambda i, j: (0, j + nj)),
            _layer_spec(layer, (1, tf), lambda i, j: (0, j)),
            _layer_spec(layer, (1, tf), lambda i, j: (0, j + nj)),
            _layer_spec(layer, (tf, d), lambda i, j: (j, 0)),
            pl.BlockSpec((1, d), lambda i, j: (0, 0)),
        ],
        out_specs=pl.BlockSpec((tm, d), lambda i, j: (i, 0)),
        out_shape=jax.ShapeDtypeStruct((m, d), F32),
        scratch_shapes=[
            pltpu.VMEM((tm, d), BF16),
            pltpu.VMEM((tm, d), F32),
            pltpu.VMEM((nj, 2, SUBLANES, tf), F32),
            pltpu.VMEM((2, SUBLANES + tm, tf), F32),
            pltpu.VMEM((tf, d), BF16),
        ],
        compiler_params=_params("arbitrary", "arbitrary"),
        name="conv_ffn",
    )(x, g1.reshape(1, d), w_up, w_up, conv_w, conv_w, conv_b, conv_b, w_down, g2.reshape(1, d))


def kernel(x, mem, norm_mix, norm_mem, norm_ffn, mem_in_norm, a_w_in, a_lam_re, a_lam_im, a_log_dt,
           a_b_re, a_b_im, a_c_re, a_c_im, a_d, a_w_glu, kv_norm, w_k, w_v, b_w_q, b_rel_bias, b_w_o,
           m_w_q, m_w_kv, m_w_o, f_w_up, f_conv_w, f_conv_b, f_w_down):
    bsz, seq, d = x.shape
    depth = norm_mix.shape[0]
    n_a = a_w_in.shape[0]
    n_mem = mem.shape[1]
    md = m_w_q.shape[2]
    assert seq % ROW_TILE == 0 and seq % (ATT_QGROUP * ATT_CHUNK) == 0
    assert (seq // S5_CHUNK) % LANES == 0 and (d // S5_GROUP) % 2 == 0

    xf = x.reshape(bsz * seq, d)

    w_kv_mem = m_w_kv.astype(BF16)
    w_kv = jnp.stack([w_k, w_v]).astype(BF16)
    a_w_in_t = jnp.transpose(a_w_in, (0, 2, 1)).astype(BF16)
    a_w_glu_b = a_w_glu.astype(BF16)
    b_w_q_b = (b_w_q * (ATT_HEAD_DIM ** -0.5)).astype(BF16)
    b_w_o_b = b_w_o.astype(BF16)
    m_w_q_b = (m_w_q * (MEM_HEAD_DIM ** -0.5)).astype(BF16)
    m_w_o_b = m_w_o.astype(BF16)
    f_w_up_b = f_w_up.astype(BF16)
    flat = lambda a: a.reshape((-1,) + a.shape[2:])
    s5_ops = _s5_operators(flat(a_lam_re), flat(a_lam_im), flat(a_log_dt), flat(a_b_re),
                           flat(a_b_im), flat(a_c_re), flat(a_c_im), a_d, bsz)
    s5_ops = [o.reshape((n_a, o.shape[0] // n_a) + o.shape[1:]) for o in s5_ops[:3]] + [
        o.reshape(n_a, 1, -1) for o in s5_ops[3:]]
    att_bias = jax.vmap(_band_bias)(b_rel_bias)

    kvmem = norm_matmul(mem.reshape(bsz * n_mem, d), mem_in_norm, w_kv_mem, None)

    kv = None
    for l in range(depth):
        update = None
        if l == n_a:
            kv = norm_matmul(xf, kv_norm, w_kv, None)
        if l < n_a:
            ut = norm_matmul_t(xf, norm_mix[l, 0], a_w_in_t, l, bsz=bsz, seq=seq)
            yt = s5_mixer(ut, bsz, seq, s5_ops, l)
            update = glu_norm(yt, a_w_glu_b, l, norm_mix[l, 1], bsz=bsz, seq=seq)
        else:
            jb = l - n_a
            q = norm_matmul(xf, norm_mix[l, 0], b_w_q_b, jb)
            o = band_attention(q, kv, att_bias, jb, bsz=bsz, seq=seq)
            xf = proj_res(o, b_w_o_b, jb, xf, norm_mix[l, 1])
        xf = mem_attention(xf, norm_mem[l, 0], m_w_q_b, kvmem, l, m_w_o_b, norm_mem[l, 1],
                           seq=seq, n_mem=n_mem, update=update)
        xf = conv_ffn(xf, norm_ffn[l, 0], f_w_up_b, f_conv_w, f_conv_b, f_w_down, l,
                      norm_ffn[l, 1], seq=seq)
    return xf.reshape(bsz, seq, d)
```

```python
import functools

import numpy as np
import jax
import jax.numpy as jnp
from jax import lax
from jax.experimental import pallas as pl
from jax.experimental.pallas import tpu as pltpu

EPS = 1e-6
NEG_INF = -1e30
BF16 = jnp.bfloat16
F32 = jnp.float32

S5_GROUP = 16
S5_CHUNK = 16
S5_POS_PER_STEP = 8
ATT_CHUNK = 64
ATT_LEFT = 8
ATT_HEAD_DIM = 128
ATT_MAX_REL = 256
ATT_QGROUP = 4
MEM_HEAD_DIM = 128
CONV_W = 3

VMEM_LIMIT_BYTES = 56 * 1024 * 1024
ROW_TILE = 512
COL_TILE = 2048
FFN_ROW_SPLIT = 2
FFN_COL_TILE = 512
GLU_COL_TILE = 1024
SUBLANES = 8
LANES = 128

_NT = (((1,), (1,)), ((), ()))


def _params(*sem):
    return pltpu.CompilerParams(dimension_semantics=sem, vmem_limit_bytes=VMEM_LIMIT_BYTES)


def _rms(x, g):
    ms = jnp.mean(x * x, axis=-1, keepdims=True)
    return x * lax.rsqrt(ms + EPS) * g


def _sigmoid(x):
    return 1.0 / (1.0 + jnp.exp(-x))


def _dot(a, b):
    return jnp.dot(a, b, preferred_element_type=F32)


def _dot_nt(a, b):
    return lax.dot_general(a, b, _NT, preferred_element_type=F32)


def _layer_spec(layer, block, index_map):
    return pl.BlockSpec((None,) + tuple(block), lambda *ids: (layer,) + tuple(index_map(*ids)))


def _norm_matmul_kernel(x_ref, g_ref, w_ref, o_ref, h_ref):
    @pl.when(pl.program_id(1) == 0)
    def _():
        h_ref[...] = _rms(x_ref[...], g_ref[...]).astype(h_ref.dtype)

    o_ref[...] = _dot(h_ref[...], w_ref[...]).astype(o_ref.dtype)


def norm_matmul(x, g, w, layer, *, tm=ROW_TILE, tn=COL_TILE, out_dtype=BF16):
    m, k = x.shape
    if layer is None:
        tn = w.shape[2]
        n = w.shape[0] * tn
        w_spec = pl.BlockSpec((None, k, tn), lambda i, j: (j, 0, 0))
    else:
        n = w.shape[2]
        tn = min(tn, n)
        w_spec = _layer_spec(layer, (k, tn), lambda i, j: (0, j))
    return pl.pallas_call(
        _norm_matmul_kernel,
        grid=(m // tm, n // tn),
        in_specs=[
            pl.BlockSpec((tm, k), lambda i, j: (i, 0)),
            pl.BlockSpec((1, k), lambda i, j: (0, 0)),
            w_spec,
        ],
        out_specs=pl.BlockSpec((tm, tn), lambda i, j: (i, j)),
        out_shape=jax.ShapeDtypeStruct((m, n), out_dtype),
        scratch_shapes=[pltpu.VMEM((tm, k), BF16)],
        compiler_params=_params("parallel", "arbitrary"),
        name="norm_matmul",
    )(x, g.reshape(1, k), w)


def _proj_res_kernel(a_ref, w_ref, x_ref, g_ref, o_ref):
    f = _dot(a_ref[...], w_ref[...])
    o_ref[...] = x_ref[...] + _rms(f, g_ref[...])


def proj_res(a, w, layer, x, g, *, tm=ROW_TILE):
    m, k = a.shape
    d = w.shape[2]
    return pl.pallas_call(
        _proj_res_kernel,
        grid=(m // tm,),
        in_specs=[
            pl.BlockSpec((tm, k), lambda i: (i, 0)),
            _layer_spec(layer, (k, d), lambda i: (0, 0)),
            pl.BlockSpec((tm, d), lambda i: (i, 0)),
            pl.BlockSpec((1, d), lambda i: (0, 0)),
        ],
        out_specs=pl.BlockSpec((tm, d), lambda i: (i, 0)),
        out_shape=jax.ShapeDtypeStruct((m, d), F32),
        compiler_params=_params("parallel"),
        name="proj_res",
    )(a, w, x, g.reshape(1, d))


def _s5_build_kernel(rev_ref, bb_ref, cc_ref, cpad_ref, pt_ref, d_ref, t_ref, w_ref, v_ref):
    n_l = S5_CHUNK
    n_h = S5_GROUP
    lh = n_l * n_h
    n_p = rev_ref.shape[2]

    def exact_dot(a, b):
        return jnp.dot(a, b, precision=lax.Precision.HIGHEST, preferred_element_type=F32)

    def spread(a, pattern):
        hi = a.astype(BF16)
        rest = a - hi.astype(F32)
        mid = rest.astype(BF16)
        lo = (rest - mid.astype(F32)).astype(BF16)
        return _dot(hi, pattern) + _dot(mid, pattern) + _dot(lo, pattern)

    lane = lax.broadcasted_iota(jnp.int32, (n_l, lh), 1)
    row = lax.broadcasted_iota(jnp.int32, (n_l, lh), 0)
    per_step = jnp.where(lane // n_h == row, 1.0, 0.0).astype(BF16)
    per_chan = jnp.where(lane % n_h == row, 1.0, 0.0).astype(BF16)
    sub = lax.broadcasted_iota(jnp.int32, (n_h, lh), 0)
    lan = lax.broadcasted_iota(jnp.int32, (n_h, lh), 1)
    no_state = jnp.zeros((n_p, lh), BF16)
    for k in range(2):
        r_re = spread(rev_ref[k, 0], per_step)
        r_im = spread(rev_ref[k, 1], per_step)
        b_re = spread(bb_ref[k, 0], per_chan)
        b_im = spread(bb_ref[k, 1], per_chan)
        w_re = r_re * b_re - r_im * b_im
        w_im = r_re * b_im + r_im * b_re
        for ri, w in enumerate((w_re, w_im)):
            for slot in range(2):
                w_ref[0, k, ri, slot * n_p:(slot + 1) * n_p, :] = (
                    w.astype(BF16) if slot == k else no_state)
        krev = exact_dot(cc_ref[k, 0], w_re) - exact_dot(cc_ref[k, 1], w_im)
        skip = d_ref[k]
        for t in range(n_l):
            sh = (n_l - 1 - t) * n_h
            slab = krev
            if sh:
                slab = jnp.where(lan < lh - sh, pltpu.roll(krev, lh - sh, axis=1), 0.0)
            slab = slab + jnp.where(lan == t * n_h + sub, skip, 0.0)
            t_ref[k, t * n_h:(t + 1) * n_h, :] = slab.astype(BF16)
        c_re = cpad_ref[k, 0]
        c_im = cpad_ref[k, 1]
        for t in range(n_l):
            p_re = pt_ref[k, 0, t + 1:t + 2, :]
            p_im = pt_ref[k, 1, t + 1:t + 2, :]
            v_ref[0, k, 0, t * n_h:(t + 1) * n_h, :] = (c_re * p_re - c_im * p_im).astype(BF16)
            v_ref[0, k, 1, t * n_h:(t + 1) * n_h, :] = (-(c_re * p_im + c_im * p_re)).astype(BF16)


def _s5_operators(lam_re, lam_im, log_dt, b_re, b_im, c_re, c_im, d_skip, bsz):
    n_g, n_p = lam_re.shape
    n_h = S5_GROUP
    n_l = S5_CHUNK
    lh = n_l * n_h
    lr = lam_re.astype(F32)
    li = lam_im.astype(F32)
    dt = jnp.exp(log_dt.astype(F32))[:, None]
    mag = jnp.exp(lr * dt)
    ang = li * dt
    ab_re = mag * jnp.cos(ang)
    ab_im = mag * jnp.sin(ang)
    den = lr * lr + li * li
    nr = ab_re - 1.0
    f_re = (nr * lr + ab_im * li) / den
    f_im = (ab_im * lr - nr * li) / den
    br = b_re.astype(F32)
    bi = b_im.astype(F32)
    bb = jnp.stack([f_re[..., None] * br - f_im[..., None] * bi,
                    f_re[..., None] * bi + f_im[..., None] * br], axis=1)
    tau = jnp.arange(n_l + 1, dtype=F32)
    pw_re = jnp.exp((lr * dt)[..., None] * tau) * jnp.cos(ang[..., None] * tau)
    pw_im = jnp.exp((lr * dt)[..., None] * tau) * jnp.sin(ang[..., None] * tau)
    tau_m = tau[None, :, None]
    pt = jnp.stack([jnp.exp((lr * dt)[:, None, :] * tau_m) * jnp.cos(ang[:, None, :] * tau_m),
                    jnp.exp((lr * dt)[:, None, :] * tau_m) * jnp.sin(ang[:, None, :] * tau_m)],
                   axis=1)
    rev = jnp.stack([pw_re[:, :, n_l - 1::-1], pw_im[:, :, n_l - 1::-1]], axis=1)
    cc = jnp.stack([c_re.astype(F32), c_im.astype(F32)], axis=1)
    slot = (jnp.arange(n_g) % 2)[:, None, None, None]

    def lane_place(a):
        z = jnp.zeros_like(a)
        return jnp.where(slot == 0, jnp.concatenate([a, z], -1), jnp.concatenate([z, a], -1))

    skip = jnp.broadcast_to(d_skip.astype(F32).reshape(n_g, n_h, 1), (n_g, n_h, lh))
    n_pair = n_g // 2
    pair = lambda shape: pl.BlockSpec((2,) + shape, lambda j: (j,) + (0,) * len(shape))
    t_mat, w_pack, v_pack = pl.pallas_call(
        _s5_build_kernel,
        grid=(n_pair,),
        in_specs=[pair((2, n_p, n_l)), pair((2, n_p, n_h)), pair((2, n_h, n_p)),
                  pair((2, n_h, 2 * n_p)), pair((2, n_l + 1, 2 * n_p)), pair((n_h, lh))],
        out_specs=[
            pl.BlockSpec((2, lh, lh), lambda j: (j, 0, 0)),
            pl.BlockSpec((1, 2, 2, 2 * n_p, lh), lambda j: (j, 0, 0, 0, 0)),
            pl.BlockSpec((1, 2, 2, lh, 2 * n_p), lambda j: (j, 0, 0, 0, 0)),
        ],
        out_shape=[
            jax.ShapeDtypeStruct((n_g, lh, lh), BF16),
            jax.ShapeDtypeStruct((n_pair, 2, 2, 2 * n_p, lh), BF16),
            jax.ShapeDtypeStruct((n_pair, 2, 2, lh, 2 * n_p), BF16),
        ],
        compiler_params=_params("parallel"),
        name="s5_build_operators",
    )(rev, bb, cc, lane_place(cc), lane_place(pt), skip)

    def state_cols(a):
        a = jnp.broadcast_to(a.reshape(n_pair, 1, 2 * n_p), (n_pair, bsz, 2 * n_p))
        return a.reshape(1, n_g * bsz * n_p)
    return t_mat, w_pack, v_pack, state_cols(pw_re[:, :, n_l]), state_cols(pw_im[:, :, n_l])


def _chunk_view(x, bsz, seq):
    return x.reshape(bsz, seq // S5_CHUNK, S5_CHUNK, x.shape[1])


def _position_slab_copy(x_hbm, slab_ref, sem, b, pos, k):
    return pltpu.make_async_copy(x_hbm.at[b, :, pos, :], slab_ref.at[k], sem.at[k])


def _norm_matmul_t_kernel(x_hbm, g_ref, wt_ref, o_ref, slab_ref, h_ref, sem):
    b = pl.program_id(0)
    s = pl.program_id(1)
    sq, nc, _ = slab_ref.shape

    @pl.when(pl.program_id(2) == 0)
    def _():
        copies = [_position_slab_copy(x_hbm, slab_ref, sem, b, s * sq + k, k) for k in range(sq)]
        for c in copies:
            c.start()
        for k, c in enumerate(copies):
            c.wait()
            h_ref[k * nc:(k + 1) * nc, :] = _rms(slab_ref[k], g_ref[...]).astype(BF16)

    o_ref[...] = _dot_nt(wt_ref[...], h_ref[...]).astype(o_ref.dtype)


def norm_matmul_t(x, g, wt, layer, *, bsz, seq, tn=1024):
    d = x.shape[1]
    n = wt.shape[1]
    tn = min(tn, n)
    nc = seq // S5_CHUNK
    sq = S5_POS_PER_STEP
    steps = S5_CHUNK // sq
    return pl.pallas_call(
        _norm_matmul_t_kernel,
        grid=(bsz, steps, n // tn),
        in_specs=[
            pl.BlockSpec(memory_space=pl.ANY),
            pl.BlockSpec((1, d), lambda b, s, j: (0, 0)),
            _layer_spec(layer, (tn, d), lambda b, s, j: (j, 0)),
        ],
        out_specs=pl.BlockSpec((tn, sq * nc), lambda b, s, j: (j, b * steps + s)),
        out_shape=jax.ShapeDtypeStruct((n, bsz * seq), BF16),
        scratch_shapes=[
            pltpu.VMEM((sq, nc, d), x.dtype),
            pltpu.VMEM((sq * nc, d), BF16),
            pltpu.SemaphoreType.DMA((sq,)),
        ],
        compiler_params=_params("arbitrary", "arbitrary", "arbitrary"),
        name="norm_matmul_t",
    )(_chunk_view(x, bsz, seq), g.reshape(1, d), wt)


def _chunk_operand(u_ref, k, nc):
    h = S5_GROUP
    return jnp.concatenate(
        [u_ref[k * h:(k + 1) * h, s * nc:(s + 1) * nc] for s in range(S5_CHUNK)], axis=0)


def _s5_inject_kernel(u_ref, w_ref, ore_ref, oim_ref, *, bsz, seq):
    nc = ore_ref.shape[0]
    p2 = w_ref.shape[3]
    for b in range(bsz):
        ub = u_ref.at[:, b * seq:(b + 1) * seq]
        u0 = _chunk_operand(ub, 0, nc)
        u1 = _chunk_operand(ub, 1, nc)
        cols = slice(b * p2, (b + 1) * p2)
        ore_ref[:, cols] = (_dot(w_ref[0, 0, 0], u0) + _dot(w_ref[0, 1, 0], u1)).T
        oim_ref[:, cols] = (_dot(w_ref[0, 0, 1], u0) + _dot(w_ref[0, 1, 1], u1)).T


def s5_inject(ut, w_pack, layer, *, bsz, seq):
    n_pair = w_pack.shape[1]
    p2 = w_pack.shape[4]
    nc = seq // S5_CHUNK
    out = jax.ShapeDtypeStruct((nc, n_pair * bsz * p2), F32)
    oblk = pl.BlockSpec((nc, bsz * p2), lambda j: (0, j))
    return pl.pallas_call(
        functools.partial(_s5_inject_kernel, bsz=bsz, seq=seq),
        grid=(n_pair,),
        in_specs=[
            pl.BlockSpec((2 * S5_GROUP, bsz * seq), lambda j: (j, 0)),
            _layer_spec(layer, (1,) + w_pack.shape[2:], lambda j: (j, 0, 0, 0, 0)),
        ],
        out_specs=[oblk, oblk],
        out_shape=[out, out],
        compiler_params=_params("parallel"),
        name="s5_inject",
    )(ut, w_pack)


def _s5_scan_kernel(wre_ref, wim_ref, are_ref, aim_ref, xre_ref, xim_ref, *, n_c):
    a_re = are_ref[...]
    a_im = aim_ref[...]

    def body(c, carry):
        s_re, s_im = carry
        row = pl.ds(c, 1)
        xre_ref[row, :] = s_re
        xim_ref[row, :] = s_im
        n_re = a_re * s_re - a_im * s_im + wre_ref[row, :]
        n_im = a_re * s_im + a_im * s_re + wim_ref[row, :]
        return n_re, n_im

    zero = jnp.zeros(a_re.shape, F32)
    lax.fori_loop(0, n_c, body, (zero, zero))


def s5_scan(w_re, w_im, a_re, a_im, layer, *, tl=2048):
    n_c, cols = w_re.shape
    tl = min(tl, cols)
    out = jax.ShapeDtypeStruct((n_c, cols), F32)
    blk = pl.BlockSpec((n_c, tl), lambda j: (0, j))
    coef = _layer_spec(layer, (1, tl), lambda j: (0, j))
    return pl.pallas_call(
        functools.partial(_s5_scan_kernel, n_c=n_c),
        grid=(cols // tl,),
        in_specs=[blk, blk, coef, coef],
        out_specs=[blk, blk],
        out_shape=[out, out],
        compiler_params=_params("parallel"),
        name="s5_scan",
    )(w_re, w_im, a_re, a_im)


def _gelu_tanh(y):
    c = np.float32(np.sqrt(2.0 / np.pi))
    return 0.5 * y * (1.0 + jnp.tanh(c * (y + 0.044715 * (y * y * y))))


def _s5_out_kernel(u_ref, t_ref, xre_ref, xim_ref, v_ref, o_ref, *, bsz, seq):
    nc = xre_ref.shape[0]
    p2 = v_ref.shape[-1]
    h = S5_GROUP
    for b in range(bsz):
        ub = u_ref.at[:, b * seq:(b + 1) * seq]
        x_re = xre_ref[:, b * p2:(b + 1) * p2].astype(BF16)
        x_im = xim_ref[:, b * p2:(b + 1) * p2].astype(BF16)
        for k in range(2):
            y = (_dot(t_ref[k], _chunk_operand(ub, k, nc))
                 + _dot_nt(v_ref[0, k, 0], x_re) + _dot_nt(v_ref[0, k, 1], x_im))
            y = _gelu_tanh(y).astype(o_ref.dtype)
            for t in range(S5_CHUNK):
                lanes = slice(b * seq + t * nc, b * seq + (t + 1) * nc)
                o_ref[k * h:(k + 1) * h, lanes] = y[t * h:(t + 1) * h, :]


def s5_out(ut, t_mat, x_re, x_im, v_pack, layer, *, bsz, seq):
    n_pair = v_pack.shape[1]
    p2 = v_pack.shape[-1]
    nc = seq // S5_CHUNK
    lh = t_mat.shape[-1]
    xblk = pl.BlockSpec((nc, bsz * p2), lambda j: (0, j))
    ublk = pl.BlockSpec((2 * S5_GROUP, bsz * seq), lambda j: (j, 0))
    return pl.pallas_call(
        functools.partial(_s5_out_kernel, bsz=bsz, seq=seq),
        grid=(n_pair,),
        in_specs=[
            ublk,
            _layer_spec(layer, (2, lh, lh), lambda j: (j, 0, 0)),
            xblk,
            xblk,
            _layer_spec(layer, (1,) + v_pack.shape[2:], lambda j: (j, 0, 0, 0, 0)),
        ],
        out_specs=ublk,
        out_shape=jax.ShapeDtypeStruct(ut.shape, BF16),
        compiler_params=_params("parallel"),
        name="s5_out",
    )(ut, t_mat, x_re, x_im, v_pack)


def s5_mixer(ut, bsz, seq, ops, layer):
    t_mat, w_pack, v_pack, a_re, a_im = ops
    w_re, w_im = s5_inject(ut, w_pack, layer, bsz=bsz, seq=seq)
    x_re, x_im = s5_scan(w_re, w_im, a_re, a_im, layer)
    return s5_out(ut, t_mat, x_re, x_im, v_pack, layer, bsz=bsz, seq=seq)


def _glu_kernel(yt_ref, wv_ref, wg_ref, o_ref, y_ref):
    nc = o_ref.shape[0]

    @pl.when(pl.program_id(2) == 0)
    def _():
        for k in range(S5_POS_PER_STEP):
            y_ref[k * nc:(k + 1) * nc, :] = (
                yt_ref[:, k * nc:(k + 1) * nc].astype(F32).T.astype(BF16))

    y = y_ref[...]
    m = _dot(y, wv_ref[...]) * _sigmoid(_dot(y, wg_ref[...]))
    for k in range(S5_POS_PER_STEP):
        o_ref[:, k, :] = m[k * nc:(k + 1) * nc, :]


def glu_gate(yt, w_glu, layer, *, bsz, seq, tn=GLU_COL_TILE):
    k, _ = yt.shape
    d = w_glu.shape[2] // 2
    tn = min(tn, d)
    nj = d // tn
    nc = seq // S5_CHUNK
    sq = S5_POS_PER_STEP
    steps = S5_CHUNK // sq
    out = pl.pallas_call(
        _glu_kernel,
        grid=(bsz, steps, nj),
        in_specs=[
            pl.BlockSpec((k, sq * nc), lambda b, s, j: (0, b * steps + s)),
            _layer_spec(layer, (k, tn), lambda b, s, j: (0, j)),
            _layer_spec(layer, (k, tn), lambda b, s, j: (0, j + nj)),
        ],
        out_specs=pl.BlockSpec((None, nc, sq, tn), lambda b, s, j: (b, 0, s, j)),
        out_shape=jax.ShapeDtypeStruct((bsz, nc, S5_CHUNK, d), F32),
        scratch_shapes=[pltpu.VMEM((sq * nc, k), BF16)],
        compiler_params=_params("parallel", "parallel", "arbitrary"),
        name="glu_gate",
    )(yt, w_glu, w_glu)
    return out.reshape(bsz * seq, d)


def _band_bias(rel_bias):
    c = ATT_CHUNK
    nq = ATT_QGROUP * c
    nk = (ATT_QGROUP + ATT_LEFT) * c
    r = rel_bias.astype(F32)
    n_h = r.shape[0]
    d_lo = ATT_LEFT * c - (nk - 1)
    d_hi = ATT_LEFT * c + nq - 1
    n_lo = -(c - 1) - d_lo
    n_hi = d_hi - ATT_MAX_REL
    e = jnp.concatenate([jnp.broadcast_to(r[:, :1], (n_h, n_lo)), r,
                         jnp.broadcast_to(r[:, -1:], (n_h, n_hi))], axis=1)
    assert e.shape[1] == nq + nk - 1
    width = 2 * c - 1
    starts = [nk - 1 - (c - 1) - n * c for n in range(ATT_LEFT + 1)]
    v = jnp.stack([e[:, st:st + width] for st in starts], axis=1)
    frev = jnp.concatenate([v[:, :, ::-1], jnp.zeros(v.shape[:2] + (1,), F32)], axis=2)
    skew = jnp.tile(frev, (1, 1, c))[:, :, :c * width].reshape(n_h, ATT_LEFT + 1, c, width)
    blocks = skew[:, :, :, c - 1:]
    masked = jnp.full((n_h, c, c), NEG_INF, F32)
    rows = [jnp.concatenate([blocks[:, kb - qa] if 0 <= kb - qa <= ATT_LEFT else masked
                             for kb in range(ATT_QGROUP + ATT_LEFT)], axis=-1)
            for qa in range(ATT_QGROUP)]
    return jnp.concatenate(rows, axis=-2)


def _band_attn_kernel(q_ref, k_ref, v_ref, b_ref, o_ref, s_ref, p_ref, v1_ref, *, seq):
    c = ATT_CHUNK
    dh = ATT_HEAD_DIM
    nq = ATT_QGROUP * c
    nk = (ATT_QGROUP + ATT_LEFT) * c
    v1_ref[:, :dh] = v_ref[...]
    v1_ref[:, dh:] = jnp.ones((seq, dh), BF16)
    for gi in range(seq // nq):
        q0 = gi * nq
        off = max(0, ATT_LEFT * c - q0)
        k0 = q0 - ATT_LEFT * c + off
        w = nk - off
        s_ref[:, :w] = _dot_nt(q_ref[q0:q0 + nq, :], k_ref[k0:k0 + w, :])
        for a in range(ATT_QGROUP):
            rows = slice(a * c, (a + 1) * c)
            lo = max((a * c) // LANES * LANES, off)
            hi = min(-(-((a + ATT_LEFT + 1) * c) // LANES) * LANES, nk)
            s = s_ref[rows, lo - off:hi - off] + b_ref[0, rows, lo:hi]
            m = jnp.max(s, axis=-1, keepdims=True)
            p_ref[rows, lo - off:hi - off] = jnp.exp(s - m).astype(BF16)
            if lo > off:
                p_ref[rows, 0:lo - off] = jnp.zeros((c, lo - off), BF16)
            if hi < nk:
                p_ref[rows, hi - off:w] = jnp.zeros((c, nk - hi), BF16)
        o2 = _dot(p_ref[:, :w], v1_ref[k0:k0 + w, :])
        o_ref[q0:q0 + nq, :] = (o2[:, :dh] / o2[:, dh:]).astype(o_ref.dtype)


def band_attention(q, kv, bias, layer, *, bsz, seq):
    m, d = q.shape
    n_h = d // ATT_HEAD_DIM
    dh = ATT_HEAD_DIM
    nq, nk = bias.shape[2:]
    return pl.pallas_call(
        functools.partial(_band_attn_kernel, seq=seq),
        grid=(bsz, n_h),
        in_specs=[
            pl.BlockSpec((seq, dh), lambda b, h: (b, h)),
            pl.BlockSpec((seq, dh), lambda b, h: (b, h)),
            pl.BlockSpec((seq, dh), lambda b, h: (b, h + n_h)),
            _layer_spec(layer, (1, nq, nk), lambda b, h: (h, 0, 0)),
        ],
        out_specs=pl.BlockSpec((seq, dh), lambda b, h: (b, h)),
        out_shape=jax.ShapeDtypeStruct((m, d), BF16),
        scratch_shapes=[
            pltpu.VMEM((nq, nk), F32),
            pltpu.VMEM((nq, nk), BF16),
            pltpu.VMEM((seq, 2 * dh), BF16),
        ],
        compiler_params=_params("parallel", "parallel"),
        name="band_attention",
    )(q, kv, kv, bias)


def _mem_attn_kernel(*refs, has_update):
    if has_update:
        x_ref, r_ref, gu_ref, g1_ref, wq_ref, k_ref, v_ref, wo_ref, g2_ref, o_ref = refs
        x = x_ref[...] + _rms(r_ref[...], gu_ref[...])
    else:
        x_ref, g1_ref, wq_ref, k_ref, v_ref, wo_ref, g2_ref, o_ref = refs
        x = x_ref[...]
    h = _rms(x, g1_ref[...]).astype(BF16)
    q = _dot(h, wq_ref[...]).astype(BF16)
    k = k_ref[...]
    v = v_ref[...]
    dh = MEM_HEAD_DIM
    n_heads = q.shape[1] // dh
    scores = [_dot_nt(q[:, hd * dh:(hd + 1) * dh], k[:, hd * dh:(hd + 1) * dh])
              for hd in range(n_heads)]
    ones = jnp.ones((v.shape[0], dh), BF16)
    heads = []
    for hd in range(n_heads):
        s = scores[hd]
        p = jnp.exp(s - jnp.max(s, axis=-1, keepdims=True)).astype(BF16)
        o2 = _dot(p, jnp.concatenate([v[:, hd * dh:(hd + 1) * dh], ones], axis=-1))
        heads.append((o2[:, :dh] / o2[:, dh:]).astype(BF16))
    o = jnp.concatenate(heads, axis=-1)
    c = _dot(o, wo_ref[...])
    o_ref[...] = x + _rms(c, g2_ref[...])


def mem_attention(x, g1, wq, kvmem, layer, wo, g2, *, seq, n_mem, update=None, tm=ROW_TILE):
    m, d = x.shape
    md = wq.shape[2]
    tiles_per_batch = seq // tm
    row = pl.BlockSpec((tm, d), lambda i: (i, 0))
    gain = pl.BlockSpec((1, d), lambda i: (0, 0))
    acts = [x] if update is None else [x, update[0], update[1].reshape(1, d)]
    return pl.pallas_call(
        functools.partial(_mem_attn_kernel, has_update=update is not None),
        grid=(m // tm,),
        in_specs=([row] if update is None else [row, row, gain]) + [
            pl.BlockSpec((1, d), lambda i: (0, 0)),
            _layer_spec(layer, (d, md), lambda i: (0, 0)),
            pl.BlockSpec((n_mem, md), lambda i: (i // tiles_per_batch, 2 * layer)),
            pl.BlockSpec((n_mem, md), lambda i: (i // tiles_per_batch, 2 * layer + 1)),
            _layer_spec(layer, (md, d), lambda i: (0, 0)),
            pl.BlockSpec((1, d), lambda i: (0, 0)),
        ],
        out_specs=row,
        out_shape=jax.ShapeDtypeStruct((m, d), F32),
        compiler_params=_params("parallel"),
        name="mem_attention",
    )(*acts, g1.reshape(1, d), wq, kvmem, kvmem, wo, g2.reshape(1, d))


def _conv_ffn_kernel(x_ref, g1_ref, wuv_ref, wug_ref, cwv_ref, cwg_ref, cbv_ref, cbg_ref, wd_ref,
                     g2_ref, o_ref, h_ref, acc_ref, tail_ref, up_ref, wd_bf_ref, *, nj,
                     tiles_per_batch):
    i = pl.program_id(0)
    j = pl.program_id(1)
    tm = x_ref.shape[0]
    hm = tm // FFN_ROW_SPLIT

    @pl.when(j == 0)
    def _():
        h_ref[...] = _rms(x_ref[...], g1_ref[...]).astype(h_ref.dtype)
        acc_ref[...] = jnp.zeros(acc_ref.shape, F32)

    @pl.when(i % tiles_per_batch == 0)
    def _():
        tail_ref[j] = jnp.zeros(tail_ref.shape[1:], F32)

    up_ref[:, 0:SUBLANES, :] = tail_ref[j]
    wd_bf_ref[...] = wd_ref[...].astype(BF16)

    for r in range(FFN_ROW_SPLIT):
        h = h_ref[r * hm:(r + 1) * hm, :]
        rows = slice(SUBLANES + r * hm, SUBLANES + (r + 1) * hm)
        up_ref[0, rows, :] = _dot(h, wuv_ref[...])
        up_ref[1, rows, :] = _dot(h, wug_ref[...])

    tail_ref[j] = up_ref[:, tm:tm + SUBLANES, :]

    def conv(slot, base, cw, cb):
        taps = [up_ref[slot, base - (CONV_W - 1 - k):base - (CONV_W - 1 - k) + hm, :]
                for k in range(CONV_W)]
        return cb + taps[0] * cw[0:1, :] + taps[1] * cw[1:2, :] + taps[2] * cw[2:3, :]

    for r in range(FFN_ROW_SPLIT):
        base = SUBLANES + r * hm
        val = conv(0, base, cwv_ref[...], cbv_ref[...])
        gate = conv(1, base, cwg_ref[...], cbg_ref[...])
        act = (val * (gate * _sigmoid(gate))).astype(BF16)
        acc_ref[r * hm:(r + 1) * hm, :] += _dot(act, wd_bf_ref[...])

    @pl.when(j == nj - 1)
    def _():
        o_ref[...] = x_ref[...] + _rms(acc_ref[...], g2_ref[...])


def conv_ffn(x, g1, w_up, conv_w, conv_b, w_down, layer, g2, *, seq, tm=ROW_TILE, tf=FFN_COL_TILE):
    m, d = x.shape
    f = w_down.shape[1]
    nj = f // tf
    tiles_per_batch = seq // tm
    conv_b = conv_b.reshape(conv_b.shape[0], 1, 2 * f)
    return pl.pallas_call(
        functools.partial(_conv_ffn_kernel, nj=nj, tiles_per_batch=tiles_per_batch),
        grid=(m // tm, nj),
        in_specs=[
            pl.BlockSpec((tm, d), lambda i, j: (i, 0)),
            pl.BlockSpec((1, d), lambda i, j: (0, 0)),
            _layer_spec(layer, (d, tf), lambda i, j: (0, j)),
            _layer_spec(layer, (d, tf), lambda i, j: (0, j + nj)),
            _layer_spec(layer, (CONV_W, tf), lambda i, j: (0, j)),
            _layer_spec(layer, (CONV_W, tf), lambda i, j: (0, j + nj)),
            _layer_spec(layer, (1, tf), lambda i, j: (0, j)),
            _layer_spec(layer, (1, tf), lambda i, j: (0, j + nj)),
            _layer_spec(layer, (tf, d), lambda i, j: (j, 0)),
            pl.BlockSpec((1, d), lambda i, j: (0, 0)),
        ],
        out_specs=pl.BlockSpec((tm, d), lambda i, j: (i, 0)),
        out_shape=jax.ShapeDtypeStruct((m, d), F32),
        scratch_shapes=[
            pltpu.VMEM((tm, d), BF16),
            pltpu.VMEM((tm, d), F32),
            pltpu.VMEM((nj, 2, SUBLANES, tf), F32),
            pltpu.VMEM((2, SUBLANES + tm, tf), F32),
            pltpu.VMEM((tf, d), BF16),
        ],
        compiler_params=_params("arbitrary", "arbitrary"),
        name="conv_ffn",
    )(x, g1.reshape(1, d), w_up, w_up, conv_w, conv_w, conv_b, conv_b, w_down, g2.reshape(1, d))


def kernel(x, mem, norm_mix, norm_mem, norm_ffn, mem_in_norm, a_w_in, a_lam_re, a_lam_im, a_log_dt,
           a_b_re, a_b_im, a_c_re, a_c_im, a_d, a_w_glu, kv_norm, w_k, w_v, b_w_q, b_rel_bias, b_w_o,
           m_w_q, m_w_kv, m_w_o, f_w_up, f_conv_w, f_conv_b, f_w_down):
    bsz, seq, d = x.shape
    depth = norm_mix.shape[0]
    n_a = a_w_in.shape[0]
    n_mem = mem.shape[1]
    md = m_w_q.shape[2]
    assert seq % ROW_TILE == 0 and seq % (ATT_QGROUP * ATT_CHUNK) == 0
    assert (seq // S5_CHUNK) % LANES == 0 and (d // S5_GROUP) % 2 == 0

    xf = x.reshape(bsz * seq, d)

    w_kv_mem = m_w_kv.astype(BF16)
    w_kv = jnp.stack([w_k, w_v]).astype(BF16)
    a_w_in_t = jnp.transpose(a_w_in, (0, 2, 1)).astype(BF16)
    a_w_glu_b = a_w_glu.astype(BF16)
    b_w_q_b = (b_w_q * (ATT_HEAD_DIM ** -0.5)).astype(BF16)
    b_w_o_b = b_w_o.astype(BF16)
    m_w_q_b = (m_w_q * (MEM_HEAD_DIM ** -0.5)).astype(BF16)
    m_w_o_b = m_w_o.astype(BF16)
    f_w_up_b = f_w_up.astype(BF16)
    flat = lambda a: a.reshape((-1,) + a.shape[2:])
    s5_ops = _s5_operators(flat(a_lam_re), flat(a_lam_im), flat(a_log_dt), flat(a_b_re),
                           flat(a_b_im), flat(a_c_re), flat(a_c_im), a_d, bsz)
    s5_ops = [o.reshape((n_a, o.shape[0] // n_a) + o.shape[1:]) for o in s5_ops[:3]] + [
        o.reshape(n_a, 1, -1) for o in s5_ops[3:]]
    att_bias = jax.vmap(_band_bias)(b_rel_bias)

    kvmem = norm_matmul(mem.reshape(bsz * n_mem, d), mem_in_norm, w_kv_mem, None)

    kv = None
    for l in range(depth):
        update = None
        if l == n_a:
            kv = norm_matmul(xf, kv_norm, w_kv, None)
        if l < n_a:
            ut = norm_matmul_t(xf, norm_mix[l, 0], a_w_in_t, l, bsz=bsz, seq=seq)
            yt = s5_mixer(ut, bsz, seq, s5_ops, l)
            update = (glu_gate(yt, a_w_glu_b, l, bsz=bsz, seq=seq), norm_mix[l, 1])
        else:
            jb = l - n_a
            q = norm_matmul(xf, norm_mix[l, 0], b_w_q_b, jb)
            o = band_attention(q, kv, att_bias, jb, bsz=bsz, seq=seq)
            xf = proj_res(o, b_w_o_b, jb, xf, norm_mix[l, 1])
        xf = mem_attention(xf, norm_mem[l, 0], m_w_q_b, kvmem, l, m_w_o_b, norm_mem[l, 1],
                           seq=seq, n_mem=n_mem, update=update)
        xf = conv_ffn(xf, norm_ffn[l, 0], f_w_up_b, f_conv_w, f_conv_b, f_w_down, l,
                      norm_ffn[l, 1], seq=seq)
    return xf.reshape(bsz, seq, d)
```

```python
import functools

import numpy as np
import jax
import jax.numpy as jnp
from jax import lax
from jax.experimental import pallas as pl
from jax.experimental.pallas import tpu as pltpu

EPS = 1e-6
NEG_INF = -1e30
BF16 = jnp.bfloat16
F32 = jnp.float32

S5_GROUP = 16
S5_CHUNK = 16
S5_POS_PER_STEP = 8
ATT_CHUNK = 64
ATT_LEFT = 8
ATT_HEAD_DIM = 128
ATT_MAX_REL = 256
ATT_QGROUP = 4
MEM_HEAD_DIM = 128
CONV_W = 3

VMEM_LIMIT_BYTES = 56 * 1024 * 1024
ROW_TILE = 512
PROJ_ROW_TILE = 256
COL_TILE = 2048
FFN_ROW_SPLIT = 2
FFN_COL_TILE = 512
GLU_COL_TILE = 1024
SUBLANES = 8
LANES = 128

_NT = (((1,), (1,)), ((), ()))


def _params(*sem):
    return pltpu.CompilerParams(dimension_semantics=sem, vmem_limit_bytes=VMEM_LIMIT_BYTES)


def _rms(x, g):
    ms = jnp.mean(x * x, axis=-1, keepdims=True)
    return x * lax.rsqrt(ms + EPS) * g


def _sigmoid(x):
    return 1.0 / (1.0 + jnp.exp(-x))


def _dot(a, b):
    return jnp.dot(a, b, preferred_element_type=F32)


def _dot_nt(a, b):
    return lax.dot_general(a, b, _NT, preferred_element_type=F32)


def _layer_spec(layer, block, index_map):
    return pl.BlockSpec((None,) + tuple(block), lambda *ids: (layer,) + tuple(index_map(*ids)))


def _norm_matmul_kernel(x_ref, g_ref, w_ref, o_ref, h_ref):
    @pl.when(pl.program_id(1) == 0)
    def _():
        h_ref[...] = _rms(x_ref[...], g_ref[...]).astype(h_ref.dtype)

    o_ref[...] = _dot(h_ref[...], w_ref[...]).astype(o_ref.dtype)


def norm_matmul(x, g, w, layer, *, tm=ROW_TILE, tn=COL_TILE, out_dtype=BF16):
    m, k = x.shape
    if layer is None:
        tn = w.shape[2]
        n = w.shape[0] * tn
        w_spec = pl.BlockSpec((None, k, tn), lambda i, j: (j, 0, 0))
    else:
        n = w.shape[2]
        tn = min(tn, n)
        w_spec = _layer_spec(layer, (k, tn), lambda i, j: (0, j))
    return pl.pallas_call(
        _norm_matmul_kernel,
        grid=(m // tm, n // tn),
        in_specs=[
            pl.BlockSpec((tm, k), lambda i, j: (i, 0)),
            pl.BlockSpec((1, k), lambda i, j: (0, 0)),
            w_spec,
        ],
        out_specs=pl.BlockSpec((tm, tn), lambda i, j: (i, j)),
        out_shape=jax.ShapeDtypeStruct((m, n), out_dtype),
        scratch_shapes=[pltpu.VMEM((tm, k), BF16)],
        compiler_params=_params("parallel", "arbitrary"),
        name="norm_matmul",
    )(x, g.reshape(1, k), w)


def _s5_build_kernel(rev_ref, bb_ref, cc_ref, cpad_ref, pt_ref, d_ref, t_ref, w_ref, v_ref):
    n_l = S5_CHUNK
    n_h = S5_GROUP
    lh = n_l * n_h
    n_p = rev_ref.shape[2]

    def exact_dot(a, b):
        return jnp.dot(a, b, precision=lax.Precision.HIGHEST, preferred_element_type=F32)

    def spread(a, pattern):
        hi = a.astype(BF16)
        rest = a - hi.astype(F32)
        mid = rest.astype(BF16)
        lo = (rest - mid.astype(F32)).astype(BF16)
        return _dot(hi, pattern) + _dot(mid, pattern) + _dot(lo, pattern)

    lane = lax.broadcasted_iota(jnp.int32, (n_l, lh), 1)
    row = lax.broadcasted_iota(jnp.int32, (n_l, lh), 0)
    per_step = jnp.where(lane // n_h == row, 1.0, 0.0).astype(BF16)
    per_chan = jnp.where(lane % n_h == row, 1.0, 0.0).astype(BF16)
    sub = lax.broadcasted_iota(jnp.int32, (n_h, lh), 0)
    lan = lax.broadcasted_iota(jnp.int32, (n_h, lh), 1)
    no_state = jnp.zeros((n_p, lh), BF16)
    for k in range(2):
        r_re = spread(rev_ref[k, 0], per_step)
        r_im = spread(rev_ref[k, 1], per_step)
        b_re = spread(bb_ref[k, 0], per_chan)
        b_im = spread(bb_ref[k, 1], per_chan)
        w_re = r_re * b_re - r_im * b_im
        w_im = r_re * b_im + r_im * b_re
        for ri, w in enumerate((w_re, w_im)):
            for slot in range(2):
                w_ref[0, k, ri, slot * n_p:(slot + 1) * n_p, :] = (
                    w.astype(BF16) if slot == k else no_state)
        krev = exact_dot(cc_ref[k, 0], w_re) - exact_dot(cc_ref[k, 1], w_im)
        skip = d_ref[k]
        for t in range(n_l):
            sh = (n_l - 1 - t) * n_h
            slab = krev
            if sh:
                slab = jnp.where(lan < lh - sh, pltpu.roll(krev, lh - sh, axis=1), 0.0)
            slab = slab + jnp.where(lan == t * n_h + sub, skip, 0.0)
            t_ref[k, t * n_h:(t + 1) * n_h, :] = slab.astype(BF16)
        c_re = cpad_ref[k, 0]
        c_im = cpad_ref[k, 1]
        for t in range(n_l):
            p_re = pt_ref[k, 0, t + 1:t + 2, :]
            p_im = pt_ref[k, 1, t + 1:t + 2, :]
            v_ref[0, k, 0, t * n_h:(t + 1) * n_h, :] = (c_re * p_re - c_im * p_im).astype(BF16)
            v_ref[0, k, 1, t * n_h:(t + 1) * n_h, :] = (-(c_re * p_im + c_im * p_re)).astype(BF16)


def _s5_operators(lam_re, lam_im, log_dt, b_re, b_im, c_re, c_im, d_skip, bsz):
    n_g, n_p = lam_re.shape
    n_h = S5_GROUP
    n_l = S5_CHUNK
    lh = n_l * n_h
    lr = lam_re.astype(F32)
    li = lam_im.astype(F32)
    dt = jnp.exp(log_dt.astype(F32))[:, None]
    mag = jnp.exp(lr * dt)
    ang = li * dt
    ab_re = mag * jnp.cos(ang)
    ab_im = mag * jnp.sin(ang)
    den = lr * lr + li * li
    nr = ab_re - 1.0
    f_re = (nr * lr + ab_im * li) / den
    f_im = (ab_im * lr - nr * li) / den
    br = b_re.astype(F32)
    bi = b_im.astype(F32)
    bb = jnp.stack([f_re[..., None] * br - f_im[..., None] * bi,
                    f_re[..., None] * bi + f_im[..., None] * br], axis=1)
    tau = jnp.arange(n_l + 1, dtype=F32)
    pw_re = jnp.exp((lr * dt)[..., None] * tau) * jnp.cos(ang[..., None] * tau)
    pw_im = jnp.exp((lr * dt)[..., None] * tau) * jnp.sin(ang[..., None] * tau)
    tau_m = tau[None, :, None]
    pt = jnp.stack([jnp.exp((lr * dt)[:, None, :] * tau_m) * jnp.cos(ang[:, None, :] * tau_m),
                    jnp.exp((lr * dt)[:, None, :] * tau_m) * jnp.sin(ang[:, None, :] * tau_m)],
                   axis=1)
    rev = jnp.stack([pw_re[:, :, n_l - 1::-1], pw_im[:, :, n_l - 1::-1]], axis=1)
    cc = jnp.stack([c_re.astype(F32), c_im.astype(F32)], axis=1)
    slot = (jnp.arange(n_g) % 2)[:, None, None, None]

    def lane_place(a):
        z = jnp.zeros_like(a)
        return jnp.where(slot == 0, jnp.concatenate([a, z], -1), jnp.concatenate([z, a], -1))

    skip = jnp.broadcast_to(d_skip.astype(F32).reshape(n_g, n_h, 1), (n_g, n_h, lh))
    n_pair = n_g // 2
    pair = lambda shape: pl.BlockSpec((2,) + shape, lambda j: (j,) + (0,) * len(shape))
    t_mat, w_pack, v_pack = pl.pallas_call(
        _s5_build_kernel,
        grid=(n_pair,),
        in_specs=[pair((2, n_p, n_l)), pair((2, n_p, n_h)), pair((2, n_h, n_p)),
                  pair((2, n_h, 2 * n_p)), pair((2, n_l + 1, 2 * n_p)), pair((n_h, lh))],
        out_specs=[
            pl.BlockSpec((2, lh, lh), lambda j: (j, 0, 0)),
            pl.BlockSpec((1, 2, 2, 2 * n_p, lh), lambda j: (j, 0, 0, 0, 0)),
            pl.BlockSpec((1, 2, 2, lh, 2 * n_p), lambda j: (j, 0, 0, 0, 0)),
        ],
        out_shape=[
            jax.ShapeDtypeStruct((n_g, lh, lh), BF16),
            jax.ShapeDtypeStruct((n_pair, 2, 2, 2 * n_p, lh), BF16),
            jax.ShapeDtypeStruct((n_pair, 2, 2, lh, 2 * n_p), BF16),
        ],
        compiler_params=_params("parallel"),
        name="s5_build_operators",
    )(rev, bb, cc, lane_place(cc), lane_place(pt), skip)

    def state_cols(a):
        a = jnp.broadcast_to(a.reshape(n_pair, 1, 2 * n_p), (n_pair, bsz, 2 * n_p))
        return a.reshape(1, n_g * bsz * n_p)
    return t_mat, w_pack, v_pack, state_cols(pw_re[:, :, n_l]), state_cols(pw_im[:, :, n_l])


def _chunk_view(x, bsz, seq):
    return x.reshape(bsz, seq // S5_CHUNK, S5_CHUNK, x.shape[1])


def _position_slab_copy(x_hbm, slab_ref, sem, b, pos, k):
    return pltpu.make_async_copy(x_hbm.at[b, :, pos, :], slab_ref.at[k], sem.at[k])


def _norm_matmul_t_kernel(x_hbm, g_ref, wt_ref, o_ref, slab_ref, h_ref, sem):
    b = pl.program_id(0)
    s = pl.program_id(1)
    sq, nc, _ = slab_ref.shape

    @pl.when(pl.program_id(2) == 0)
    def _():
        copies = [_position_slab_copy(x_hbm, slab_ref, sem, b, s * sq + k, k) for k in range(sq)]
        for c in copies:
            c.start()
        for k, c in enumerate(copies):
            c.wait()
            h_ref[k * nc:(k + 1) * nc, :] = _rms(slab_ref[k], g_ref[...]).astype(BF16)

    o_ref[...] = _dot_nt(wt_ref[...], h_ref[...]).astype(o_ref.dtype)


def norm_matmul_t(x, g, wt, layer, *, bsz, seq, tn=1024):
    d = x.shape[1]
    n = wt.shape[1]
    tn = min(tn, n)
    nc = seq // S5_CHUNK
    sq = S5_POS_PER_STEP
    steps = S5_CHUNK // sq
    return pl.pallas_call(
        _norm_matmul_t_kernel,
        grid=(bsz, steps, n // tn),
        in_specs=[
            pl.BlockSpec(memory_space=pl.ANY),
            pl.BlockSpec((1, d), lambda b, s, j: (0, 0)),
            _layer_spec(layer, (tn, d), lambda b, s, j: (j, 0)),
        ],
        out_specs=pl.BlockSpec((tn, sq * nc), lambda b, s, j: (j, b * steps + s)),
        out_shape=jax.ShapeDtypeStruct((n, bsz * seq), BF16),
        scratch_shapes=[
            pltpu.VMEM((sq, nc, d), x.dtype),
            pltpu.VMEM((sq * nc, d), BF16),
            pltpu.SemaphoreType.DMA((sq,)),
        ],
        compiler_params=_params("arbitrary", "arbitrary", "arbitrary"),
        name="norm_matmul_t",
    )(_chunk_view(x, bsz, seq), g.reshape(1, d), wt)


def _chunk_operand(u_ref, k, nc):
    h = S5_GROUP
    return jnp.concatenate(
        [u_ref[k * h:(k + 1) * h, s * nc:(s + 1) * nc] for s in range(S5_CHUNK)], axis=0)


def _s5_inject_kernel(u_ref, w_ref, ore_ref, oim_ref, *, bsz, seq):
    nc = ore_ref.shape[0]
    p2 = w_ref.shape[3]
    for b in range(bsz):
        ub = u_ref.at[:, b * seq:(b + 1) * seq]
        u0 = _chunk_operand(ub, 0, nc)
        u1 = _chunk_operand(ub, 1, nc)
        cols = slice(b * p2, (b + 1) * p2)
        ore_ref[:, cols] = (_dot(w_ref[0, 0, 0], u0) + _dot(w_ref[0, 1, 0], u1)).T
        oim_ref[:, cols] = (_dot(w_ref[0, 0, 1], u0) + _dot(w_ref[0, 1, 1], u1)).T


def s5_inject(ut, w_pack, layer, *, bsz, seq):
    n_pair = w_pack.shape[1]
    p2 = w_pack.shape[4]
    nc = seq // S5_CHUNK
    out = jax.ShapeDtypeStruct((nc, n_pair * bsz * p2), F32)
    oblk = pl.BlockSpec((nc, bsz * p2), lambda j: (0, j))
    return pl.pallas_call(
        functools.partial(_s5_inject_kernel, bsz=bsz, seq=seq),
        grid=(n_pair,),
        in_specs=[
            pl.BlockSpec((2 * S5_GROUP, bsz * seq), lambda j: (j, 0)),
            _layer_spec(layer, (1,) + w_pack.shape[2:], lambda j: (j, 0, 0, 0, 0)),
        ],
        out_specs=[oblk, oblk],
        out_shape=[out, out],
        compiler_params=_params("parallel"),
        name="s5_inject",
    )(ut, w_pack)


def _s5_scan_kernel(wre_ref, wim_ref, are_ref, aim_ref, xre_ref, xim_ref, *, n_c):
    a_re = are_ref[...]
    a_im = aim_ref[...]

    def body(c, carry):
        s_re, s_im = carry
        row = pl.ds(c, 1)
        xre_ref[row, :] = s_re
        xim_ref[row, :] = s_im
        n_re = a_re * s_re - a_im * s_im + wre_ref[row, :]
        n_im = a_re * s_im + a_im * s_re + wim_ref[row, :]
        return n_re, n_im

    zero = jnp.zeros(a_re.shape, F32)
    lax.fori_loop(0, n_c, body, (zero, zero))


def s5_scan(w_re, w_im, a_re, a_im, layer, *, tl=2048):
    n_c, cols = w_re.shape
    tl = min(tl, cols)
    out = jax.ShapeDtypeStruct((n_c, cols), F32)
    blk = pl.BlockSpec((n_c, tl), lambda j: (0, j))
    coef = _layer_spec(layer, (1, tl), lambda j: (0, j))
    return pl.pallas_call(
        functools.partial(_s5_scan_kernel, n_c=n_c),
        grid=(cols // tl,),
        in_specs=[blk, blk, coef, coef],
        out_specs=[blk, blk],
        out_shape=[out, out],
        compiler_params=_params("parallel"),
        name="s5_scan",
    )(w_re, w_im, a_re, a_im)


def _gelu_tanh(y):
    c = np.float32(np.sqrt(2.0 / np.pi))
    return 0.5 * y * (1.0 + jnp.tanh(c * (y + 0.044715 * (y * y * y))))


def _s5_out_kernel(u_ref, t_ref, xre_ref, xim_ref, v_ref, o_ref, *, bsz, seq):
    nc = xre_ref.shape[0]
    p2 = v_ref.shape[-1]
    h = S5_GROUP
    for b in range(bsz):
        ub = u_ref.at[:, b * seq:(b + 1) * seq]
        x_re = xre_ref[:, b * p2:(b + 1) * p2].astype(BF16)
        x_im = xim_ref[:, b * p2:(b + 1) * p2].astype(BF16)
        for k in range(2):
            y = (_dot(t_ref[k], _chunk_operand(ub, k, nc))
                 + _dot_nt(v_ref[0, k, 0], x_re) + _dot_nt(v_ref[0, k, 1], x_im))
            y = _gelu_tanh(y).astype(o_ref.dtype)
            for t in range(S5_CHUNK):
                lanes = slice(b * seq + t * nc, b * seq + (t + 1) * nc)
                o_ref[k * h:(k + 1) * h, lanes] = y[t * h:(t + 1) * h, :]


def s5_out(ut, t_mat, x_re, x_im, v_pack, layer, *, bsz, seq):
    n_pair = v_pack.shape[1]
    p2 = v_pack.shape[-1]
    nc = seq // S5_CHUNK
    lh = t_mat.shape[-1]
    xblk = pl.BlockSpec((nc, bsz * p2), lambda j: (0, j))
    ublk = pl.BlockSpec((2 * S5_GROUP, bsz * seq), lambda j: (j, 0))
    return pl.pallas_call(
        functools.partial(_s5_out_kernel, bsz=bsz, seq=seq),
        grid=(n_pair,),
        in_specs=[
            ublk,
            _layer_spec(layer, (2, lh, lh), lambda j: (j, 0, 0)),
            xblk,
            xblk,
            _layer_spec(layer, (1,) + v_pack.shape[2:], lambda j: (j, 0, 0, 0, 0)),
        ],
        out_specs=ublk,
        out_shape=jax.ShapeDtypeStruct(ut.shape, BF16),
        compiler_params=_params("parallel"),
        name="s5_out",
    )(ut, t_mat, x_re, x_im, v_pack)


def s5_mixer(ut, bsz, seq, ops, layer):
    t_mat, w_pack, v_pack, a_re, a_im = ops
    w_re, w_im = s5_inject(ut, w_pack, layer, bsz=bsz, seq=seq)
    x_re, x_im = s5_scan(w_re, w_im, a_re, a_im, layer)
    return s5_out(ut, t_mat, x_re, x_im, v_pack, layer, bsz=bsz, seq=seq)


def _glu_kernel(yt_ref, wv_ref, wg_ref, o_ref, y_ref):
    nc = o_ref.shape[0]

    @pl.when(pl.program_id(2) == 0)
    def _():
        for k in range(S5_POS_PER_STEP):
            y_ref[k * nc:(k + 1) * nc, :] = (
                yt_ref[:, k * nc:(k + 1) * nc].astype(F32).T.astype(BF16))

    y = y_ref[...]
    m = _dot(y, wv_ref[...]) * _sigmoid(_dot(y, wg_ref[...]))
    for k in range(S5_POS_PER_STEP):
        o_ref[:, k, :] = m[k * nc:(k + 1) * nc, :]


def glu_gate(yt, w_glu, layer, *, bsz, seq, tn=GLU_COL_TILE):
    k, _ = yt.shape
    d = w_glu.shape[2] // 2
    tn = min(tn, d)
    nj = d // tn
    nc = seq // S5_CHUNK
    sq = S5_POS_PER_STEP
    steps = S5_CHUNK // sq
    out = pl.pallas_call(
        _glu_kernel,
        grid=(bsz, steps, nj),
        in_specs=[
            pl.BlockSpec((k, sq * nc), lambda b, s, j: (0, b * steps + s)),
            _layer_spec(layer, (k, tn), lambda b, s, j: (0, j)),
            _layer_spec(layer, (k, tn), lambda b, s, j: (0, j + nj)),
        ],
        out_specs=pl.BlockSpec((None, nc, sq, tn), lambda b, s, j: (b, 0, s, j)),
        out_shape=jax.ShapeDtypeStruct((bsz, nc, S5_CHUNK, d), F32),
        scratch_shapes=[pltpu.VMEM((sq * nc, k), BF16)],
        compiler_params=_params("parallel", "parallel", "arbitrary"),
        name="glu_gate",
    )(yt, w_glu, w_glu)
    return out.reshape(bsz * seq, d)


def _band_bias(rel_bias):
    c = ATT_CHUNK
    nq = ATT_QGROUP * c
    nk = (ATT_QGROUP + ATT_LEFT) * c
    r = rel_bias.astype(F32)
    n_h = r.shape[0]
    d_lo = ATT_LEFT * c - (nk - 1)
    d_hi = ATT_LEFT * c + nq - 1
    n_lo = -(c - 1) - d_lo
    n_hi = d_hi - ATT_MAX_REL
    e = jnp.concatenate([jnp.broadcast_to(r[:, :1], (n_h, n_lo)), r,
                         jnp.broadcast_to(r[:, -1:], (n_h, n_hi))], axis=1)
    assert e.shape[1] == nq + nk - 1
    width = 2 * c - 1
    starts = [nk - 1 - (c - 1) - n * c for n in range(ATT_LEFT + 1)]
    v = jnp.stack([e[:, st:st + width] for st in starts], axis=1)
    frev = jnp.concatenate([v[:, :, ::-1], jnp.zeros(v.shape[:2] + (1,), F32)], axis=2)
    skew = jnp.tile(frev, (1, 1, c))[:, :, :c * width].reshape(n_h, ATT_LEFT + 1, c, width)
    blocks = skew[:, :, :, c - 1:]
    masked = jnp.full((n_h, c, c), NEG_INF, F32)
    rows = [jnp.concatenate([blocks[:, kb - qa] if 0 <= kb - qa <= ATT_LEFT else masked
                             for kb in range(ATT_QGROUP + ATT_LEFT)], axis=-1)
            for qa in range(ATT_QGROUP)]
    return jnp.concatenate(rows, axis=-2)


def _band_attn_kernel(q_ref, k_ref, v_ref, b_ref, o_ref, s_ref, p_ref, v1_ref, *, seq):
    c = ATT_CHUNK
    dh = ATT_HEAD_DIM
    nq = ATT_QGROUP * c
    nk = (ATT_QGROUP + ATT_LEFT) * c
    v1_ref[:, :dh] = v_ref[...]
    v1_ref[:, dh:] = jnp.ones((seq, dh), BF16)
    for gi in range(seq // nq):
        q0 = gi * nq
        off = max(0, ATT_LEFT * c - q0)
        k0 = q0 - ATT_LEFT * c + off
        w = nk - off
        s_ref[:, :w] = _dot_nt(q_ref[q0:q0 + nq, :], k_ref[k0:k0 + w, :])
        for a in range(ATT_QGROUP):
            rows = slice(a * c, (a + 1) * c)
            lo = max((a * c) // LANES * LANES, off)
            hi = min(-(-((a + ATT_LEFT + 1) * c) // LANES) * LANES, nk)
            s = s_ref[rows, lo - off:hi - off] + b_ref[0, rows, lo:hi]
            m = jnp.max(s, axis=-1, keepdims=True)
            p_ref[rows, lo - off:hi - off] = jnp.exp(s - m).astype(BF16)
            if lo > off:
                p_ref[rows, 0:lo - off] = jnp.zeros((c, lo - off), BF16)
            if hi < nk:
                p_ref[rows, hi - off:w] = jnp.zeros((c, nk - hi), BF16)
        o2 = _dot(p_ref[:, :w], v1_ref[k0:k0 + w, :])
        o_ref[q0:q0 + nq, :] = (o2[:, :dh] / o2[:, dh:]).astype(o_ref.dtype)


def band_attention(q, kv, bias, layer, *, bsz, seq):
    m, d = q.shape
    n_h = d // ATT_HEAD_DIM
    dh = ATT_HEAD_DIM
    nq, nk = bias.shape[2:]
    return pl.pallas_call(
        functools.partial(_band_attn_kernel, seq=seq),
        grid=(bsz, n_h),
        in_specs=[
            pl.BlockSpec((seq, dh), lambda b, h: (b, h)),
            pl.BlockSpec((seq, dh), lambda b, h: (b, h)),
            pl.BlockSpec((seq, dh), lambda b, h: (b, h + n_h)),
            _layer_spec(layer, (1, nq, nk), lambda b, h: (h, 0, 0)),
        ],
        out_specs=pl.BlockSpec((seq, dh), lambda b, h: (b, h)),
        out_shape=jax.ShapeDtypeStruct((m, d), BF16),
        scratch_shapes=[
            pltpu.VMEM((nq, nk), F32),
            pltpu.VMEM((nq, nk), BF16),
            pltpu.VMEM((seq, 2 * dh), BF16),
        ],
        compiler_params=_params("parallel", "parallel"),
        name="band_attention",
    )(q, kv, kv, bias)


def _mem_attn_kernel(*refs, mixer):
    if mixer == "update":
        x_ref, r_ref, gu_ref, g1_ref, wq_ref, k_ref, v_ref, wo_ref, g2_ref, o_ref = refs
        x = x_ref[...] + _rms(r_ref[...], gu_ref[...])
    else:
        x_ref, a_ref, wp_ref, gu_ref, g1_ref, wq_ref, k_ref, v_ref, wo_ref, g2_ref, o_ref = refs
        x = x_ref[...] + _rms(_dot(a_ref[...], wp_ref[...]), gu_ref[...])
    h = _rms(x, g1_ref[...]).astype(BF16)
    q = _dot(h, wq_ref[...]).astype(BF16)
    k = k_ref[...]
    v = v_ref[...]
    dh = MEM_HEAD_DIM
    n_heads = q.shape[1] // dh
    scores = [_dot_nt(q[:, hd * dh:(hd + 1) * dh], k[:, hd * dh:(hd + 1) * dh])
              for hd in range(n_heads)]
    ones = jnp.ones((v.shape[0], dh), BF16)
    heads = []
    for hd in range(n_heads):
        s = scores[hd]
        p = jnp.exp(s - jnp.max(s, axis=-1, keepdims=True)).astype(BF16)
        o2 = _dot(p, jnp.concatenate([v[:, hd * dh:(hd + 1) * dh], ones], axis=-1))
        heads.append((o2[:, :dh] / o2[:, dh:]).astype(BF16))
    o = jnp.concatenate(heads, axis=-1)
    c = _dot(o, wo_ref[...])
    o_ref[...] = x + _rms(c, g2_ref[...])


def mem_attention(x, mixer, g1, wq, kvmem, layer, wo, g2, *, seq, n_mem, tm):
    m, d = x.shape
    md = wq.shape[2]
    tiles_per_batch = seq // tm
    row = pl.BlockSpec((tm, d), lambda i: (i, 0))
    gain = pl.BlockSpec((1, d), lambda i: (0, 0))
    if mixer[0] == "update":
        _, r, gu = mixer
        acts, act_specs = [x, r, gu.reshape(1, d)], [row, row, gain]
    else:
        _, a, wp, wp_layer, gu = mixer
        kp = a.shape[1]
        acts = [x, a, wp, gu.reshape(1, d)]
        act_specs = [row, pl.BlockSpec((tm, kp), lambda i: (i, 0)),
                     _layer_spec(wp_layer, (kp, d), lambda i: (0, 0)), gain]
    return pl.pallas_call(
        functools.partial(_mem_attn_kernel, mixer=mixer[0]),
        grid=(m // tm,),
        in_specs=act_specs + [
            pl.BlockSpec((1, d), lambda i: (0, 0)),
            _layer_spec(layer, (d, md), lambda i: (0, 0)),
            pl.BlockSpec((n_mem, md), lambda i: (i // tiles_per_batch, 2 * layer)),
            pl.BlockSpec((n_mem, md), lambda i: (i // tiles_per_batch, 2 * layer + 1)),
            _layer_spec(layer, (md, d), lambda i: (0, 0)),
            pl.BlockSpec((1, d), lambda i: (0, 0)),
        ],
        out_specs=row,
        out_shape=jax.ShapeDtypeStruct((m, d), F32),
        compiler_params=_params("parallel"),
        name="mem_attention",
    )(*acts, g1.reshape(1, d), wq, kvmem, kvmem, wo, g2.reshape(1, d))


def _conv_ffn_kernel(x_ref, g1_ref, wuv_ref, wug_ref, cwv_ref, cwg_ref, cbv_ref, cbg_ref, wd_ref,
                     g2_ref, o_ref, h_ref, acc_ref, tail_ref, up_ref, wd_bf_ref, *, nj,
                     tiles_per_batch):
    i = pl.program_id(0)
    j = pl.program_id(1)
    tm = x_ref.shape[0]
    hm = tm // FFN_ROW_SPLIT

    @pl.when(j == 0)
    def _():
        h_ref[...] = _rms(x_ref[...], g1_ref[...]).astype(h_ref.dtype)
        acc_ref[...] = jnp.zeros(acc_ref.shape, F32)

    @pl.when(i % tiles_per_batch == 0)
    def _():
        tail_ref[j] = jnp.zeros(tail_ref.shape[1:], F32)

    up_ref[:, 0:SUBLANES, :] = tail_ref[j]
    wd_bf_ref[...] = wd_ref[...].astype(BF16)

    for r in range(FFN_ROW_SPLIT):
        h = h_ref[r * hm:(r + 1) * hm, :]
        rows = slice(SUBLANES + r * hm, SUBLANES + (r + 1) * hm)
        up_ref[0, rows, :] = _dot(h, wuv_ref[...])
        up_ref[1, rows, :] = _dot(h, wug_ref[...])

    tail_ref[j] = up_ref[:, tm:tm + SUBLANES, :]

    def conv(slot, base, cw, cb):
        taps = [up_ref[slot, base - (CONV_W - 1 - k):base - (CONV_W - 1 - k) + hm, :]
                for k in range(CONV_W)]
        return cb + taps[0] * cw[0:1, :] + taps[1] * cw[1:2, :] + taps[2] * cw[2:3, :]

    for r in range(FFN_ROW_SPLIT):
        base = SUBLANES + r * hm
        val = conv(0, base, cwv_ref[...], cbv_ref[...])
        gate = conv(1, base, cwg_ref[...], cbg_ref[...])
        act = (val * (gate * _sigmoid(gate))).astype(BF16)
        acc_ref[r * hm:(r + 1) * hm, :] += _dot(act, wd_bf_ref[...])

    @pl.when(j == nj - 1)
    def _():
        o_ref[...] = x_ref[...] + _rms(acc_ref[...], g2_ref[...])


def conv_ffn(x, g1, w_up, conv_w, conv_b, w_down, layer, g2, *, seq, tm=ROW_TILE, tf=FFN_COL_TILE):
    m, d = x.shape
    f = w_down.shape[1]
    nj = f // tf
    tiles_per_batch = seq // tm
    conv_b = conv_b.reshape(conv_b.shape[0], 1, 2 * f)
    return pl.pallas_call(
        functools.partial(_conv_ffn_kernel, nj=nj, tiles_per_batch=tiles_per_batch),
        grid=(m // tm, nj),
        in_specs=[
            pl.BlockSpec((tm, d), lambda i, j: (i, 0)),
            pl.BlockSpec((1, d), lambda i, j: (0, 0)),
            _layer_spec(layer, (d, tf), lambda i, j: (0, j)),
            _layer_spec(layer, (d, tf), lambda i, j: (0, j + nj)),
            _layer_spec(layer, (CONV_W, tf), lambda i, j: (0, j)),
            _layer_spec(layer, (CONV_W, tf), lambda i, j: (0, j + nj)),
            _layer_spec(layer, (1, tf), lambda i, j: (0, j)),
            _layer_spec(layer, (1, tf), lambda i, j: (0, j + nj)),
            _layer_spec(layer, (tf, d), lambda i, j: (j, 0)),
            pl.BlockSpec((1, d), lambda i, j: (0, 0)),
        ],
        out_specs=pl.BlockSpec((tm, d), lambda i, j: (i, 0)),
        out_shape=jax.ShapeDtypeStruct((m, d), F32),
        scratch_shapes=[
            pltpu.VMEM((tm, d), BF16),
            pltpu.VMEM((tm, d), F32),
            pltpu.VMEM((nj, 2, SUBLANES, tf), F32),
            pltpu.VMEM((2, SUBLANES + tm, tf), F32),
            pltpu.VMEM((tf, d), BF16),
        ],
        compiler_params=_params("arbitrary", "arbitrary"),
        name="conv_ffn",
    )(x, g1.reshape(1, d), w_up, w_up, conv_w, conv_w, conv_b, conv_b, w_down, g2.reshape(1, d))


def kernel(x, mem, norm_mix, norm_mem, norm_ffn, mem_in_norm, a_w_in, a_lam_re, a_lam_im, a_log_dt,
           a_b_re, a_b_im, a_c_re, a_c_im, a_d, a_w_glu, kv_norm, w_k, w_v, b_w_q, b_rel_bias, b_w_o,
           m_w_q, m_w_kv, m_w_o, f_w_up, f_conv_w, f_conv_b, f_w_down):
    bsz, seq, d = x.shape
    depth = norm_mix.shape[0]
    n_a = a_w_in.shape[0]
    n_mem = mem.shape[1]
    md = m_w_q.shape[2]
    assert seq % ROW_TILE == 0 and seq % (ATT_QGROUP * ATT_CHUNK) == 0
    assert (seq // S5_CHUNK) % LANES == 0 and (d // S5_GROUP) % 2 == 0

    xf = x.reshape(bsz * seq, d)

    w_kv_mem = m_w_kv.astype(BF16)
    w_kv = jnp.stack([w_k, w_v]).astype(BF16)
    a_w_in_t = jnp.transpose(a_w_in, (0, 2, 1)).astype(BF16)
    a_w_glu_b = a_w_glu.astype(BF16)
    b_w_q_b = (b_w_q * (ATT_HEAD_DIM ** -0.5)).astype(BF16)
    b_w_o_b = b_w_o.astype(BF16)
    m_w_q_b = (m_w_q * (MEM_HEAD_DIM ** -0.5)).astype(BF16)
    m_w_o_b = m_w_o.astype(BF16)
    f_w_up_b = f_w_up.astype(BF16)
    flat = lambda a: a.reshape((-1,) + a.shape[2:])
    s5_ops = _s5_operators(flat(a_lam_re), flat(a_lam_im), flat(a_log_dt), flat(a_b_re),
                           flat(a_b_im), flat(a_c_re), flat(a_c_im), a_d, bsz)
    s5_ops = [o.reshape((n_a, o.shape[0] // n_a) + o.shape[1:]) for o in s5_ops[:3]] + [
        o.reshape(n_a, 1, -1) for o in s5_ops[3:]]
    att_bias = jax.vmap(_band_bias)(b_rel_bias)

    kvmem = norm_matmul(mem.reshape(bsz * n_mem, d), mem_in_norm, w_kv_mem, None)

    kv = None
    for l in range(depth):
        if l == n_a:
            kv = norm_matmul(xf, kv_norm, w_kv, None)
        if l < n_a:
            ut = norm_matmul_t(xf, norm_mix[l, 0], a_w_in_t, l, bsz=bsz, seq=seq)
            yt = s5_mixer(ut, bsz, seq, s5_ops, l)
            mixer = ("update", glu_gate(yt, a_w_glu_b, l, bsz=bsz, seq=seq), norm_mix[l, 1])
            tm = ROW_TILE
        else:
            jb = l - n_a
            q = norm_matmul(xf, norm_mix[l, 0], b_w_q_b, jb)
            o = band_attention(q, kv, att_bias, jb, bsz=bsz, seq=seq)
            mixer = ("project", o, b_w_o_b, jb, norm_mix[l, 1])
            tm = PROJ_ROW_TILE
        xf = mem_attention(xf, mixer, norm_mem[l, 0], m_w_q_b, kvmem, l, m_w_o_b, norm_mem[l, 1],
                           seq=seq, n_mem=n_mem, tm=tm)
        xf = conv_ffn(xf, norm_ffn[l, 0], f_w_up_b, f_conv_w, f_conv_b, f_w_down, l,
                      norm_ffn[l, 1], seq=seq)
    return xf.reshape(bsz, seq, d)
```

```python
import functools

import numpy as np
import jax
import jax.numpy as jnp
from jax import lax
from jax.experimental import pallas as pl
from jax.experimental.pallas import tpu as pltpu

EPS = 1e-6
NEG_INF = -1e30
BF16 = jnp.bfloat16
F32 = jnp.float32

S5_GROUP = 16
S5_CHUNK = 16
S5_POS_PER_STEP = 8
ATT_CHUNK = 64
ATT_LEFT = 8
ATT_HEAD_DIM = 128
ATT_MAX_REL = 256
ATT_QGROUP = 4
MEM_HEAD_DIM = 128
CONV_W = 3

VMEM_LIMIT_BYTES = 56 * 1024 * 1024
ROW_TILE = 512
PROJ_ROW_TILE = 512
COL_TILE = 2048
FFN_ROW_SPLIT = 2
FFN_COL_TILE = 512
GLU_COL_TILE = 1024
SUBLANES = 8
LANES = 128

_NT = (((1,), (1,)), ((), ()))


def _params(*sem):
    return pltpu.CompilerParams(dimension_semantics=sem, vmem_limit_bytes=VMEM_LIMIT_BYTES)


def _rms(x, g):
    ms = jnp.mean(x * x, axis=-1, keepdims=True)
    return x * lax.rsqrt(ms + EPS) * g


def _sigmoid(x):
    return 1.0 / (1.0 + jnp.exp(-x))


def _dot(a, b):
    return jnp.dot(a, b, preferred_element_type=F32)


def _dot_nt(a, b):
    return lax.dot_general(a, b, _NT, preferred_element_type=F32)


def _layer_spec(layer, block, index_map):
    return pl.BlockSpec((None,) + tuple(block), lambda *ids: (layer,) + tuple(index_map(*ids)))


def _norm_matmul_kernel(x_ref, g_ref, w_ref, o_ref, h_ref):
    @pl.when(pl.program_id(1) == 0)
    def _():
        h_ref[...] = _rms(x_ref[...], g_ref[...]).astype(h_ref.dtype)

    o_ref[...] = _dot(h_ref[...], w_ref[...]).astype(o_ref.dtype)


def norm_matmul(x, g, w, layer, *, tm=ROW_TILE, tn=COL_TILE, out_dtype=BF16):
    m, k = x.shape
    if layer is None:
        tn = w.shape[2]
        n = w.shape[0] * tn
        w_spec = pl.BlockSpec((None, k, tn), lambda i, j: (j, 0, 0))
    else:
        n = w.shape[2]
        tn = min(tn, n)
        w_spec = _layer_spec(layer, (k, tn), lambda i, j: (0, j))
    return pl.pallas_call(
        _norm_matmul_kernel,
        grid=(m // tm, n // tn),
        in_specs=[
            pl.BlockSpec((tm, k), lambda i, j: (i, 0)),
            pl.BlockSpec((1, k), lambda i, j: (0, 0)),
            w_spec,
        ],
        out_specs=pl.BlockSpec((tm, tn), lambda i, j: (i, j)),
        out_shape=jax.ShapeDtypeStruct((m, n), out_dtype),
        scratch_shapes=[pltpu.VMEM((tm, k), BF16)],
        compiler_params=_params("parallel", "arbitrary"),
        name="norm_matmul",
    )(x, g.reshape(1, k), w)


def _s5_build_kernel(rev_ref, bb_ref, cc_ref, cpad_ref, pt_ref, d_ref, t_ref, w_ref, v_ref):
    n_l = S5_CHUNK
    n_h = S5_GROUP
    lh = n_l * n_h
    n_p = rev_ref.shape[2]

    def exact_dot(a, b):
        return jnp.dot(a, b, precision=lax.Precision.HIGHEST, preferred_element_type=F32)

    def spread(a, pattern):
        hi = a.astype(BF16)
        rest = a - hi.astype(F32)
        mid = rest.astype(BF16)
        lo = (rest - mid.astype(F32)).astype(BF16)
        return _dot(hi, pattern) + _dot(mid, pattern) + _dot(lo, pattern)

    lane = lax.broadcasted_iota(jnp.int32, (n_l, lh), 1)
    row = lax.broadcasted_iota(jnp.int32, (n_l, lh), 0)
    per_step = jnp.where(lane // n_h == row, 1.0, 0.0).astype(BF16)
    per_chan = jnp.where(lane % n_h == row, 1.0, 0.0).astype(BF16)
    sub = lax.broadcasted_iota(jnp.int32, (n_h, lh), 0)
    lan = lax.broadcasted_iota(jnp.int32, (n_h, lh), 1)
    no_state = jnp.zeros((n_p, lh), BF16)
    for k in range(2):
        r_re = spread(rev_ref[k, 0], per_step)
        r_im = spread(rev_ref[k, 1], per_step)
        b_re = spread(bb_ref[k, 0], per_chan)
        b_im = spread(bb_ref[k, 1], per_chan)
        w_re = r_re * b_re - r_im * b_im
        w_im = r_re * b_im + r_im * b_re
        for ri, w in enumerate((w_re, w_im)):
            for slot in range(2):
                w_ref[0, k, ri, slot * n_p:(slot + 1) * n_p, :] = (
                    w.astype(BF16) if slot == k else no_state)
        krev = exact_dot(cc_ref[k, 0], w_re) - exact_dot(cc_ref[k, 1], w_im)
        skip = d_ref[k]
        for t in range(n_l):
            sh = (n_l - 1 - t) * n_h
            slab = krev
            if sh:
                slab = jnp.where(lan < lh - sh, pltpu.roll(krev, lh - sh, axis=1), 0.0)
            slab = slab + jnp.where(lan == t * n_h + sub, skip, 0.0)
            t_ref[k, t * n_h:(t + 1) * n_h, :] = slab.astype(BF16)
        c_re = cpad_ref[k, 0]
        c_im = cpad_ref[k, 1]
        for t in range(n_l):
            p_re = pt_ref[k, 0, t + 1:t + 2, :]
            p_im = pt_ref[k, 1, t + 1:t + 2, :]
            v_ref[0, k, 0, t * n_h:(t + 1) * n_h, :] = (c_re * p_re - c_im * p_im).astype(BF16)
            v_ref[0, k, 1, t * n_h:(t + 1) * n_h, :] = (-(c_re * p_im + c_im * p_re)).astype(BF16)


def _s5_operators(lam_re, lam_im, log_dt, b_re, b_im, c_re, c_im, d_skip, bsz):
    n_g, n_p = lam_re.shape
    n_h = S5_GROUP
    n_l = S5_CHUNK
    lh = n_l * n_h
    lr = lam_re.astype(F32)
    li = lam_im.astype(F32)
    dt = jnp.exp(log_dt.astype(F32))[:, None]
    mag = jnp.exp(lr * dt)
    ang = li * dt
    ab_re = mag * jnp.cos(ang)
    ab_im = mag * jnp.sin(ang)
    den = lr * lr + li * li
    nr = ab_re - 1.0
    f_re = (nr * lr + ab_im * li) / den
    f_im = (ab_im * lr - nr * li) / den
    br = b_re.astype(F32)
    bi = b_im.astype(F32)
    bb = jnp.stack([f_re[..., None] * br - f_im[..., None] * bi,
                    f_re[..., None] * bi + f_im[..., None] * br], axis=1)
    tau = jnp.arange(n_l + 1, dtype=F32)
    pw_re = jnp.exp((lr * dt)[..., None] * tau) * jnp.cos(ang[..., None] * tau)
    pw_im = jnp.exp((lr * dt)[..., None] * tau) * jnp.sin(ang[..., None] * tau)
    tau_m = tau[None, :, None]
    pt = jnp.stack([jnp.exp((lr * dt)[:, None, :] * tau_m) * jnp.cos(ang[:, None, :] * tau_m),
                    jnp.exp((lr * dt)[:, None, :] * tau_m) * jnp.sin(ang[:, None, :] * tau_m)],
                   axis=1)
    rev = jnp.stack([pw_re[:, :, n_l - 1::-1], pw_im[:, :, n_l - 1::-1]], axis=1)
    cc = jnp.stack([c_re.astype(F32), c_im.astype(F32)], axis=1)
    slot = (jnp.arange(n_g) % 2)[:, None, None, None]

    def lane_place(a):
        z = jnp.zeros_like(a)
        return jnp.where(slot == 0, jnp.concatenate([a, z], -1), jnp.concatenate([z, a], -1))

    skip = jnp.broadcast_to(d_skip.astype(F32).reshape(n_g, n_h, 1), (n_g, n_h, lh))
    n_pair = n_g // 2
    pair = lambda shape: pl.BlockSpec((2,) + shape, lambda j: (j,) + (0,) * len(shape))
    t_mat, w_pack, v_pack = pl.pallas_call(
        _s5_build_kernel,
        grid=(n_pair,),
        in_specs=[pair((2, n_p, n_l)), pair((2, n_p, n_h)), pair((2, n_h, n_p)),
                  pair((2, n_h, 2 * n_p)), pair((2, n_l + 1, 2 * n_p)), pair((n_h, lh))],
        out_specs=[
            pl.BlockSpec((2, lh, lh), lambda j: (j, 0, 0)),
            pl.BlockSpec((1, 2, 2, 2 * n_p, lh), lambda j: (j, 0, 0, 0, 0)),
            pl.BlockSpec((1, 2, 2, lh, 2 * n_p), lambda j: (j, 0, 0, 0, 0)),
        ],
        out_shape=[
            jax.ShapeDtypeStruct((n_g, lh, lh), BF16),
            jax.ShapeDtypeStruct((n_pair, 2, 2, 2 * n_p, lh), BF16),
            jax.ShapeDtypeStruct((n_pair, 2, 2, lh, 2 * n_p), BF16),
        ],
        compiler_params=_params("parallel"),
        name="s5_build_operators",
    )(rev, bb, cc, lane_place(cc), lane_place(pt), skip)

    def state_cols(a):
        a = jnp.broadcast_to(a.reshape(n_pair, 1, 2 * n_p), (n_pair, bsz, 2 * n_p))
        return a.reshape(1, n_g * bsz * n_p)
    return t_mat, w_pack, v_pack, state_cols(pw_re[:, :, n_l]), state_cols(pw_im[:, :, n_l])


def _chunk_view(x, bsz, seq):
    return x.reshape(bsz, seq // S5_CHUNK, S5_CHUNK, x.shape[1])


def _position_slab_copy(x_hbm, slab_ref, sem, b, pos, k):
    return pltpu.make_async_copy(x_hbm.at[b, :, pos, :], slab_ref.at[k], sem.at[k])


def _norm_matmul_t_kernel(x_hbm, g_ref, wt_ref, o_ref, slab_ref, h_ref, sem):
    b = pl.program_id(0)
    s = pl.program_id(1)
    sq, nc, _ = slab_ref.shape

    @pl.when(pl.program_id(2) == 0)
    def _():
        copies = [_position_slab_copy(x_hbm, slab_ref, sem, b, s * sq + k, k) for k in range(sq)]
        for c in copies:
            c.start()
        for k, c in enumerate(copies):
            c.wait()
            h_ref[k * nc:(k + 1) * nc, :] = _rms(slab_ref[k], g_ref[...]).astype(BF16)

    o_ref[...] = _dot_nt(wt_ref[...], h_ref[...]).astype(o_ref.dtype)


def norm_matmul_t(x, g, wt, layer, *, bsz, seq, tn=1024):
    d = x.shape[1]
    n = wt.shape[1]
    tn = min(tn, n)
    nc = seq // S5_CHUNK
    sq = S5_POS_PER_STEP
    steps = S5_CHUNK // sq
    return pl.pallas_call(
        _norm_matmul_t_kernel,
        grid=(bsz, steps, n // tn),
        in_specs=[
            pl.BlockSpec(memory_space=pl.ANY),
            pl.BlockSpec((1, d), lambda b, s, j: (0, 0)),
            _layer_spec(layer, (tn, d), lambda b, s, j: (j, 0)),
        ],
        out_specs=pl.BlockSpec((tn, sq * nc), lambda b, s, j: (j, b * steps + s)),
        out_shape=jax.ShapeDtypeStruct((n, bsz * seq), BF16),
        scratch_shapes=[
            pltpu.VMEM((sq, nc, d), x.dtype),
            pltpu.VMEM((sq * nc, d), BF16),
            pltpu.SemaphoreType.DMA((sq,)),
        ],
        compiler_params=_params("arbitrary", "arbitrary", "arbitrary"),
        name="norm_matmul_t",
    )(_chunk_view(x, bsz, seq), g.reshape(1, d), wt)


def _chunk_operand(u_ref, k, nc):
    h = S5_GROUP
    return jnp.concatenate(
        [u_ref[k * h:(k + 1) * h, s * nc:(s + 1) * nc] for s in range(S5_CHUNK)], axis=0)


def _s5_inject_kernel(u_ref, w_ref, ore_ref, oim_ref, *, bsz, seq):
    nc = ore_ref.shape[0]
    p2 = w_ref.shape[3]
    for b in range(bsz):
        ub = u_ref.at[:, b * seq:(b + 1) * seq]
        u0 = _chunk_operand(ub, 0, nc)
        u1 = _chunk_operand(ub, 1, nc)
        cols = slice(b * p2, (b + 1) * p2)
        ore_ref[:, cols] = (_dot(w_ref[0, 0, 0], u0) + _dot(w_ref[0, 1, 0], u1)).T
        oim_ref[:, cols] = (_dot(w_ref[0, 0, 1], u0) + _dot(w_ref[0, 1, 1], u1)).T


def s5_inject(ut, w_pack, layer, *, bsz, seq):
    n_pair = w_pack.shape[1]
    p2 = w_pack.shape[4]
    nc = seq // S5_CHUNK
    out = jax.ShapeDtypeStruct((nc, n_pair * bsz * p2), F32)
    oblk = pl.BlockSpec((nc, bsz * p2), lambda j: (0, j))
    return pl.pallas_call(
        functools.partial(_s5_inject_kernel, bsz=bsz, seq=seq),
        grid=(n_pair,),
        in_specs=[
            pl.BlockSpec((2 * S5_GROUP, bsz * seq), lambda j: (j, 0)),
            _layer_spec(layer, (1,) + w_pack.shape[2:], lambda j: (j, 0, 0, 0, 0)),
        ],
        out_specs=[oblk, oblk],
        out_shape=[out, out],
        compiler_params=_params("parallel"),
        name="s5_inject",
    )(ut, w_pack)


def _s5_scan_kernel(wre_ref, wim_ref, are_ref, aim_ref, xre_ref, xim_ref, *, n_c):
    a_re = are_ref[...]
    a_im = aim_ref[...]

    def body(c, carry):
        s_re, s_im = carry
        row = pl.ds(c, 1)
        xre_ref[row, :] = s_re
        xim_ref[row, :] = s_im
        n_re = a_re * s_re - a_im * s_im + wre_ref[row, :]
        n_im = a_re * s_im + a_im * s_re + wim_ref[row, :]
        return n_re, n_im

    zero = jnp.zeros(a_re.shape, F32)
    lax.fori_loop(0, n_c, body, (zero, zero))


def s5_scan(w_re, w_im, a_re, a_im, layer, *, tl=2048):
    n_c, cols = w_re.shape
    tl = min(tl, cols)
    out = jax.ShapeDtypeStruct((n_c, cols), F32)
    blk = pl.BlockSpec((n_c, tl), lambda j: (0, j))
    coef = _layer_spec(layer, (1, tl), lambda j: (0, j))
    return pl.pallas_call(
        functools.partial(_s5_scan_kernel, n_c=n_c),
        grid=(cols // tl,),
        in_specs=[blk, blk, coef, coef],
        out_specs=[blk, blk],
        out_shape=[out, out],
        compiler_params=_params("parallel"),
        name="s5_scan",
    )(w_re, w_im, a_re, a_im)


def _gelu_tanh(y):
    c = np.float32(np.sqrt(2.0 / np.pi))
    return 0.5 * y * (1.0 + jnp.tanh(c * (y + 0.044715 * (y * y * y))))


def _s5_out_kernel(u_ref, t_ref, xre_ref, xim_ref, v_ref, o_ref, *, bsz, seq):
    nc = xre_ref.shape[0]
    p2 = v_ref.shape[-1]
    h = S5_GROUP
    for b in range(bsz):
        ub = u_ref.at[:, b * seq:(b + 1) * seq]
        x_re = xre_ref[:, b * p2:(b + 1) * p2].astype(BF16)
        x_im = xim_ref[:, b * p2:(b + 1) * p2].astype(BF16)
        for k in range(2):
            y = (_dot(t_ref[k], _chunk_operand(ub, k, nc))
                 + _dot_nt(v_ref[0, k, 0], x_re) + _dot_nt(v_ref[0, k, 1], x_im))
            y = _gelu_tanh(y).astype(o_ref.dtype)
            for t in range(S5_CHUNK):
                lanes = slice(b * seq + t * nc, b * seq + (t + 1) * nc)
                o_ref[k * h:(k + 1) * h, lanes] = y[t * h:(t + 1) * h, :]


def s5_out(ut, t_mat, x_re, x_im, v_pack, layer, *, bsz, seq):
    n_pair = v_pack.shape[1]
    p2 = v_pack.shape[-1]
    nc = seq // S5_CHUNK
    lh = t_mat.shape[-1]
    xblk = pl.BlockSpec((nc, bsz * p2), lambda j: (0, j))
    ublk = pl.BlockSpec((2 * S5_GROUP, bsz * seq), lambda j: (j, 0))
    return pl.pallas_call(
        functools.partial(_s5_out_kernel, bsz=bsz, seq=seq),
        grid=(n_pair,),
        in_specs=[
            ublk,
            _layer_spec(layer, (2, lh, lh), lambda j: (j, 0, 0)),
            xblk,
            xblk,
            _layer_spec(layer, (1,) + v_pack.shape[2:], lambda j: (j, 0, 0, 0, 0)),
        ],
        out_specs=ublk,
        out_shape=jax.ShapeDtypeStruct(ut.shape, BF16),
        compiler_params=_params("parallel"),
        name="s5_out",
    )(ut, t_mat, x_re, x_im, v_pack)


def s5_mixer(ut, bsz, seq, ops, layer):
    t_mat, w_pack, v_pack, a_re, a_im = ops
    w_re, w_im = s5_inject(ut, w_pack, layer, bsz=bsz, seq=seq)
    x_re, x_im = s5_scan(w_re, w_im, a_re, a_im, layer)
    return s5_out(ut, t_mat, x_re, x_im, v_pack, layer, bsz=bsz, seq=seq)


def _glu_kernel(yt_ref, wv_ref, wg_ref, o_ref, y_ref):
    nc = o_ref.shape[0]

    @pl.when(pl.program_id(2) == 0)
    def _():
        for k in range(S5_POS_PER_STEP):
            y_ref[k * nc:(k + 1) * nc, :] = (
                yt_ref[:, k * nc:(k + 1) * nc].astype(F32).T.astype(BF16))

    y = y_ref[...]
    m = _dot(y, wv_ref[...]) * _sigmoid(_dot(y, wg_ref[...]))
    for k in range(S5_POS_PER_STEP):
        o_ref[:, k, :] = m[k * nc:(k + 1) * nc, :]


def glu_gate(yt, w_glu, layer, *, bsz, seq, tn=GLU_COL_TILE):
    k, _ = yt.shape
    d = w_glu.shape[2] // 2
    tn = min(tn, d)
    nj = d // tn
    nc = seq // S5_CHUNK
    sq = S5_POS_PER_STEP
    steps = S5_CHUNK // sq
    out = pl.pallas_call(
        _glu_kernel,
        grid=(bsz, steps, nj),
        in_specs=[
            pl.BlockSpec((k, sq * nc), lambda b, s, j: (0, b * steps + s)),
            _layer_spec(layer, (k, tn), lambda b, s, j: (0, j)),
            _layer_spec(layer, (k, tn), lambda b, s, j: (0, j + nj)),
        ],
        out_specs=pl.BlockSpec((None, nc, sq, tn), lambda b, s, j: (b, 0, s, j)),
        out_shape=jax.ShapeDtypeStruct((bsz, nc, S5_CHUNK, d), F32),
        scratch_shapes=[pltpu.VMEM((sq * nc, k), BF16)],
        compiler_params=_params("parallel", "parallel", "arbitrary"),
        name="glu_gate",
    )(yt, w_glu, w_glu)
    return out.reshape(bsz * seq, d)


def _band_bias(rel_bias):
    c = ATT_CHUNK
    nq = ATT_QGROUP * c
    nk = (ATT_QGROUP + ATT_LEFT) * c
    r = rel_bias.astype(F32)
    n_h = r.shape[0]
    d_lo = ATT_LEFT * c - (nk - 1)
    d_hi = ATT_LEFT * c + nq - 1
    n_lo = -(c - 1) - d_lo
    n_hi = d_hi - ATT_MAX_REL
    e = jnp.concatenate([jnp.broadcast_to(r[:, :1], (n_h, n_lo)), r,
                         jnp.broadcast_to(r[:, -1:], (n_h, n_hi))], axis=1)
    assert e.shape[1] == nq + nk - 1
    width = 2 * c - 1
    starts = [nk - 1 - (c - 1) - n * c for n in range(ATT_LEFT + 1)]
    v = jnp.stack([e[:, st:st + width] for st in starts], axis=1)
    frev = jnp.concatenate([v[:, :, ::-1], jnp.zeros(v.shape[:2] + (1,), F32)], axis=2)
    skew = jnp.tile(frev, (1, 1, c))[:, :, :c * width].reshape(n_h, ATT_LEFT + 1, c, width)
    blocks = skew[:, :, :, c - 1:]
    masked = jnp.full((n_h, c, c), NEG_INF, F32)
    rows = [jnp.concatenate([blocks[:, kb - qa] if 0 <= kb - qa <= ATT_LEFT else masked
                             for kb in range(ATT_QGROUP + ATT_LEFT)], axis=-1)
            for qa in range(ATT_QGROUP)]
    return jnp.concatenate(rows, axis=-2)


def _band_attn_kernel(q_ref, k_ref, v_ref, b_ref, o_ref, s_ref, p_ref, v1_ref, *, seq):
    c = ATT_CHUNK
    dh = ATT_HEAD_DIM
    nq = ATT_QGROUP * c
    nk = (ATT_QGROUP + ATT_LEFT) * c
    v1_ref[:, :dh] = v_ref[...]
    v1_ref[:, dh:] = jnp.ones((seq, dh), BF16)
    for gi in range(seq // nq):
        q0 = gi * nq
        off = max(0, ATT_LEFT * c - q0)
        k0 = q0 - ATT_LEFT * c + off
        w = nk - off
        s_ref[:, :w] = _dot_nt(q_ref[q0:q0 + nq, :], k_ref[k0:k0 + w, :])
        for a in range(ATT_QGROUP):
            rows = slice(a * c, (a + 1) * c)
            lo = max((a * c) // LANES * LANES, off)
            hi = min(-(-((a + ATT_LEFT + 1) * c) // LANES) * LANES, nk)
            s = s_ref[rows, lo - off:hi - off] + b_ref[0, rows, lo:hi]
            m = jnp.max(s, axis=-1, keepdims=True)
            p_ref[rows, lo - off:hi - off] = jnp.exp(s - m).astype(BF16)
            if lo > off:
                p_ref[rows, 0:lo - off] = jnp.zeros((c, lo - off), BF16)
            if hi < nk:
                p_ref[rows, hi - off:w] = jnp.zeros((c, nk - hi), BF16)
        o2 = _dot(p_ref[:, :w], v1_ref[k0:k0 + w, :])
        o_ref[q0:q0 + nq, :] = (o2[:, :dh] / o2[:, dh:]).astype(o_ref.dtype)


def band_attention(q, kv, bias, layer, *, bsz, seq):
    m, d = q.shape
    n_h = d // ATT_HEAD_DIM
    dh = ATT_HEAD_DIM
    nq, nk = bias.shape[2:]
    return pl.pallas_call(
        functools.partial(_band_attn_kernel, seq=seq),
        grid=(bsz, n_h),
        in_specs=[
            pl.BlockSpec((seq, dh), lambda b, h: (b, h)),
            pl.BlockSpec((seq, dh), lambda b, h: (b, h)),
            pl.BlockSpec((seq, dh), lambda b, h: (b, h + n_h)),
            _layer_spec(layer, (1, nq, nk), lambda b, h: (h, 0, 0)),
        ],
        out_specs=pl.BlockSpec((seq, dh), lambda b, h: (b, h)),
        out_shape=jax.ShapeDtypeStruct((m, d), BF16),
        scratch_shapes=[
            pltpu.VMEM((nq, nk), F32),
            pltpu.VMEM((nq, nk), BF16),
            pltpu.VMEM((seq, 2 * dh), BF16),
        ],
        compiler_params=_params("parallel", "parallel"),
        name="band_attention",
    )(q, kv, kv, bias)


def _mem_attn_kernel(*refs, mixer):
    if mixer == "update":
        x_ref, r_ref, gu_ref, g1_ref, wq_ref, k_ref, v_ref, wo_ref, g2_ref, o_ref = refs
        x = x_ref[...] + _rms(r_ref[...], gu_ref[...])
    else:
        x_ref, a_ref, wp_ref, gu_ref, g1_ref, wq_ref, k_ref, v_ref, wo_ref, g2_ref, o_ref = refs
        x = x_ref[...] + _rms(_dot(a_ref[...], wp_ref[...]), gu_ref[...])
    h = _rms(x, g1_ref[...]).astype(BF16)
    q = _dot(h, wq_ref[...]).astype(BF16)
    k = k_ref[...]
    v = v_ref[...]
    dh = MEM_HEAD_DIM
    n_heads = q.shape[1] // dh
    scores = [_dot_nt(q[:, hd * dh:(hd + 1) * dh], k[:, hd * dh:(hd + 1) * dh])
              for hd in range(n_heads)]
    ones = jnp.ones((v.shape[0], dh), BF16)
    heads = []
    for hd in range(n_heads):
        s = scores[hd]
        p = jnp.exp(s - jnp.max(s, axis=-1, keepdims=True)).astype(BF16)
        o2 = _dot(p, jnp.concatenate([v[:, hd * dh:(hd + 1) * dh], ones], axis=-1))
        heads.append((o2[:, :dh] / o2[:, dh:]).astype(BF16))
    o = jnp.concatenate(heads, axis=-1)
    c = _dot(o, wo_ref[...])
    o_ref[...] = x + _rms(c, g2_ref[...])


def mem_attention(x, mixer, g1, wq, kvmem, layer, wo, g2, *, seq, n_mem, tm):
    m, d = x.shape
    md = wq.shape[2]
    tiles_per_batch = seq // tm
    row = pl.BlockSpec((tm, d), lambda i: (i, 0))
    gain = pl.BlockSpec((1, d), lambda i: (0, 0))
    if mixer[0] == "update":
        _, r, gu = mixer
        acts, act_specs = [x, r, gu.reshape(1, d)], [row, row, gain]
    else:
        _, a, wp, wp_layer, gu = mixer
        kp = a.shape[1]
        acts = [x, a, wp, gu.reshape(1, d)]
        act_specs = [row, pl.BlockSpec((tm, kp), lambda i: (i, 0)),
                     _layer_spec(wp_layer, (kp, d), lambda i: (0, 0)), gain]
    return pl.pallas_call(
        functools.partial(_mem_attn_kernel, mixer=mixer[0]),
        grid=(m // tm,),
        in_specs=act_specs + [
            pl.BlockSpec((1, d), lambda i: (0, 0)),
            _layer_spec(layer, (d, md), lambda i: (0, 0)),
            pl.BlockSpec((n_mem, md), lambda i: (i // tiles_per_batch, 2 * layer)),
            pl.BlockSpec((n_mem, md), lambda i: (i // tiles_per_batch, 2 * layer + 1)),
            _layer_spec(layer, (md, d), lambda i: (0, 0)),
            pl.BlockSpec((1, d), lambda i: (0, 0)),
        ],
        out_specs=row,
        out_shape=jax.ShapeDtypeStruct((m, d), F32),
        compiler_params=_params("parallel"),
        name="mem_attention",
    )(*acts, g1.reshape(1, d), wq, kvmem, kvmem, wo, g2.reshape(1, d))


def _conv_ffn_kernel(x_ref, g1_ref, wuv_ref, wug_ref, cwv_ref, cwg_ref, cbv_ref, cbg_ref, wd_ref,
                     g2_ref, o_ref, h_ref, acc_ref, tail_ref, up_ref, wd_bf_ref, *, nj,
                     tiles_per_batch):
    i = pl.program_id(0)
    j = pl.program_id(1)
    tm = x_ref.shape[0]
    hm = tm // FFN_ROW_SPLIT

    @pl.when(j == 0)
    def _():
        h_ref[...] = _rms(x_ref[...], g1_ref[...]).astype(h_ref.dtype)
        acc_ref[...] = jnp.zeros(acc_ref.shape, F32)

    @pl.when(i % tiles_per_batch == 0)
    def _():
        tail_ref[j] = jnp.zeros(tail_ref.shape[1:], F32)

    up_ref[:, 0:SUBLANES, :] = tail_ref[j]
    wd_bf_ref[...] = wd_ref[...].astype(BF16)

    for r in range(FFN_ROW_SPLIT):
        h = h_ref[r * hm:(r + 1) * hm, :]
        rows = slice(SUBLANES + r * hm, SUBLANES + (r + 1) * hm)
        up_ref[0, rows, :] = _dot(h, wuv_ref[...])
        up_ref[1, rows, :] = _dot(h, wug_ref[...])

    tail_ref[j] = up_ref[:, tm:tm + SUBLANES, :]

    def conv(slot, base, cw, cb):
        taps = [up_ref[slot, base - (CONV_W - 1 - k):base - (CONV_W - 1 - k) + hm, :]
                for k in range(CONV_W)]
        return cb + taps[0] * cw[0:1, :] + taps[1] * cw[1:2, :] + taps[2] * cw[2:3, :]

    for r in range(FFN_ROW_SPLIT):
        base = SUBLANES + r * hm
        val = conv(0, base, cwv_ref[...], cbv_ref[...])
        gate = conv(1, base, cwg_ref[...], cbg_ref[...])
        act = (val * (gate * _sigmoid(gate))).astype(BF16)
        acc_ref[r * hm:(r + 1) * hm, :] += _dot(act, wd_bf_ref[...])

    @pl.when(j == nj - 1)
    def _():
        o_ref[...] = x_ref[...] + _rms(acc_ref[...], g2_ref[...])


def conv_ffn(x, g1, w_up, conv_w, conv_b, w_down, layer, g2, *, seq, tm=ROW_TILE, tf=FFN_COL_TILE):
    m, d = x.shape
    f = w_down.shape[1]
    nj = f // tf
    tiles_per_batch = seq // tm
    conv_b = conv_b.reshape(conv_b.shape[0], 1, 2 * f)
    return pl.pallas_call(
        functools.partial(_conv_ffn_kernel, nj=nj, tiles_per_batch=tiles_per_batch),
        grid=(m // tm, nj),
        in_specs=[
            pl.BlockSpec((tm, d), lambda i, j: (i, 0)),
            pl.BlockSpec((1, d), lambda i, j: (0, 0)),
            _layer_spec(layer, (d, tf), lambda i, j: (0, j)),
            _layer_spec(layer, (d, tf), lambda i, j: (0, j + nj)),
            _layer_spec(layer, (CONV_W, tf), lambda i, j: (0, j)),
            _layer_spec(layer, (CONV_W, tf), lambda i, j: (0, j + nj)),
            _layer_spec(layer, (1, tf), lambda i, j: (0, j)),
            _layer_spec(layer, (1, tf), lambda i, j: (0, j + nj)),
            _layer_spec(layer, (tf, d), lambda i, j: (j, 0)),
            pl.BlockSpec((1, d), lambda i, j: (0, 0)),
        ],
        out_specs=pl.BlockSpec((tm, d), lambda i, j: (i, 0)),
        out_shape=jax.ShapeDtypeStruct((m, d), F32),
        scratch_shapes=[
            pltpu.VMEM((tm, d), BF16),
            pltpu.VMEM((tm, d), F32),
            pltpu.VMEM((nj, 2, SUBLANES, tf), F32),
            pltpu.VMEM((2, SUBLANES + tm, tf), F32),
            pltpu.VMEM((tf, d), BF16),
        ],
        compiler_params=_params("arbitrary", "arbitrary"),
        name="conv_ffn",
    )(x, g1.reshape(1, d), w_up, w_up, conv_w, conv_w, conv_b, conv_b, w_down, g2.reshape(1, d))


def kernel(x, mem, norm_mix, norm_mem, norm_ffn, mem_in_norm, a_w_in, a_lam_re, a_lam_im, a_log_dt,
           a_b_re, a_b_im, a_c_re, a_c_im, a_d, a_w_glu, kv_norm, w_k, w_v, b_w_q, b_rel_bias, b_w_o,
           m_w_q, m_w_kv, m_w_o, f_w_up, f_conv_w, f_conv_b, f_w_down):
    bsz, seq, d = x.shape
    depth = norm_mix.shape[0]
    n_a = a_w_in.shape[0]
    n_mem = mem.shape[1]
    md = m_w_q.shape[2]
    assert seq % ROW_TILE == 0 and seq % (ATT_QGROUP * ATT_CHUNK) == 0
    assert (seq // S5_CHUNK) % LANES == 0 and (d // S5_GROUP) % 2 == 0

    xf = x.reshape(bsz * seq, d)

    w_kv_mem = m_w_kv.astype(BF16)
    w_kv = jnp.stack([w_k, w_v]).astype(BF16)
    a_w_in_t = jnp.transpose(a_w_in, (0, 2, 1)).astype(BF16)
    a_w_glu_b = a_w_glu.astype(BF16)
    b_w_q_b = (b_w_q * (ATT_HEAD_DIM ** -0.5)).astype(BF16)
    b_w_o_b = b_w_o.astype(BF16)
    m_w_q_b = (m_w_q * (MEM_HEAD_DIM ** -0.5)).astype(BF16)
    m_w_o_b = m_w_o.astype(BF16)
    f_w_up_b = f_w_up.astype(BF16)
    flat = lambda a: a.reshape((-1,) + a.shape[2:])
    s5_ops = _s5_operators(flat(a_lam_re), flat(a_lam_im), flat(a_log_dt), flat(a_b_re),
                           flat(a_b_im), flat(a_c_re), flat(a_c_im), a_d, bsz)
    s5_ops = [o.reshape((n_a, o.shape[0] // n_a) + o.shape[1:]) for o in s5_ops[:3]] + [
        o.reshape(n_a, 1, -1) for o in s5_ops[3:]]
    att_bias = jax.vmap(_band_bias)(b_rel_bias)

    kvmem = norm_matmul(mem.reshape(bsz * n_mem, d), mem_in_norm, w_kv_mem, None)

    kv = None
    for l in range(depth):
        if l == n_a:
            kv = norm_matmul(xf, kv_norm, w_kv, None)
        if l < n_a:
            ut = norm_matmul_t(xf, norm_mix[l, 0], a_w_in_t, l, bsz=bsz, seq=seq)
            yt = s5_mixer(ut, bsz, seq, s5_ops, l)
            mixer = ("update", glu_gate(yt, a_w_glu_b, l, bsz=bsz, seq=seq), norm_mix[l, 1])
            tm = ROW_TILE
        else:
            jb = l - n_a
            q = norm_matmul(xf, norm_mix[l, 0], b_w_q_b, jb)
            o = band_attention(q, kv, att_bias, jb, bsz=bsz, seq=seq)
            mixer = ("project", o, b_w_o_b, jb, norm_mix[l, 1])
            tm = PROJ_ROW_TILE
        xf = mem_attention(xf, mixer, norm_mem[l, 0], m_w_q_b, kvmem, l, m_w_o_b, norm_mem[l, 1],
                           seq=seq, n_mem=n_mem, tm=tm)
        xf = conv_ffn(xf, norm_ffn[l, 0], f_w_up_b, f_conv_w, f_conv_b, f_w_down, l,
                      norm_ffn[l, 1], seq=seq)
    return xf.reshape(bsz, seq, d)
```

```python
import functools

import numpy as np
import jax
import jax.numpy as jnp
from jax import lax
from jax.experimental import pallas as pl
from jax.experimental.pallas import tpu as pltpu

EPS = 1e-6
NEG_INF = -1e30
BF16 = jnp.bfloat16
F32 = jnp.float32

S5_GROUP = 16
S5_CHUNK = 16
S5_POS_PER_STEP = 8
ATT_CHUNK = 64
ATT_LEFT = 8
ATT_HEAD_DIM = 128
ATT_MAX_REL = 256
ATT_QGROUP = 4
MEM_HEAD_DIM = 128
CONV_W = 3

VMEM_LIMIT_BYTES = 56 * 1024 * 1024
ROW_TILE = 512
COL_TILE = 2048
FFN_ROW_SPLIT = 2
FFN_COL_TILE = 512
GLU_COL_TILE = 1024
SUBLANES = 8
LANES = 128

_NT = (((1,), (1,)), ((), ()))


def _params(*sem):
    return pltpu.CompilerParams(dimension_semantics=sem, vmem_limit_bytes=VMEM_LIMIT_BYTES)


def _rms(x, g):
    ms = jnp.mean(x * x, axis=-1, keepdims=True)
    return x * lax.rsqrt(ms + EPS) * g


def _sigmoid(x):
    return 1.0 / (1.0 + jnp.exp(-x))


def _dot(a, b):
    return jnp.dot(a, b, preferred_element_type=F32)


def _dot_nt(a, b):
    return lax.dot_general(a, b, _NT, preferred_element_type=F32)


def _layer_spec(layer, block, index_map):
    return pl.BlockSpec((None,) + tuple(block), lambda *ids: (layer,) + tuple(index_map(*ids)))


def _norm_matmul_kernel(x_ref, g_ref, w_ref, o_ref, h_ref):
    @pl.when(pl.program_id(1) == 0)
    def _():
        h_ref[...] = _rms(x_ref[...], g_ref[...]).astype(h_ref.dtype)

    o_ref[...] = _dot(h_ref[...], w_ref[...]).astype(o_ref.dtype)


def norm_matmul(x, g, w, layer, *, tm=ROW_TILE, tn=COL_TILE, out_dtype=BF16):
    m, k = x.shape
    if layer is None:
        tn = w.shape[2]
        n = w.shape[0] * tn
        w_spec = pl.BlockSpec((None, k, tn), lambda i, j: (j, 0, 0))
    else:
        n = w.shape[2]
        tn = min(tn, n)
        w_spec = _layer_spec(layer, (k, tn), lambda i, j: (0, j))
    return pl.pallas_call(
        _norm_matmul_kernel,
        grid=(m // tm, n // tn),
        in_specs=[
            pl.BlockSpec((tm, k), lambda i, j: (i, 0)),
            pl.BlockSpec((1, k), lambda i, j: (0, 0)),
            w_spec,
        ],
        out_specs=pl.BlockSpec((tm, tn), lambda i, j: (i, j)),
        out_shape=jax.ShapeDtypeStruct((m, n), out_dtype),
        scratch_shapes=[pltpu.VMEM((tm, k), BF16)],
        compiler_params=_params("parallel", "arbitrary"),
        name="norm_matmul",
    )(x, g.reshape(1, k), w)


def _proj_res_kernel(a_ref, w_ref, x_ref, g_ref, o_ref):
    f = _dot(a_ref[...], w_ref[...])
    o_ref[...] = x_ref[...] + _rms(f, g_ref[...])


def proj_res(a, w, layer, x, g, *, tm=ROW_TILE):
    m, k = a.shape
    d = w.shape[2]
    return pl.pallas_call(
        _proj_res_kernel,
        grid=(m // tm,),
        in_specs=[
            pl.BlockSpec((tm, k), lambda i: (i, 0)),
            _layer_spec(layer, (k, d), lambda i: (0, 0)),
            pl.BlockSpec((tm, d), lambda i: (i, 0)),
            pl.BlockSpec((1, d), lambda i: (0, 0)),
        ],
        out_specs=pl.BlockSpec((tm, d), lambda i: (i, 0)),
        out_shape=jax.ShapeDtypeStruct((m, d), F32),
        compiler_params=_params("parallel"),
        name="proj_res",
    )(a, w, x, g.reshape(1, d))


def _s5_build_kernel(rev_ref, bb_ref, cc_ref, cpad_ref, pt_ref, d_ref, t_ref, w_ref, v_ref):
    n_l = S5_CHUNK
    n_h = S5_GROUP
    lh = n_l * n_h
    n_p = rev_ref.shape[2]

    def exact_dot(a, b):
        return jnp.dot(a, b, precision=lax.Precision.HIGHEST, preferred_element_type=F32)

    def spread(a, pattern):
        hi = a.astype(BF16)
        rest = a - hi.astype(F32)
        mid = rest.astype(BF16)
        lo = (rest - mid.astype(F32)).astype(BF16)
        return _dot(hi, pattern) + _dot(mid, pattern) + _dot(lo, pattern)

    lane = lax.broadcasted_iota(jnp.int32, (n_l, lh), 1)
    row = lax.broadcasted_iota(jnp.int32, (n_l, lh), 0)
    per_step = jnp.where(lane // n_h == row, 1.0, 0.0).astype(BF16)
    per_chan = jnp.where(lane % n_h == row, 1.0, 0.0).astype(BF16)
    sub = lax.broadcasted_iota(jnp.int32, (n_h, lh), 0)
    lan = lax.broadcasted_iota(jnp.int32, (n_h, lh), 1)
    no_state = jnp.zeros((n_p, lh), BF16)
    for k in range(2):
        r_re = spread(rev_ref[k, 0], per_step)
        r_im = spread(rev_ref[k, 1], per_step)
        b_re = spread(bb_ref[k, 0], per_chan)
        b_im = spread(bb_ref[k, 1], per_chan)
        w_re = r_re * b_re - r_im * b_im
        w_im = r_re * b_im + r_im * b_re
        for ri, w in enumerate((w_re, w_im)):
            for slot in range(2):
                w_ref[0, k, ri, slot * n_p:(slot + 1) * n_p, :] = (
                    w.astype(BF16) if slot == k else no_state)
        krev = exact_dot(cc_ref[k, 0], w_re) - exact_dot(cc_ref[k, 1], w_im)
        skip = d_ref[k]
        for t in range(n_l):
            sh = (n_l - 1 - t) * n_h
            slab = krev
            if sh:
                slab = jnp.where(lan < lh - sh, pltpu.roll(krev, lh - sh, axis=1), 0.0)
            slab = slab + jnp.where(lan == t * n_h + sub, skip, 0.0)
            t_ref[k, t * n_h:(t + 1) * n_h, :] = slab.astype(BF16)
        c_re = cpad_ref[k, 0]
        c_im = cpad_ref[k, 1]
        for t in range(n_l):
            p_re = pt_ref[k, 0, t + 1:t + 2, :]
            p_im = pt_ref[k, 1, t + 1:t + 2, :]
            v_ref[0, k, 0, t * n_h:(t + 1) * n_h, :] = (c_re * p_re - c_im * p_im).astype(BF16)
            v_ref[0, k, 1, t * n_h:(t + 1) * n_h, :] = (-(c_re * p_im + c_im * p_re)).astype(BF16)


def _s5_operators(lam_re, lam_im, log_dt, b_re, b_im, c_re, c_im, d_skip, bsz):
    n_g, n_p = lam_re.shape
    n_h = S5_GROUP
    n_l = S5_CHUNK
    lh = n_l * n_h
    lr = lam_re.astype(F32)
    li = lam_im.astype(F32)
    dt = jnp.exp(log_dt.astype(F32))[:, None]
    mag = jnp.exp(lr * dt)
    ang = li * dt
    ab_re = mag * jnp.cos(ang)
    ab_im = mag * jnp.sin(ang)
    den = lr * lr + li * li
    nr = ab_re - 1.0
    f_re = (nr * lr + ab_im * li) / den
    f_im = (ab_im * lr - nr * li) / den
    br = b_re.astype(F32)
    bi = b_im.astype(F32)
    bb = jnp.stack([f_re[..., None] * br - f_im[..., None] * bi,
                    f_re[..., None] * bi + f_im[..., None] * br], axis=1)
    tau = jnp.arange(n_l + 1, dtype=F32)
    pw_re = jnp.exp((lr * dt)[..., None] * tau) * jnp.cos(ang[..., None] * tau)
    pw_im = jnp.exp((lr * dt)[..., None] * tau) * jnp.sin(ang[..., None] * tau)
    tau_m = tau[None, :, None]
    pt = jnp.stack([jnp.exp((lr * dt)[:, None, :] * tau_m) * jnp.cos(ang[:, None, :] * tau_m),
                    jnp.exp((lr * dt)[:, None, :] * tau_m) * jnp.sin(ang[:, None, :] * tau_m)],
                   axis=1)
    rev = jnp.stack([pw_re[:, :, n_l - 1::-1], pw_im[:, :, n_l - 1::-1]], axis=1)
    cc = jnp.stack([c_re.astype(F32), c_im.astype(F32)], axis=1)
    slot = (jnp.arange(n_g) % 2)[:, None, None, None]

    def lane_place(a):
        z = jnp.zeros_like(a)
        return jnp.where(slot == 0, jnp.concatenate([a, z], -1), jnp.concatenate([z, a], -1))

    skip = jnp.broadcast_to(d_skip.astype(F32).reshape(n_g, n_h, 1), (n_g, n_h, lh))
    n_pair = n_g // 2
    pair = lambda shape: pl.BlockSpec((2,) + shape, lambda j: (j,) + (0,) * len(shape))
    t_mat, w_pack, v_pack = pl.pallas_call(
        _s5_build_kernel,
        grid=(n_pair,),
        in_specs=[pair((2, n_p, n_l)), pair((2, n_p, n_h)), pair((2, n_h, n_p)),
                  pair((2, n_h, 2 * n_p)), pair((2, n_l + 1, 2 * n_p)), pair((n_h, lh))],
        out_specs=[
            pl.BlockSpec((2, lh, lh), lambda j: (j, 0, 0)),
            pl.BlockSpec((1, 2, 2, 2 * n_p, lh), lambda j: (j, 0, 0, 0, 0)),
            pl.BlockSpec((1, 2, 2, lh, 2 * n_p), lambda j: (j, 0, 0, 0, 0)),
        ],
        out_shape=[
            jax.ShapeDtypeStruct((n_g, lh, lh), BF16),
            jax.ShapeDtypeStruct((n_pair, 2, 2, 2 * n_p, lh), BF16),
            jax.ShapeDtypeStruct((n_pair, 2, 2, lh, 2 * n_p), BF16),
        ],
        compiler_params=_params("parallel"),
        name="s5_build_operators",
    )(rev, bb, cc, lane_place(cc), lane_place(pt), skip)

    def state_cols(a):
        a = jnp.broadcast_to(a.reshape(n_pair, 1, 2 * n_p), (n_pair, bsz, 2 * n_p))
        return a.reshape(1, n_g * bsz * n_p)
    return t_mat, w_pack, v_pack, state_cols(pw_re[:, :, n_l]), state_cols(pw_im[:, :, n_l])


def _chunk_view(x, bsz, seq):
    return x.reshape(bsz, seq // S5_CHUNK, S5_CHUNK, x.shape[1])


def _norm_matmul_t_kernel(x_hbm, g_ref, wt_ref, o_ref, slab_ref, h_ref, sem, *, steps, total):
    j = pl.program_id(2)
    _, sq, nc, _ = slab_ref.shape
    step = pl.program_id(0) * steps + pl.program_id(1)
    slot = step % 2

    def slab_copies(step_, slot_):
        b = step_ // steps
        pos0 = (step_ % steps) * sq
        return [pltpu.make_async_copy(x_hbm.at[b, :, pos0 + k, :], slab_ref.at[slot_, k],
                                      sem.at[slot_, k]) for k in range(sq)]

    @pl.when(j == 0)
    def _():
        @pl.when(step == 0)
        def _():
            for c in slab_copies(step, slot):
                c.start()

        for k, c in enumerate(slab_copies(step, slot)):
            c.wait()
            h_ref[k * nc:(k + 1) * nc, :] = _rms(slab_ref[slot, k], g_ref[...]).astype(BF16)

    @pl.when((j == pl.num_programs(2) - 1) & (step + 1 < total))
    def _():
        for c in slab_copies(step + 1, 1 - slot):
            c.start()

    o_ref[...] = _dot_nt(wt_ref[...], h_ref[...]).astype(o_ref.dtype)


def norm_matmul_t(x, g, wt, layer, *, bsz, seq, tn=1024):
    d = x.shape[1]
    n = wt.shape[1]
    tn = min(tn, n)
    nc = seq // S5_CHUNK
    sq = S5_POS_PER_STEP
    steps = S5_CHUNK // sq
    return pl.pallas_call(
        functools.partial(_norm_matmul_t_kernel, steps=steps, total=bsz * steps),
        grid=(bsz, steps, n // tn),
        in_specs=[
            pl.BlockSpec(memory_space=pl.ANY),
            pl.BlockSpec((1, d), lambda b, s, j: (0, 0)),
            _layer_spec(layer, (tn, d), lambda b, s, j: (j, 0)),
        ],
        out_specs=pl.BlockSpec((tn, sq * nc), lambda b, s, j: (j, b * steps + s)),
        out_shape=jax.ShapeDtypeStruct((n, bsz * seq), BF16),
        scratch_shapes=[
            pltpu.VMEM((2, sq, nc, d), x.dtype),
            pltpu.VMEM((sq * nc, d), BF16),
            pltpu.SemaphoreType.DMA((2, sq)),
        ],
        compiler_params=_params("arbitrary", "arbitrary", "arbitrary"),
        name="norm_matmul_t",
    )(_chunk_view(x, bsz, seq), g.reshape(1, d), wt)


def _chunk_operand(u_ref, k, nc):
    h = S5_GROUP
    return jnp.concatenate(
        [u_ref[k * h:(k + 1) * h, s * nc:(s + 1) * nc] for s in range(S5_CHUNK)], axis=0)


def _s5_inject_kernel(u_ref, w_ref, ore_ref, oim_ref, *, bsz, seq):
    nc = ore_ref.shape[0]
    p2 = w_ref.shape[3]
    for b in range(bsz):
        ub = u_ref.at[:, b * seq:(b + 1) * seq]
        u0 = _chunk_operand(ub, 0, nc)
        u1 = _chunk_operand(ub, 1, nc)
        cols = slice(b * p2, (b + 1) * p2)
        ore_ref[:, cols] = (_dot(w_ref[0, 0, 0], u0) + _dot(w_ref[0, 1, 0], u1)).T
        oim_ref[:, cols] = (_dot(w_ref[0, 0, 1], u0) + _dot(w_ref[0, 1, 1], u1)).T


def s5_inject(ut, w_pack, layer, *, bsz, seq):
    n_pair = w_pack.shape[1]
    p2 = w_pack.shape[4]
    nc = seq // S5_CHUNK
    out = jax.ShapeDtypeStruct((nc, n_pair * bsz * p2), F32)
    oblk = pl.BlockSpec((nc, bsz * p2), lambda j: (0, j))
    return pl.pallas_call(
        functools.partial(_s5_inject_kernel, bsz=bsz, seq=seq),
        grid=(n_pair,),
        in_specs=[
            pl.BlockSpec((2 * S5_GROUP, bsz * seq), lambda j: (j, 0)),
            _layer_spec(layer, (1,) + w_pack.shape[2:], lambda j: (j, 0, 0, 0, 0)),
        ],
        out_specs=[oblk, oblk],
        out_shape=[out, out],
        compiler_params=_params("parallel"),
        name="s5_inject",
    )(ut, w_pack)


def _s5_scan_kernel(wre_ref, wim_ref, are_ref, aim_ref, xre_ref, xim_ref, *, n_c):
    a_re = are_ref[...]
    a_im = aim_ref[...]

    def body(c, carry):
        s_re, s_im = carry
        row = pl.ds(c, 1)
        xre_ref[row, :] = s_re
        xim_ref[row, :] = s_im
        n_re = a_re * s_re - a_im * s_im + wre_ref[row, :]
        n_im = a_re * s_im + a_im * s_re + wim_ref[row, :]
        return n_re, n_im

    zero = jnp.zeros(a_re.shape, F32)
    lax.fori_loop(0, n_c, body, (zero, zero))


def s5_scan(w_re, w_im, a_re, a_im, layer, *, tl=2048):
    n_c, cols = w_re.shape
    tl = min(tl, cols)
    out = jax.ShapeDtypeStruct((n_c, cols), F32)
    blk = pl.BlockSpec((n_c, tl), lambda j: (0, j))
    coef = _layer_spec(layer, (1, tl), lambda j: (0, j))
    return pl.pallas_call(
        functools.partial(_s5_scan_kernel, n_c=n_c),
        grid=(cols // tl,),
        in_specs=[blk, blk, coef, coef],
        out_specs=[blk, blk],
        out_shape=[out, out],
        compiler_params=_params("parallel"),
        name="s5_scan",
    )(w_re, w_im, a_re, a_im)


def _gelu_tanh(y):
    c = np.float32(np.sqrt(2.0 / np.pi))
    return 0.5 * y * (1.0 + jnp.tanh(c * (y + 0.044715 * (y * y * y))))


def _s5_out_kernel(u_ref, t_ref, xre_ref, xim_ref, v_ref, o_ref, *, bsz, seq):
    nc = xre_ref.shape[0]
    p2 = v_ref.shape[-1]
    h = S5_GROUP
    for b in range(bsz):
        ub = u_ref.at[:, b * seq:(b + 1) * seq]
        x_re = xre_ref[:, b * p2:(b + 1) * p2].astype(BF16)
        x_im = xim_ref[:, b * p2:(b + 1) * p2].astype(BF16)
        for k in range(2):
            y = (_dot(t_ref[k], _chunk_operand(ub, k, nc))
                 + _dot_nt(v_ref[0, k, 0], x_re) + _dot_nt(v_ref[0, k, 1], x_im))
            y = _gelu_tanh(y).astype(o_ref.dtype)
            for t in range(S5_CHUNK):
                lanes = slice(b * seq + t * nc, b * seq + (t + 1) * nc)
                o_ref[k * h:(k + 1) * h, lanes] = y[t * h:(t + 1) * h, :]


def s5_out(ut, t_mat, x_re, x_im, v_pack, layer, *, bsz, seq):
    n_pair = v_pack.shape[1]
    p2 = v_pack.shape[-1]
    nc = seq // S5_CHUNK
    lh = t_mat.shape[-1]
    xblk = pl.BlockSpec((nc, bsz * p2), lambda j: (0, j))
    ublk = pl.BlockSpec((2 * S5_GROUP, bsz * seq), lambda j: (j, 0))
    return pl.pallas_call(
        functools.partial(_s5_out_kernel, bsz=bsz, seq=seq),
        grid=(n_pair,),
        in_specs=[
            ublk,
            _layer_spec(layer, (2, lh, lh), lambda j: (j, 0, 0)),
            xblk,
            xblk,
            _layer_spec(layer, (1,) + v_pack.shape[2:], lambda j: (j, 0, 0, 0, 0)),
        ],
        out_specs=ublk,
        out_shape=jax.ShapeDtypeStruct(ut.shape, BF16),
        compiler_params=_params("parallel"),
        name="s5_out",
    )(ut, t_mat, x_re, x_im, v_pack)


def s5_mixer(ut, bsz, seq, ops, layer):
    t_mat, w_pack, v_pack, a_re, a_im = ops
    w_re, w_im = s5_inject(ut, w_pack, layer, bsz=bsz, seq=seq)
    x_re, x_im = s5_scan(w_re, w_im, a_re, a_im, layer)
    return s5_out(ut, t_mat, x_re, x_im, v_pack, layer, bsz=bsz, seq=seq)


def _glu_kernel(yt_ref, wv_ref, wg_ref, o_ref, y_ref):
    nc = o_ref.shape[0]

    @pl.when(pl.program_id(2) == 0)
    def _():
        for k in range(S5_POS_PER_STEP):
            y_ref[k * nc:(k + 1) * nc, :] = (
                yt_ref[:, k * nc:(k + 1) * nc].astype(F32).T.astype(BF16))

    y = y_ref[...]
    m = _dot(y, wv_ref[...]) * _sigmoid(_dot(y, wg_ref[...]))
    for k in range(S5_POS_PER_STEP):
        o_ref[:, k, :] = m[k * nc:(k + 1) * nc, :]


def glu_gate(yt, w_glu, layer, *, bsz, seq, tn=GLU_COL_TILE):
    k, _ = yt.shape
    d = w_glu.shape[2] // 2
    tn = min(tn, d)
    nj = d // tn
    nc = seq // S5_CHUNK
    sq = S5_POS_PER_STEP
    steps = S5_CHUNK // sq
    out = pl.pallas_call(
        _glu_kernel,
        grid=(bsz, steps, nj),
        in_specs=[
            pl.BlockSpec((k, sq * nc), lambda b, s, j: (0, b * steps + s)),
            _layer_spec(layer, (k, tn), lambda b, s, j: (0, j)),
            _layer_spec(layer, (k, tn), lambda b, s, j: (0, j + nj)),
        ],
        out_specs=pl.BlockSpec((None, nc, sq, tn), lambda b, s, j: (b, 0, s, j)),
        out_shape=jax.ShapeDtypeStruct((bsz, nc, S5_CHUNK, d), F32),
        scratch_shapes=[pltpu.VMEM((sq * nc, k), BF16)],
        compiler_params=_params("parallel", "parallel", "arbitrary"),
        name="glu_gate",
    )(yt, w_glu, w_glu)
    return out.reshape(bsz * seq, d)


def _band_bias(rel_bias):
    c = ATT_CHUNK
    nq = ATT_QGROUP * c
    nk = (ATT_QGROUP + ATT_LEFT) * c
    r = rel_bias.astype(F32)
    n_h = r.shape[0]
    d_lo = ATT_LEFT * c - (nk - 1)
    d_hi = ATT_LEFT * c + nq - 1
    n_lo = -(c - 1) - d_lo
    n_hi = d_hi - ATT_MAX_REL
    e = jnp.concatenate([jnp.broadcast_to(r[:, :1], (n_h, n_lo)), r,
                         jnp.broadcast_to(r[:, -1:], (n_h, n_hi))], axis=1)
    assert e.shape[1] == nq + nk - 1
    width = 2 * c - 1
    starts = [nk - 1 - (c - 1) - n * c for n in range(ATT_LEFT + 1)]
    v = jnp.stack([e[:, st:st + width] for st in starts], axis=1)
    frev = jnp.concatenate([v[:, :, ::-1], jnp.zeros(v.shape[:2] + (1,), F32)], axis=2)
    skew = jnp.tile(frev, (1, 1, c))[:, :, :c * width].reshape(n_h, ATT_LEFT + 1, c, width)
    blocks = skew[:, :, :, c - 1:]
    masked = jnp.full((n_h, c, c), NEG_INF, F32)
    rows = [jnp.concatenate([blocks[:, kb - qa] if 0 <= kb - qa <= ATT_LEFT else masked
                             for kb in range(ATT_QGROUP + ATT_LEFT)], axis=-1)
            for qa in range(ATT_QGROUP)]
    return jnp.concatenate(rows, axis=-2)


def _band_attn_kernel(q_ref, k_ref, v_ref, b_ref, o_ref, s_ref, p_ref, v1_ref, *, seq):
    c = ATT_CHUNK
    dh = ATT_HEAD_DIM
    nq = ATT_QGROUP * c
    nk = (ATT_QGROUP + ATT_LEFT) * c
    v1_ref[:, :dh] = v_ref[...]
    v1_ref[:, dh:] = jnp.ones((seq, dh), BF16)
    for gi in range(seq // nq):
        q0 = gi * nq
        off = max(0, ATT_LEFT * c - q0)
        k0 = q0 - ATT_LEFT * c + off
        w = nk - off
        s_ref[:, :w] = _dot_nt(q_ref[q0:q0 + nq, :], k_ref[k0:k0 + w, :])
        for a in range(ATT_QGROUP):
            rows = slice(a * c, (a + 1) * c)
            lo = max((a * c) // LANES * LANES, off)
            hi = min(-(-((a + ATT_LEFT + 1) * c) // LANES) * LANES, nk)
            s = s_ref[rows, lo - off:hi - off] + b_ref[0, rows, lo:hi]
            m = jnp.max(s, axis=-1, keepdims=True)
            p_ref[rows, lo - off:hi - off] = jnp.exp(s - m).astype(BF16)
            if lo > off:
                p_ref[rows, 0:lo - off] = jnp.zeros((c, lo - off), BF16)
            if hi < nk:
                p_ref[rows, hi - off:w] = jnp.zeros((c, nk - hi), BF16)
        o2 = _dot(p_ref[:, :w], v1_ref[k0:k0 + w, :])
        o_ref[q0:q0 + nq, :] = (o2[:, :dh] / o2[:, dh:]).astype(o_ref.dtype)


def band_attention(q, kv, bias, layer, *, bsz, seq):
    m, d = q.shape
    n_h = d // ATT_HEAD_DIM
    dh = ATT_HEAD_DIM
    nq, nk = bias.shape[2:]
    return pl.pallas_call(
        functools.partial(_band_attn_kernel, seq=seq),
        grid=(bsz, n_h),
        in_specs=[
            pl.BlockSpec((seq, dh), lambda b, h: (b, h)),
            pl.BlockSpec((seq, dh), lambda b, h: (b, h)),
            pl.BlockSpec((seq, dh), lambda b, h: (b, h + n_h)),
            _layer_spec(layer, (1, nq, nk), lambda b, h: (h, 0, 0)),
        ],
        out_specs=pl.BlockSpec((seq, dh), lambda b, h: (b, h)),
        out_shape=jax.ShapeDtypeStruct((m, d), BF16),
        scratch_shapes=[
            pltpu.VMEM((nq, nk), F32),
            pltpu.VMEM((nq, nk), BF16),
            pltpu.VMEM((seq, 2 * dh), BF16),
        ],
        compiler_params=_params("parallel", "parallel"),
        name="band_attention",
    )(q, kv, kv, bias)


def _mem_attn_kernel(*refs, has_update):
    if has_update:
        x_ref, r_ref, gu_ref, g1_ref, wq_ref, k_ref, v_ref, wo_ref, g2_ref, o_ref = refs
        x = x_ref[...] + _rms(r_ref[...], gu_ref[...])
    else:
        x_ref, g1_ref, wq_ref, k_ref, v_ref, wo_ref, g2_ref, o_ref = refs
        x = x_ref[...]
    h = _rms(x, g1_ref[...]).astype(BF16)
    q = _dot(h, wq_ref[...]).astype(BF16)
    k = k_ref[...]
    v = v_ref[...]
    dh = MEM_HEAD_DIM
    n_heads = q.shape[1] // dh
    scores = [_dot_nt(q[:, hd * dh:(hd + 1) * dh], k[:, hd * dh:(hd + 1) * dh])
              for hd in range(n_heads)]
    ones = jnp.ones((v.shape[0], dh), BF16)
    heads = []
    for hd in range(n_heads):
        s = scores[hd]
        p = jnp.exp(s - jnp.max(s, axis=-1, keepdims=True)).astype(BF16)
        o2 = _dot(p, jnp.concatenate([v[:, hd * dh:(hd + 1) * dh], ones], axis=-1))
        heads.append((o2[:, :dh] / o2[:, dh:]).astype(BF16))
    o = jnp.concatenate(heads, axis=-1)
    c = _dot(o, wo_ref[...])
    o_ref[...] = x + _rms(c, g2_ref[...])


def mem_attention(x, g1, wq, kvmem, layer, wo, g2, *, seq, n_mem, update=None, tm=ROW_TILE):
    m, d = x.shape
    md = wq.shape[2]
    tiles_per_batch = seq // tm
    row = pl.BlockSpec((tm, d), lambda i: (i, 0))
    gain = pl.BlockSpec((1, d), lambda i: (0, 0))
    acts = [x] if update is None else [x, update[0], update[1].reshape(1, d)]
    return pl.pallas_call(
        functools.partial(_mem_attn_kernel, has_update=update is not None),
        grid=(m // tm,),
        in_specs=([row] if update is None else [row, row, gain]) + [
            pl.BlockSpec((1, d), lambda i: (0, 0)),
            _layer_spec(layer, (d, md), lambda i: (0, 0)),
            pl.BlockSpec((n_mem, md), lambda i: (i // tiles_per_batch, 2 * layer)),
            pl.BlockSpec((n_mem, md), lambda i: (i // tiles_per_batch, 2 * layer + 1)),
            _layer_spec(layer, (md, d), lambda i: (0, 0)),
            pl.BlockSpec((1, d), lambda i: (0, 0)),
        ],
        out_specs=row,
        out_shape=jax.ShapeDtypeStruct((m, d), F32),
        compiler_params=_params("parallel"),
        name="mem_attention",
    )(*acts, g1.reshape(1, d), wq, kvmem, kvmem, wo, g2.reshape(1, d))


def _conv_ffn_kernel(x_ref, g1_ref, wuv_ref, wug_ref, cwv_ref, cwg_ref, cbv_ref, cbg_ref, wd_ref,
                     g2_ref, o_ref, h_ref, acc_ref, tail_ref, up_ref, wd_bf_ref, *, nj,
                     tiles_per_batch):
    i = pl.program_id(0)
    j = pl.program_id(1)
    tm = x_ref.shape[0]
    hm = tm // FFN_ROW_SPLIT

    @pl.when(j == 0)
    def _():
        h_ref[...] = _rms(x_ref[...], g1_ref[...]).astype(h_ref.dtype)
        acc_ref[...] = jnp.zeros(acc_ref.shape, F32)

    @pl.when(i % tiles_per_batch == 0)
    def _():
        tail_ref[j] = jnp.zeros(tail_ref.shape[1:], F32)

    up_ref[:, 0:SUBLANES, :] = tail_ref[j]
    wd_bf_ref[...] = wd_ref[...].astype(BF16)

    for r in range(FFN_ROW_SPLIT):
        h = h_ref[r * hm:(r + 1) * hm, :]
        rows = slice(SUBLANES + r * hm, SUBLANES + (r + 1) * hm)
        up_ref[0, rows, :] = _dot(h, wuv_ref[...])
        up_ref[1, rows, :] = _dot(h, wug_ref[...])

    tail_ref[j] = up_ref[:, tm:tm + SUBLANES, :]

    def conv(slot, base, cw, cb):
        taps = [up_ref[slot, base - (CONV_W - 1 - k):base - (CONV_W - 1 - k) + hm, :]
                for k in range(CONV_W)]
        return cb + taps[0] * cw[0:1, :] + taps[1] * cw[1:2, :] + taps[2] * cw[2:3, :]

    for r in range(FFN_ROW_SPLIT):
        base = SUBLANES + r * hm
        val = conv(0, base, cwv_ref[...], cbv_ref[...])
        gate = conv(1, base, cwg_ref[...], cbg_ref[...])
        act = (val * (gate * _sigmoid(gate))).astype(BF16)
        acc_ref[r * hm:(r + 1) * hm, :] += _dot(act, wd_bf_ref[...])

    @pl.when(j == nj - 1)
    def _():
        o_ref[...] = x_ref[...] + _rms(acc_ref[...], g2_ref[...])


def conv_ffn(x, g1, w_up, conv_w, conv_b, w_down, layer, g2, *, seq, tm=ROW_TILE, tf=FFN_COL_TILE):
    m, d = x.shape
    f = w_down.shape[1]
    nj = f // tf
    tiles_per_batch = seq // tm
    conv_b = conv_b.reshape(conv_b.shape[0], 1, 2 * f)
    return pl.pallas_call(
        functools.partial(_conv_ffn_kernel, nj=nj, tiles_per_batch=tiles_per_batch),
        grid=(m // tm, nj),
        in_specs=[
            pl.BlockSpec((tm, d), lambda i, j: (i, 0)),
            pl.BlockSpec((1, d), lambda i, j: (0, 0)),
            _layer_spec(layer, (d, tf), lambda i, j: (0, j)),
            _layer_spec(layer, (d, tf), lambda i, j: (0, j + nj)),
            _layer_spec(layer, (CONV_W, tf), lambda i, j: (0, j)),
            _layer_spec(layer, (CONV_W, tf), lambda i, j: (0, j + nj)),
            _layer_spec(layer, (1, tf), lambda i, j: (0, j)),
            _layer_spec(layer, (1, tf), lambda i, j: (0, j + nj)),
            _layer_spec(layer, (tf, d), lambda i, j: (j, 0)),
            pl.BlockSpec((1, d), lambda i, j: (0, 0)),
        ],
        out_specs=pl.BlockSpec((tm, d), lambda i, j: (i, 0)),
        out_shape=jax.ShapeDtypeStruct((m, d), F32),
        scratch_shapes=[
            pltpu.VMEM((tm, d), BF16),
            pltpu.VMEM((tm, d), F32),
            pltpu.VMEM((nj, 2, SUBLANES, tf), F32),
            pltpu.VMEM((2, SUBLANES + tm, tf), F32),
            pltpu.VMEM((tf, d), BF16),
        ],
        compiler_params=_params("arbitrary", "arbitrary"),
        name="conv_ffn",
    )(x, g1.reshape(1, d), w_up, w_up, conv_w, conv_w, conv_b, conv_b, w_down, g2.reshape(1, d))


def kernel(x, mem, norm_mix, norm_mem, norm_ffn, mem_in_norm, a_w_in, a_lam_re, a_lam_im, a_log_dt,
           a_b_re, a_b_im, a_c_re, a_c_im, a_d, a_w_glu, kv_norm, w_k, w_v, b_w_q, b_rel_bias, b_w_o,
           m_w_q, m_w_kv, m_w_o, f_w_up, f_conv_w, f_conv_b, f_w_down):
    bsz, seq, d = x.shape
    depth = norm_mix.shape[0]
    n_a = a_w_in.shape[0]
    n_mem = mem.shape[1]
    md = m_w_q.shape[2]
    assert seq % ROW_TILE == 0 and seq % (ATT_QGROUP * ATT_CHUNK) == 0
    assert (seq // S5_CHUNK) % LANES == 0 and (d // S5_GROUP) % 2 == 0

    xf = x.reshape(bsz * seq, d)

    w_kv_mem = m_w_kv.astype(BF16)
    w_kv = jnp.stack([w_k, w_v]).astype(BF16)
    a_w_in_t = jnp.transpose(a_w_in, (0, 2, 1)).astype(BF16)
    a_w_glu_b = a_w_glu.astype(BF16)
    b_w_q_b = (b_w_q * (ATT_HEAD_DIM ** -0.5)).astype(BF16)
    b_w_o_b = b_w_o.astype(BF16)
    m_w_q_b = (m_w_q * (MEM_HEAD_DIM ** -0.5)).astype(BF16)
    m_w_o_b = m_w_o.astype(BF16)
    f_w_up_b = f_w_up.astype(BF16)
    flat = lambda a: a.reshape((-1,) + a.shape[2:])
    s5_ops = _s5_operators(flat(a_lam_re), flat(a_lam_im), flat(a_log_dt), flat(a_b_re),
                           flat(a_b_im), flat(a_c_re), flat(a_c_im), a_d, bsz)
    s5_ops = [o.reshape((n_a, o.shape[0] // n_a) + o.shape[1:]) for o in s5_ops[:3]] + [
        o.reshape(n_a, 1, -1) for o in s5_ops[3:]]
    att_bias = jax.vmap(_band_bias)(b_rel_bias)

    kvmem = norm_matmul(mem.reshape(bsz * n_mem, d), mem_in_norm, w_kv_mem, None)

    kv = None
    for l in range(depth):
        update = None
        if l == n_a:
            kv = norm_matmul(xf, kv_norm, w_kv, None)
        if l < n_a:
            ut = norm_matmul_t(xf, norm_mix[l, 0], a_w_in_t, l, bsz=bsz, seq=seq)
            yt = s5_mixer(ut, bsz, seq, s5_ops, l)
            update = (glu_gate(yt, a_w_glu_b, l, bsz=bsz, seq=seq), norm_mix[l, 1])
        else:
            jb = l - n_a
            q = norm_matmul(xf, norm_mix[l, 0], b_w_q_b, jb)
            o = band_attention(q, kv, att_bias, jb, bsz=bsz, seq=seq)
            xf = proj_res(o, b_w_o_b, jb, xf, norm_mix[l, 1])
        xf = mem_attention(xf, norm_mem[l, 0], m_w_q_b, kvmem, l, m_w_o_b, norm_mem[l, 1],
                           seq=seq, n_mem=n_mem, update=update)
        xf = conv_ffn(xf, norm_ffn[l, 0], f_w_up_b, f_conv_w, f_conv_b, f_w_down, l,
                      norm_ffn[l, 1], seq=seq)
    return xf.reshape(bsz, seq, d)
```
